```python
import jax, jax.numpy as jnp
from jax import lax
import numpy as np

D_MODEL = 1024
BATCH = 8
SEQ = 2048
DEPTH = 4
DEC_BATCH = 128
DEC_SEQ = 4
PAST_LEN = 16384
PAGE_SIZE = 128

N_AB = (DEPTH + 1) // 2
N_SSD = DEPTH // 2
ML_HEADS = 4
ML_DK = D_MODEL // 16
ML_DV = D_MODEL // 8
ML_QK = ML_HEADS * ML_DK
ML_V = ML_HEADS * ML_DV
GLA_HEADS = 4
GLA_DK = D_MODEL // 16
GLA_DV = D_MODEL // 8
GLA_QK = GLA_HEADS * GLA_DK
GLA_V = GLA_HEADS * GLA_DV
GLA_RANK = 16
GLA_TAU = 16.0
MIX_WIDTH = ML_V + GLA_V
AB_SPLITS = (ML_QK, ML_QK, ML_V, ML_V, ML_HEADS, ML_HEADS,
             GLA_QK, GLA_QK, GLA_V, GLA_V, GLA_RANK)
AB_IN = 2 * ML_QK + 2 * ML_V + 2 * ML_HEADS + 2 * GLA_QK + 2 * GLA_V + GLA_RANK
SSD_INNER = 2 * D_MODEL
SSD_HEADDIM = 64
SSD_HEADS = SSD_INNER // SSD_HEADDIM
SSD_STATE = 128
SSD_GROUPS = 4
SSD_HPG = SSD_HEADS // SSD_GROUPS
SSD_CONV = 4
SSD_CONV_DIM = SSD_INNER + 2 * SSD_GROUPS * SSD_STATE
SSD_IN = SSD_INNER + SSD_CONV_DIM + SSD_HEADS
D_FF = 4 * D_MODEL
CHUNK = 64
DN_ALPHA = (2 * DEPTH) ** 0.25
DN_BETA = (8 * DEPTH) ** -0.25
LN_EPS = 1e-5

kernel_name = "hybrid_mlstm_gla_ssd_deepnorm_step"


def _split(t, sizes):
    idx, acc = [], 0
    for s in sizes[:-1]:
        acc += s
        idx.append(acc)
    return jnp.split(t, idx, axis=-1)


def _chunk_len(T):
    return CHUNK if T % CHUNK == 0 else T


def _chunks(x, L):
    B, T = x.shape[:2]
    return jnp.moveaxis(x.reshape((B, T // L, L) + x.shape[2:]), 1, 0)


def _unchunk(y):
    NC, B, L = y.shape[:3]
    return jnp.moveaxis(y, 0, 1).reshape((B, NC * L) + y.shape[3:])


def _causal(L):
    return jnp.tril(jnp.ones((L, L), dtype=bool))


def _layernorm(x, g, b):
    xf = x.astype(jnp.float32)
    mu = xf.mean(-1, keepdims=True)
    var = jnp.square(xf - mu).mean(-1, keepdims=True)
    y = (xf - mu) * lax.rsqrt(var + LN_EPS) * g.astype(jnp.float32) + b.astype(jnp.float32)
    return y.astype(x.dtype)


def _headnorm(h, g):
    mu = h.mean(-1, keepdims=True)
    var = jnp.square(h - mu).mean(-1, keepdims=True)
    return (h - mu) * lax.rsqrt(var + LN_EPS) * g.astype(jnp.float32)


def _mlstm_chunk(carry, inp):
    C0, n0, m0 = carry
    q, k, v, ig, lf = inp
    L = q.shape[1]
    b = jnp.cumsum(lf, axis=1)
    g = b + m0[:, None, :]
    D = b[:, :, None, :] - b[:, None, :, :] + ig[:, None, :, :]
    D = jnp.where(_causal(L)[None, :, :, None], D, -jnp.inf)
    m = jnp.maximum(g, D.max(axis=2))
    w_intra = jnp.exp(D - m[:, :, None, :])
    w_inter = jnp.exp(g - m)
    s = jnp.einsum('bthk,bshk->btsh', q, k) * w_intra
    num = jnp.einsum('btsh,bshv->bthv', s, v) + w_inter[..., None] * jnp.einsum('bthk,bhkv->bthv', q, C0)
    den = s.sum(axis=2) + w_inter * jnp.einsum('bthk,bhk->bth', q, n0)
    h = num / jnp.maximum(jnp.abs(den), jnp.exp(-m))[..., None]
    mL = m[:, -1]
    wL = jnp.exp(D[:, -1] - mL[:, None, :])
    wL0 = jnp.exp(g[:, -1] - mL)
    C = wL0[..., None, None] * C0 + jnp.einsum('bsh,bshk,bshv->bhkv', wL, k, v)
    n = wL0[..., None] * n0 + jnp.einsum('bsh,bshk->bhk', wL, k)
    return (C, n, mL), h


def _gla_chunk(S0, inp):
    q, k, v, la = inp
    L = q.shape[1]
    A = jnp.cumsum(la, axis=1)
    diff = A[:, :, None] - A[:, None, :]
    diff = jnp.where(_causal(L)[None, :, :, None, None], diff, -jnp.inf)
    s = jnp.einsum('bthk,bshk,btshk->btsh', q, k, jnp.exp(diff))
    o = jnp.einsum('btsh,bshv->bthv', s, v) + jnp.einsum('bthk,bhkv->bthv', q * jnp.exp(A), S0)
    AL = A[:, -1]
    kd = k * jnp.exp(AL[:, None] - A)
    S = jnp.exp(AL)[..., None] * S0 + jnp.einsum('bshk,bshv->bhkv', kd, v)
    return S, o


def _ssd_chunk(h0, inp):
    x, Bm, Cm, dt, a = inp
    L = x.shape[1]
    cs = jnp.cumsum(a, axis=1)
    seg = cs[:, :, None] - cs[:, None]
    seg = jnp.where(_causal(L)[None, :, :, None, None], seg, -jnp.inf)
    w = jnp.exp(seg) * dt[:, None]
    CB = jnp.einsum('btgn,bsgn->btsg', Cm, Bm)
    y = jnp.einsum('btsg,btsgr,bsgrp->btgrp', CB, w, x)
    y = y + jnp.einsum('btgn,bgrpn->btgrp', Cm, h0) * jnp.exp(cs)[..., None]
    csL = cs[:, -1]
    wL = jnp.exp(csL[:, None] - cs) * dt
    h = jnp.exp(csL)[..., None, None] * h0 + jnp.einsum('bsgr,bsgrp,bsgn->bgrpn', wL, x, Bm)
    return h, y


def _ab_mixer(x, C0, n0, m0, S0, w_in, ig_bias, fg_bias, ml_norm, wa2, ba, gla_norm, w_out):
    B, T, _ = x.shape
    L = _chunk_len(T)
    f32 = jnp.float32
    mq, mk, mv, mo, mi, mf, gq, gk, gv, gg, ga = _split((x @ w_in).astype(f32), AB_SPLITS)
    q = mq.reshape(B, T, ML_HEADS, ML_DK)
    k = mk.reshape(B, T, ML_HEADS, ML_DK) * (ML_DK ** -0.5)
    v = mv.reshape(B, T, ML_HEADS, ML_DV)
    ig = mi + ig_bias.astype(f32)
    lf = jax.nn.log_sigmoid(mf + fg_bias.astype(f32))
    (C, n, m), h = lax.scan(_mlstm_chunk, (C0.astype(f32), n0.astype(f32), m0.astype(f32)),
                            (_chunks(q, L), _chunks(k, L), _chunks(v, L), _chunks(ig, L), _chunks(lf, L)))
    h = _headnorm(_unchunk(h), ml_norm.reshape(ML_HEADS, ML_DV))
    ml_out = h.reshape(B, T, ML_V) * jax.nn.sigmoid(mo)
    q2 = gq.reshape(B, T, GLA_HEADS, GLA_DK) * (GLA_DK ** -0.5)
    k2 = gk.reshape(B, T, GLA_HEADS, GLA_DK)
    v2 = gv.reshape(B, T, GLA_HEADS, GLA_DV)
    la = jax.nn.log_sigmoid(ga @ wa2.astype(f32) + ba.astype(f32)) / GLA_TAU
    la = la.reshape(B, T, GLA_HEADS, GLA_DK)
    S, o = lax.scan(_gla_chunk, S0.astype(f32),
                    (_chunks(q2, L), _chunks(k2, L), _chunks(v2, L), _chunks(la, L)))
    o = _headnorm(_unchunk(o), gla_norm.reshape(GLA_HEADS, GLA_DV))
    gla_out = o.reshape(B, T, GLA_V) * jax.nn.silu(gg)
    mix = jnp.concatenate([ml_out, gla_out], axis=-1).astype(x.dtype) @ w_out
    return mix, C.astype(C0.dtype), n.astype(n0.dtype), m.astype(m0.dtype), S.astype(S0.dtype)


def _ssd_mixer(x, h0, conv0, w_in, conv_w, conv_b, dt_bias, a_log, d_skip, norm_g, w_out):
    B, T, _ = x.shape
    L = _chunk_len(T)
    f32 = jnp.float32
    z, xbc, dtr = _split(x @ w_in, (SSD_INNER, SSD_CONV_DIM, SSD_HEADS))
    cat = jnp.concatenate([conv0.astype(xbc.dtype), xbc], axis=1)
    conv = conv_b.astype(f32)
    for w in range(SSD_CONV):
        conv = conv + cat[:, w:w + T].astype(f32) * conv_w[w].astype(f32)
    new_conv = cat[:, T:]
    xbc = jax.nn.silu(conv)
    xs, Bm, Cm = _split(xbc, (SSD_INNER, SSD_GROUPS * SSD_STATE, SSD_GROUPS * SSD_STATE))
    xs = xs.reshape(B, T, SSD_GROUPS, SSD_HPG, SSD_HEADDIM)
    Bm = Bm.reshape(B, T, SSD_GROUPS, SSD_STATE)
    Cm = Cm.reshape(B, T, SSD_GROUPS, SSD_STATE)
    dt = jax.nn.softplus(dtr.astype(f32) + dt_bias.astype(f32)).reshape(B, T, SSD_GROUPS, SSD_HPG)
    A = -jnp.exp(a_log.astype(f32)).reshape(SSD_GROUPS, SSD_HPG)
    h0f = h0.astype(f32).reshape(B, SSD_GROUPS, SSD_HPG, SSD_HEADDIM, SSD_STATE)
    h, y = lax.scan(_ssd_chunk, h0f,
                    (_chunks(xs, L), _chunks(Bm, L), _chunks(Cm, L), _chunks(dt, L), _chunks(dt * A, L)))
    y = _unchunk(y) + d_skip.astype(f32).reshape(SSD_GROUPS, SSD_HPG)[..., None] * xs
    y = y.reshape(B, T, SSD_INNER) * jax.nn.silu(z.astype(f32))
    yg = y.reshape(B, T, SSD_GROUPS, SSD_INNER // SSD_GROUPS)
    yg = yg * lax.rsqrt(jnp.square(yg).mean(-1, keepdims=True) + LN_EPS)
    y = yg.reshape(B, T, SSD_INNER) * norm_g.astype(f32)
    mix = y.astype(x.dtype) @ w_out
    h = h.reshape(B, SSD_HEADS, SSD_HEADDIM, SSD_STATE).astype(h0.dtype)
    return mix, h, new_conv.astype(conv0.dtype)


def _mlp(x, w1, w2):
    return jnp.square(jax.nn.relu(x @ w1)) @ w2


def _trunk(x, mC, mn, mm, gS, sh, sconv, P):
    nC, nn_, nm, nS, nh, nconv = [], [], [], [], [], []
    for l in range(DEPTH):
        j = l // 2
        if l % 2 == 0:
            mix, c, n, m, s = _ab_mixer(x, mC[j], mn[j], mm[j], gS[j], P['ab_w_in'][j], P['ab_ig_bias'][j],
                                        P['ab_fg_bias'][j], P['ab_ml_norm'][j], P['ab_gla_wa2'][j],
                                        P['ab_gla_ba'][j], P['ab_gla_norm'][j], P['ab_w_out'][j])
            nC.append(c); nn_.append(n); nm.append(m); nS.append(s)
        else:
            mix, h, cv = _ssd_mixer(x, sh[j], sconv[j], P['ssd_w_in'][j], P['ssd_conv_w'][j], P['ssd_conv_b'][j],
                                    P['ssd_dt_bias'][j], P['ssd_a_log'][j], P['ssd_d'][j], P['ssd_norm'][j],
                                    P['ssd_w_out'][j])
            nh.append(h); nconv.append(cv)
        x = _layernorm(DN_ALPHA * x + mix, P['ln_mix_g'][l], P['ln_mix_b'][l])
        x = _layernorm(DN_ALPHA * x + _mlp(x, P['mlp_w1'][l], P['mlp_w2'][l]), P['ln_mlp_g'][l], P['ln_mlp_b'][l])
    return x, jnp.stack(nC), jnp.stack(nn_), jnp.stack(nm), jnp.stack(nS), jnp.stack(nh), jnp.stack(nconv)


def setup_inputs(seed: int = 0) -> dict:
    key = jax.random.key(seed)
    keys = jax.random.split(key, 40)
    cnt = [0]

    def nk():
        cnt[0] += 1
        return keys[cnt[0] - 1]

    def nrm(shape, scale):
        return scale * jax.random.normal(nk(), shape, jnp.float32)

    def gain(shape):
        return 1.0 + 0.02 * jax.random.normal(nk(), shape, jnp.float32)

    u = jax.random.uniform(nk(), (N_SSD, SSD_HEADS), jnp.float32)
    dt0 = jnp.exp(np.log(1e-3) + u * (np.log(0.1) - np.log(1e-3)))
    dt_bias = dt0 + jnp.log(-jnp.expm1(-dt0))
    a_log = jnp.log(jax.random.uniform(nk(), (N_SSD, SSD_HEADS), jnp.float32, 1.0, 16.0))
    return {
        'x_prompt': nrm((BATCH, SEQ, D_MODEL), 1.0),
        'x_sample': nrm((DEC_BATCH, DEC_SEQ, D_MODEL), 1.0),
        'state_mlstm_C': nrm((N_AB, DEC_BATCH, ML_HEADS, ML_DK, ML_DV), 1.0),
        'state_mlstm_n': nrm((N_AB, DEC_BATCH, ML_HEADS, ML_DK), 1.0),
        'state_mlstm_m': nrm((N_AB, DEC_BATCH, ML_HEADS), 1.0),
        'state_gla_S': nrm((N_AB, DEC_BATCH, GLA_HEADS, GLA_DK, GLA_DV), 0.5),
        'state_ssd_h': nrm((N_SSD, DEC_BATCH, SSD_HEADS, SSD_HEADDIM, SSD_STATE), 0.5),
        'state_ssd_conv': nrm((N_SSD, DEC_BATCH, SSD_CONV - 1, SSD_CONV_DIM), 1.0),
        'ab_w_in': nrm((N_AB, D_MODEL, AB_IN), D_MODEL ** -0.5),
        'ab_ig_bias': nrm((N_AB, ML_HEADS), 0.1),
        'ab_fg_bias': 3.0 + nrm((N_AB, ML_HEADS), 0.5),
        'ab_ml_norm': gain((N_AB, ML_V)),
        'ab_gla_wa2': nrm((N_AB, GLA_RANK, GLA_QK), GLA_RANK ** -0.5),
        'ab_gla_ba': nrm((N_AB, GLA_QK), 0.1),
        'ab_gla_norm': gain((N_AB, GLA_V)),
        'ab_w_out': nrm((N_AB, MIX_WIDTH, D_MODEL), MIX_WIDTH ** -0.5 * DN_BETA),
        'ssd_w_in': nrm((N_SSD, D_MODEL, SSD_IN), D_MODEL ** -0.5),
        'ssd_conv_w': nrm((N_SSD, SSD_CONV, SSD_CONV_DIM), SSD_CONV ** -0.5),
        'ssd_conv_b': nrm((N_SSD, SSD_CONV_DIM), 0.01),
        'ssd_dt_bias': dt_bias,
        'ssd_a_log': a_log,
        'ssd_d': 1.0 + nrm((N_SSD, SSD_HEADS), 0.1),
        'ssd_norm': gain((N_SSD, SSD_INNER)),
        'ssd_w_out': nrm((N_SSD, SSD_INNER, D_MODEL), SSD_INNER ** -0.5 * DN_BETA),
        'mlp_w1': nrm((DEPTH, D_MODEL, D_FF), D_MODEL ** -0.5),
        'mlp_w2': nrm((DEPTH, D_FF, D_MODEL), D_FF ** -0.5 * DN_BETA),
        'ln_mix_g': gain((DEPTH, D_MODEL)),
        'ln_mix_b': nrm((DEPTH, D_MODEL), 0.01),
        'ln_mlp_g': gain((DEPTH, D_MODEL)),
        'ln_mlp_b': nrm((DEPTH, D_MODEL), 0.01),
    }


def reference(x_prompt, x_sample, state_mlstm_C, state_mlstm_n, state_mlstm_m, state_gla_S, state_ssd_h,
              state_ssd_conv, ab_w_in, ab_ig_bias, ab_fg_bias, ab_ml_norm, ab_gla_wa2, ab_gla_ba, ab_gla_norm,
              ab_w_out, ssd_w_in, ssd_conv_w, ssd_conv_b, ssd_dt_bias, ssd_a_log, ssd_d, ssd_norm, ssd_w_out,
              mlp_w1, mlp_w2, ln_mix_g, ln_mix_b, ln_mlp_g, ln_mlp_b):
    P = {'ab_w_in': ab_w_in, 'ab_ig_bias': ab_ig_bias, 'ab_fg_bias': ab_fg_bias, 'ab_ml_norm': ab_ml_norm,
         'ab_gla_wa2': ab_gla_wa2, 'ab_gla_ba': ab_gla_ba, 'ab_gla_norm': ab_gla_norm, 'ab_w_out': ab_w_out,
         'ssd_w_in': ssd_w_in, 'ssd_conv_w': ssd_conv_w, 'ssd_conv_b': ssd_conv_b, 'ssd_dt_bias': ssd_dt_bias,
         'ssd_a_log': ssd_a_log, 'ssd_d': ssd_d, 'ssd_norm': ssd_norm, 'ssd_w_out': ssd_w_out,
         'mlp_w1': mlp_w1, 'mlp_w2': mlp_w2, 'ln_mix_g': ln_mix_g, 'ln_mix_b': ln_mix_b,
         'ln_mlp_g': ln_mlp_g, 'ln_mlp_b': ln_mlp_b}
    Bp = x_prompt.shape[0]
    dtp = x_prompt.dtype
    z_C = jnp.zeros((N_AB, Bp, ML_HEADS, ML_DK, ML_DV), dtp)
    z_n = jnp.zeros((N_AB, Bp, ML_HEADS, ML_DK), dtp)
    z_m = jnp.zeros((N_AB, Bp, ML_HEADS), dtp)
    z_S = jnp.zeros((N_AB, Bp, GLA_HEADS, GLA_DK, GLA_DV), dtp)
    z_h = jnp.zeros((N_SSD, Bp, SSD_HEADS, SSD_HEADDIM, SSD_STATE), dtp)
    z_cv = jnp.zeros((N_SSD, Bp, SSD_CONV - 1, SSD_CONV_DIM), dtp)
    y_prompt, p_C, p_n, p_m, p_S, p_h, p_cv = _trunk(x_prompt, z_C, z_n, z_m, z_S, z_h, z_cv, P)
    y_sample, s_C, s_n, s_m, s_S, s_h, s_cv = _trunk(x_sample, state_mlstm_C, state_mlstm_n, state_mlstm_m,
                                                     state_gla_S, state_ssd_h, state_ssd_conv, P)
    return (y_prompt, y_sample, p_C, p_n, p_m, p_S, p_h, p_cv, s_C, s_n, s_m, s_S, s_h, s_cv)
```

```python
import functools
import math

import jax
import jax.numpy as jnp
import numpy as np
from jax import lax
from jax.experimental import pallas as pl
from jax.experimental.pallas import tpu as pltpu

F32 = jnp.float32
BF16 = jnp.bfloat16

LN_EPS = 1e-5
GLA_TAU = 16.0
LANES = 128
SUBLANES = 8
VMEM_LIMIT = 48 * 1024 * 1024


def _dot(a, b):
    return jnp.dot(a.astype(BF16), b.astype(BF16), preferred_element_type=F32)


def _dot_nt(a, b):
    return lax.dot_general(a.astype(BF16), b.astype(BF16), (((1,), (1,)), ((), ())),
                           preferred_element_type=F32)


def _dot_tn(a, b):
    return lax.dot_general(a.astype(BF16), b.astype(BF16), (((0,), (0,)), ((), ())),
                           preferred_element_type=F32)


def _split3(x):
    hi = x.astype(BF16)
    r = x - hi.astype(F32)
    mid = r.astype(BF16)
    lo = (r - mid.astype(F32)).astype(BF16)
    return hi, mid, lo


def _dot3(t, x):
    hi, mid, lo = _split3(x)
    f = lambda p: jnp.dot(t, p, preferred_element_type=F32)
    return f(hi) + f(mid) + f(lo)


def _dot3_right(x, t):
    hi, mid, lo = _split3(x)
    f = lambda p: jnp.dot(p, t, preferred_element_type=F32)
    return f(hi) + f(mid) + f(lo)


def _softplus(x):
    return jnp.maximum(x, 0.0) + jnp.log1p(jnp.exp(-jnp.abs(x)))


def _log_sigmoid(x):
    return -_softplus(-x)


def _sigmoid(x):
    return 1.0 / (1.0 + jnp.exp(-x))


def _layernorm_rows(r, g, b):
    mu = jnp.mean(r, axis=1, keepdims=True)
    d = r - mu
    var = jnp.mean(d * d, axis=1, keepdims=True)
    return d * lax.rsqrt(var + LN_EPS) * g + b


def _headnorm(h, g):
    mu = jnp.mean(h, axis=1, keepdims=True)
    d = h - mu
    var = jnp.mean(d * d, axis=1, keepdims=True)
    return d * lax.rsqrt(var + LN_EPS) * g


def _cumsum_mats(L):
    t = np.arange(L)[:, None]
    j = np.arange(L)[None, :]
    upper = (t <= j)
    lower = (j <= t)
    return upper, lower


def _ab_mats(L):
    nlev = int(round(math.log2(L)))
    assert 1 << nlev == L
    upper, lower = _cumsum_mats(L)
    t = np.arange(L)[:, None]
    j = np.arange(L)[None, :]
    mats = [upper, lower]
    for i in range(nlev):
        n = L >> (i + 1)
        mid = (t // (2 * n)) * (2 * n) + n - 1
        second = (t % (2 * n)) >= n
        m = np.where(second, (j > mid) & (j <= t), (j > t) & (j <= mid))
        mats.append(m)
    return jnp.asarray(np.concatenate(mats, axis=0).astype(np.float32), dtype=BF16), nlev


def _ssd_mats(L):
    upper, lower = _cumsum_mats(L)
    return jnp.asarray(np.concatenate([upper, lower], axis=0).astype(np.float32), dtype=BF16)


def _proj_kernel(x_ref, w_ref, o_ref):
    o_ref[...] = jnp.dot(x_ref[...].astype(BF16), w_ref[...], preferred_element_type=F32)


def _proj(x, w, tm, tn):
    M, K = x.shape
    N = w.shape[1]
    assert M % tm == 0 and N % tn == 0
    return pl.pallas_call(
        _proj_kernel,
        grid=(N // tn, M // tm),
        in_specs=[pl.BlockSpec((tm, K), lambda j, i: (i, 0)),
                  pl.BlockSpec((K, tn), lambda j, i: (0, j))],
        out_specs=pl.BlockSpec((tm, tn), lambda j, i: (i, j)),
        out_shape=jax.ShapeDtypeStruct((M, N), F32),
        compiler_params=pltpu.CompilerParams(
            dimension_semantics=("parallel", "parallel"), vmem_limit_bytes=VMEM_LIMIT),
        name="proj",
    )(x, w)


def _outproj_ln_kernel(alpha, n_in, *refs):
    ys = refs[:n_in]
    ws = refs[n_in:2 * n_in]
    x_ref, g_ref, b_ref, o_ref = refs[2 * n_in:]
    r = alpha * x_ref[...]
    for y_ref, w_ref in zip(ys, ws):
        r = r + jnp.dot(y_ref[...].astype(BF16), w_ref[...], preferred_element_type=F32)
    o_ref[...] = _layernorm_rows(r, g_ref[...], b_ref[...])


def _outproj_ln(ys, ws, x, g, b, alpha, tm):
    M, D = x.shape
    assert M % tm == 0
    n_in = len(ys)
    in_specs = ([pl.BlockSpec((tm, y.shape[1]), lambda i: (i, 0)) for y in ys]
                + [pl.BlockSpec(w.shape, lambda i: (0, 0)) for w in ws]
                + [pl.BlockSpec((tm, D), lambda i: (i, 0)),
                   pl.BlockSpec((1, D), lambda i: (0, 0)),
                   pl.BlockSpec((1, D), lambda i: (0, 0))])
    return pl.pallas_call(
        functools.partial(_outproj_ln_kernel, alpha, n_in),
        grid=(M // tm,),
        in_specs=in_specs,
        out_specs=pl.BlockSpec((tm, D), lambda i: (i, 0)),
        out_shape=jax.ShapeDtypeStruct((M, D), F32),
        compiler_params=pltpu.CompilerParams(
            dimension_semantics=("parallel",), vmem_limit_bytes=VMEM_LIMIT),
        name="outproj_ln",
    )(*ys, *ws, x, g, b)


def _mlp_kernel(alpha, nf, x_ref, w1_ref, w2_ref, g_ref, b_ref, o_ref, acc_ref):
    f = pl.program_id(1)

    @pl.when(f == 0)
    def _():
        acc_ref[...] = jnp.zeros_like(acc_ref)

    h = jnp.dot(x_ref[...].astype(BF16), w1_ref[...], preferred_element_type=F32)
    h = jnp.square(jnp.maximum(h, 0.0))
    acc_ref[...] += jnp.dot(h.astype(BF16), w2_ref[...], preferred_element_type=F32)

    @pl.when(f == nf - 1)
    def _():
        r = alpha * x_ref[...] + acc_ref[...]
        o_ref[...] = _layernorm_rows(r, g_ref[...], b_ref[...])


def _mlp_ln(x, w1, w2, g, b, alpha, tm, tf):
    M, D = x.shape
    Fdim = w1.shape[1]
    assert M % tm == 0 and Fdim % tf == 0
    nf = Fdim // tf
    return pl.pallas_call(
        functools.partial(_mlp_kernel, alpha, nf),
        grid=(M // tm, nf),
        in_specs=[pl.BlockSpec((tm, D), lambda i, f: (i, 0)),
                  pl.BlockSpec((D, tf), lambda i, f: (0, f)),
                  pl.BlockSpec((tf, D), lambda i, f: (f, 0)),
                  pl.BlockSpec((1, D), lambda i, f: (0, 0)),
                  pl.BlockSpec((1, D), lambda i, f: (0, 0))],
        out_specs=pl.BlockSpec((tm, D), lambda i, f: (i, 0)),
        out_shape=jax.ShapeDtypeStruct((M, D), F32),
        scratch_shapes=[pltpu.VMEM((tm, D), F32)],
        compiler_params=pltpu.CompilerParams(
            dimension_semantics=("parallel", "arbitrary"), vmem_limit_bytes=VMEM_LIMIT),
        name="mlp_ln",
    )(x, w1, w2, g, b)


def _ab_kernel(L, valid, nlev, nc, dk,
               gb_ref, qk_ref, v_ref, mo_ref, gqk_ref, gv_ref, gg_ref, sm_ref, smt_ref,
               c0_ref, n0_ref, m0_ref, s0_ref, mln_ref, wa2_ref, ba_ref, gln_ref, tmat_ref,
               ml_ref, gla_ref, c_ref, n_ref, m_ref, s_ref,
               cs, ns, ms, ss):
    h = pl.program_id(1)
    c = pl.program_id(2)

    @pl.when(c == 0)
    def _():
        cs[...] = c0_ref[0, 0]
        ns[...] = n0_ref[0, 0]
        ms[...] = m0_ref[0, 0]
        ss[...] = s0_ref[0, 0]

    t_col = lax.broadcasted_iota(jnp.int32, (L, 1), 0)
    s_row = lax.broadcasted_iota(jnp.int32, (1, L), 1)
    tt = lax.broadcasted_iota(jnp.int32, (L, L), 0)
    sc = lax.broadcasted_iota(jnp.int32, (L, L), 1)
    causal = sc <= tt
    eye = sc == tt
    upper = tmat_ref[0:L, :]
    scale = dk ** -0.5

    def to_col(row):
        return jnp.sum(jnp.where(eye, row, 0.0), axis=1, keepdims=True)

    ig_row = smt_ref[0, pl.ds(h, 1), :] + gb_ref[h]
    lf_row = _log_sigmoid(smt_ref[0, pl.ds(4 + h, 1), :] + gb_ref[4 + h])
    if valid < L:
        ok = s_row < valid
        ig_row = jnp.where(ok, ig_row, -jnp.inf)
        lf_row = jnp.where(ok, lf_row, 0.0)
    b_row = _dot3_right(jnp.broadcast_to(lf_row, (SUBLANES, L)), upper)[0:1, :]
    b_col = to_col(b_row)
    ig_col = to_col(ig_row)
    m0 = ms[...]
    D = jnp.where(causal, b_col - b_row + ig_row, -jnp.inf)
    g_col = b_col + m0
    m_col = jnp.maximum(g_col, jnp.max(D, axis=1, keepdims=True))
    w_intra = jnp.exp(D - m_col)
    w_inter = jnp.exp(g_col - m_col)
    qk = qk_ref[...]
    q = qk[:, :dk]
    k = qk[:, dk:] * scale
    qb = q.astype(BF16)
    vb = v_ref[...].astype(BF16)
    s = _dot_nt(qb, k) * w_intra
    C0 = cs[...]
    n0 = ns[...]
    num = _dot(s, vb) + w_inter * _dot(qb, C0)
    den = jnp.sum(s, axis=1, keepdims=True) + w_inter * jnp.sum(q * n0, axis=1, keepdims=True)
    hh = num / jnp.maximum(jnp.abs(den), jnp.exp(-m_col))
    ml_ref[...] = _headnorm(hh, mln_ref[...]) * _sigmoid(mo_ref[...])

    mL = m_col[L - 1:L, :]
    wL_col = jnp.exp(b_col[L - 1:L, :] - b_col + ig_col - mL)
    wL0 = jnp.exp(g_col[L - 1:L, :] - mL)
    kw = k * wL_col
    cs[...] = wL0 * C0 + _dot_tn(kw, vb)
    ns[...] = wL0 * n0 + jnp.sum(kw, axis=0, keepdims=True)
    ms[...] = mL

    gqk = gqk_ref[...]
    q2 = gqk[:, :dk] * scale
    k2 = gqk[:, dk:]
    v2b = gv_ref[...].astype(BF16)
    la = _log_sigmoid(_dot(sm_ref[...], wa2_ref[0]) + ba_ref[0]) * (1.0 / GLA_TAU)
    if valid < L:
        la = jnp.where(t_col < valid, la, 0.0)
    TL = _dot3(tmat_ref[L:(2 + nlev) * L, :], la)
    A = TL[0:L]
    scores = jnp.where(eye, _dot_nt(q2, k2), 0.0)
    for i in range(nlev):
        n = L >> (i + 1)
        En = jnp.exp(TL[(1 + i) * L:(2 + i) * L])
        second = (t_col & n) != 0
        X = jnp.where(second, q2, k2) * En
        sn = _dot_nt(jnp.where(second, X, 0.0), jnp.where(second, 0.0, X))
        if i > 0:
            sh = int(round(math.log2(2 * n)))
            sn = jnp.where((tt >> sh) == (sc >> sh), sn, 0.0)
        scores = scores + sn
    S0 = ss[...]
    o = _dot(scores, v2b) + _dot(q2 * jnp.exp(A), S0)
    gg = gg_ref[...]
    gla_ref[...] = _headnorm(o, gln_ref[...]) * (gg * _sigmoid(gg))

    AL = A[L - 1:L, :]
    kd = k2 * jnp.exp(AL - A)
    kk = lax.broadcasted_iota(jnp.int32, (dk, dk), 0)
    kj = lax.broadcasted_iota(jnp.int32, (dk, dk), 1)
    dec_col = jnp.sum(jnp.where(kk == kj, jnp.exp(AL), 0.0), axis=1, keepdims=True)
    ss[...] = dec_col * S0 + _dot_tn(kd, v2b)

    @pl.when(c == nc - 1)
    def _():
        c_ref[0, 0] = cs[...]
        n_ref[0, 0] = ns[...]
        m_ref[0, 0] = ms[...]
        s_ref[0, 0] = ss[...]


def _ab_scan(P, B, nc, L, valid, C0, n0, m0, S0, prm):
    M = P.shape[0]
    H = C0.shape[1]
    dk, dv = C0.shape[2], C0.shape[3]
    assert M == B * nc * L and dv == LANES and 2 * dk == LANES
    tmat, nlev = _ab_mats(L)
    nsm = 6 * H
    smt = P[:, nsm * LANES:nsm * LANES + 8].reshape(B * nc, L, 8).transpose(0, 2, 1)
    row = lambda b, h, c: b * nc + c
    colblk = lambda base: pl.BlockSpec((L, LANES), lambda b, h, c: (row(b, h, c), base + h))
    st4 = lambda shp: pl.BlockSpec((1, 1) + shp, lambda b, h, c: (b, h, 0, 0))
    in_specs = [
        pl.BlockSpec(memory_space=pltpu.SMEM),
        colblk(0), colblk(H), colblk(2 * H), colblk(3 * H), colblk(4 * H), colblk(5 * H),
        pl.BlockSpec((L, LANES), lambda b, h, c: (row(b, h, c), nsm)),
        pl.BlockSpec((1, 8, L), lambda b, h, c: (row(b, h, c), 0, 0)),
        st4((dk, dv)), st4((1, dk)), st4((1, 1)), st4((dk, dv)),
        pl.BlockSpec((1, dv), lambda b, h, c: (0, h)),
        pl.BlockSpec((1, LANES, dk), lambda b, h, c: (h, 0, 0)),
        pl.BlockSpec((1, 1, dk), lambda b, h, c: (h, 0, 0)),
        pl.BlockSpec((1, dv), lambda b, h, c: (0, h)),
        pl.BlockSpec(tmat.shape, lambda b, h, c: (0, 0)),
    ]
    out_specs = [
        pl.BlockSpec((L, dv), lambda b, h, c: (row(b, h, c), h)),
        pl.BlockSpec((L, dv), lambda b, h, c: (row(b, h, c), h)),
        st4((dk, dv)), st4((1, dk)), st4((1, 1)), st4((dk, dv)),
    ]
    out_shape = [
        jax.ShapeDtypeStruct((M, H * dv), F32), jax.ShapeDtypeStruct((M, H * dv), F32),
        jax.ShapeDtypeStruct((B, H, dk, dv), F32), jax.ShapeDtypeStruct((B, H, 1, dk), F32),
        jax.ShapeDtypeStruct((B, H, 1, 1), F32), jax.ShapeDtypeStruct((B, H, dk, dv), F32),
    ]
    return pl.pallas_call(
        functools.partial(_ab_kernel, L, valid, nlev, nc, dk),
        grid=(B, H, nc),
        in_specs=in_specs,
        out_specs=out_specs,
        out_shape=out_shape,
        scratch_shapes=[pltpu.VMEM((dk, dv), F32), pltpu.VMEM((1, dk), F32),
                        pltpu.VMEM((1, 1), F32), pltpu.VMEM((dk, dv), F32)],
        compiler_params=pltpu.CompilerParams(
            dimension_semantics=("parallel", "parallel", "arbitrary"),
            vmem_limit_bytes=VMEM_LIMIT),
        name="ab_scan",
    )(prm["gate_bias"], P, P, P, P, P, P, P, smt,
      C0, n0.reshape(B, H, 1, dk), m0.reshape(B, H, 1, 1), S0,
      prm["ml_norm"], prm["wa2"], prm["ba"], prm["gla_norm"], tmat)


def _ssd_kernel(L, valid, nc, hpg, hd,
                z_ref, x_ref, b_ref, c_ref, dt_ref, dtt_ref,
                cvx_ref, cvb_ref, cvc_ref, h0_ref,
                cwx_ref, cwb_ref, cwc_ref, cbx_ref, cbb_ref, cbc_ref,
                dtbr_ref, dtbc_ref, alr_ref, alc_ref, dsk_ref, nrm_ref, tmat_ref, exp_ref,
                y_ref, h_ref,
                catx, catb, catc, hs):
    cidx = pl.program_id(2)
    gw = hpg * hd
    W = cwx_ref.shape[0]
    P0 = SUBLANES - (W - 1)

    @pl.when(cidx == 0)
    def _():
        catx[P0:SUBLANES, :] = cvx_ref[0]
        catb[P0:SUBLANES, :] = cvb_ref[0]
        catc[P0:SUBLANES, :] = cvc_ref[0]
        hs[...] = h0_ref[0].reshape(gw, h0_ref.shape[3])

    catx[SUBLANES:SUBLANES + L, :] = x_ref[...]
    catb[SUBLANES:SUBLANES + L, :] = b_ref[...]
    catc[SUBLANES:SUBLANES + L, :] = c_ref[...]

    def conv_silu(cat, cw_ref, cb_ref):
        acc = cb_ref[...]
        for w in range(W):
            acc = acc + cat[P0 + w:P0 + w + L, :] * cw_ref[w:w + 1, :]
        return acc * _sigmoid(acc)

    xa = conv_silu(catx, cwx_ref, cbx_ref)
    Bm = conv_silu(catb, cwb_ref, cbb_ref)
    Cm = conv_silu(catc, cwc_ref, cbc_ref)
    if nc > 1:
        for cat in (catx, catb, catc):
            tail = cat[P0 + L:SUBLANES + L, :]
            cat[P0:SUBLANES, :] = tail

    t_col = lax.broadcasted_iota(jnp.int32, (L, 1), 0)
    s_row = lax.broadcasted_iota(jnp.int32, (1, L), 1)
    tt = lax.broadcasted_iota(jnp.int32, (L, L), 0)
    sc = lax.broadcasted_iota(jnp.int32, (L, L), 1)
    causal = sc <= tt

    dtc = _softplus(dt_ref[...] + dtbr_ref[...])
    dtr = _softplus(dtt_ref[0] + dtbc_ref[...])
    if valid < L:
        dtc = jnp.where(t_col < valid, dtc, 0.0)
        dtr = jnp.where(s_row < valid, dtr, 0.0)
    a_col = dtc * (-jnp.exp(alr_ref[...]))
    a_row = dtr * (-jnp.exp(alc_ref[...]))
    cs_col = _dot3(tmat_ref[L:2 * L, :], a_col)
    cs_row = _dot3_right(a_row, tmat_ref[0:L, :])
    expand = exp_ref[...]
    cs_exp = _dot3_right(cs_col, expand)
    dt_exp = _dot3_right(dtc, expand)

    CB = _dot_nt(Cm, Bm)
    hs0 = hs[...]
    y = _dot_nt(Cm, hs0) * jnp.exp(cs_exp)
    xab = xa.astype(BF16)
    lane = lax.broadcasted_iota(jnp.int32, (L, LANES), 1)
    per = LANES // hd
    parts = []
    for p in range(gw // LANES):
        xp = xab[:, p * LANES:(p + 1) * LANES]
        acc = None
        for u in range(per):
            j = p * per + u
            seg = jnp.where(causal, cs_col[:, j:j + 1] - cs_row[j:j + 1, :], -jnp.inf)
            mj = CB * (jnp.exp(seg) * dtr[j:j + 1, :])
            yj = _dot(mj, xp)
            acc = yj if acc is None else jnp.where(lane < u * hd, acc, yj)
        parts.append(acc)
    y = y + jnp.concatenate(parts, axis=1) + dsk_ref[...] * xa
    z = z_ref[...]
    y = y * (z * _sigmoid(z))
    y = y * lax.rsqrt(jnp.mean(y * y, axis=1, keepdims=True) + LN_EPS) * nrm_ref[...]
    y_ref[...] = y

    xw = xa * (jnp.exp(cs_exp[L - 1:L, :] - cs_exp) * dt_exp)
    upd = _dot_tn(xw, Bm)
    for j in range(hpg):
        dec = jnp.exp(cs_row[j:j + 1, L - 1:L])
        hs[j * hd:(j + 1) * hd, :] = dec * hs0[j * hd:(j + 1) * hd, :] + upd[j * hd:(j + 1) * hd, :]

    @pl.when(cidx == nc - 1)
    def _():
        h_ref[0] = hs[...].reshape(h_ref.shape[1:])


def _ssd_scan(P, B, nc, L, valid, h0, conv0, prm):
    M = P.shape[0]
    nh, hd, N = h0.shape[1], h0.shape[2], h0.shape[3]
    G = prm["groups"]
    hpg = nh // G
    gw = hpg * hd
    inner = nh * hd
    assert M == B * nc * L and N == LANES and gw % LANES == 0 and hpg == SUBLANES
    tmat = _ssd_mats(L)
    nx = inner // gw
    o_b = 2 * inner // LANES
    o_c = o_b + G
    o_dt = o_c + G
    dtt = (P[:, o_dt * LANES:(o_dt + G) * LANES].reshape(M, G, LANES)[:, :, :hpg]
           .reshape(B * nc, L, nh).transpose(0, 2, 1))
    W1 = conv0.shape[1]
    row = lambda b, g, c: b * nc + c
    in_specs = [
        pl.BlockSpec((L, gw), lambda b, g, c: (row(b, g, c), g)),
        pl.BlockSpec((L, gw), lambda b, g, c: (row(b, g, c), nx + g)),
        pl.BlockSpec((L, LANES), lambda b, g, c: (row(b, g, c), o_b + g)),
        pl.BlockSpec((L, LANES), lambda b, g, c: (row(b, g, c), o_c + g)),
        pl.BlockSpec((L, LANES), lambda b, g, c: (row(b, g, c), o_dt + g)),
        pl.BlockSpec((1, hpg, L), lambda b, g, c: (row(b, g, c), g, 0)),
        pl.BlockSpec((1, W1, gw), lambda b, g, c: (b, 0, g)),
        pl.BlockSpec((1, W1, LANES), lambda b, g, c: (b, 0, inner // LANES + g)),
        pl.BlockSpec((1, W1, LANES), lambda b, g, c: (b, 0, inner // LANES + G + g)),
        pl.BlockSpec((1, hpg, hd, N), lambda b, g, c: (b, g, 0, 0)),
        pl.BlockSpec((W1 + 1, gw), lambda b, g, c: (0, g)),
        pl.BlockSpec((W1 + 1, LANES), lambda b, g, c: (0, inner // LANES + g)),
        pl.BlockSpec((W1 + 1, LANES), lambda b, g, c: (0, inner // LANES + G + g)),
        pl.BlockSpec((1, gw), lambda b, g, c: (0, g)),
        pl.BlockSpec((1, LANES), lambda b, g, c: (0, inner // LANES + g)),
        pl.BlockSpec((1, LANES), lambda b, g, c: (0, inner // LANES + G + g)),
        pl.BlockSpec((1, LANES), lambda b, g, c: (0, g)),
        pl.BlockSpec((hpg, 1), lambda b, g, c: (g, 0)),
        pl.BlockSpec((1, LANES), lambda b, g, c: (0, g)),
        pl.BlockSpec((hpg, 1), lambda b, g, c: (g, 0)),
        pl.BlockSpec((1, gw), lambda b, g, c: (0, g)),
        pl.BlockSpec((1, gw), lambda b, g, c: (0, g)),
        pl.BlockSpec(tmat.shape, lambda b, g, c: (0, 0)),
        pl.BlockSpec((LANES, gw), lambda b, g, c: (0, 0)),
    ]
    out_specs = [
        pl.BlockSpec((L, gw), lambda b, g, c: (row(b, g, c), g)),
        pl.BlockSpec((1, hpg, hd, N), lambda b, g, c: (b, g, 0, 0)),
    ]
    out_shape = [jax.ShapeDtypeStruct((M, inner), F32), jax.ShapeDtypeStruct(h0.shape, F32)]
    return pl.pallas_call(
        functools.partial(_ssd_kernel, L, valid, nc, hpg, hd),
        grid=(B, G, nc),
        in_specs=in_specs,
        out_specs=out_specs,
        out_shape=out_shape,
        scratch_shapes=[pltpu.VMEM((SUBLANES + L, gw), F32), pltpu.VMEM((SUBLANES + L, LANES), F32),
                        pltpu.VMEM((SUBLANES + L, LANES), F32), pltpu.VMEM((gw, N), F32)],
        compiler_params=pltpu.CompilerParams(
            dimension_semantics=("parallel", "parallel", "arbitrary"),
            vmem_limit_bytes=VMEM_LIMIT),
        name="ssd_scan",
    )(P, P, P, P, P, dtt, conv0, conv0, conv0, h0,
      prm["conv_w"], prm["conv_w"], prm["conv_w"], prm["conv_b"], prm["conv_b"], prm["conv_b"],
      prm["dtb_row"], prm["dtb_col"], prm["alog_row"], prm["alog_col"],
      prm["dskip"], prm["norm"], tmat, prm["expand"])


def _ab_col_index(H, dk, dv, rank):
    qk, vv = H * dk, H * dv
    o = np.cumsum([0, qk, qk, vv, vv, H, H, qk, qk, vv, vv, rank])
    mq, mk, mv, mo, mi, mf, gq, gk, gv, gg, ga = o[:11]
    idx = []
    for h in range(H):
        idx += list(range(mq + h * dk, mq + (h + 1) * dk)) + list(range(mk + h * dk, mk + (h + 1) * dk))
    idx += list(range(mv, mv + vv)) + list(range(mo, mo + vv))
    for h in range(H):
        idx += list(range(gq + h * dk, gq + (h + 1) * dk)) + list(range(gk + h * dk, gk + (h + 1) * dk))
    idx += list(range(gv, gv + vv)) + list(range(gg, gg + vv))
    idx += list(range(mi, mi + H)) + list(range(mf, mf + H)) + list(range(ga, ga + rank))
    return np.asarray(idx, dtype=np.int32), int(o[11])


def _prep_ab(j, H, dk, dv, w_in, ig_bias, fg_bias, ml_norm, wa2, ba, gla_norm, w_out):
    rank = wa2.shape[1]
    idx, total = _ab_col_index(H, dk, dv, rank)
    assert total == w_in.shape[2]
    pad = (-len(idx)) % LANES
    w = jnp.pad(w_in[j][:, idx], ((0, 0), (0, pad))).astype(BF16)
    wa2p = jnp.zeros((LANES, H * dk), F32).at[2 * H:2 * H + rank, :].set(wa2[j])
    wa2p = wa2p.reshape(LANES, H, dk).transpose(1, 0, 2).astype(BF16)
    return {
        "w_in": w,
        "gate_bias": jnp.concatenate([ig_bias[j], fg_bias[j]]).astype(F32),
        "ml_norm": ml_norm[j].reshape(1, H * dv),
        "gla_norm": gla_norm[j].reshape(1, H * dv),
        "wa2": wa2p,
        "ba": ba[j].reshape(H, 1, dk),
        "w_out_ml": w_out[j][:H * dv].astype(BF16),
        "w_out_gla": w_out[j][H * dv:].astype(BF16),
    }


def _prep_ssd(j, G, nh, hd, N, w_in, conv_w, conv_b, dt_bias, a_log, d_skip, norm_g, w_out):
    inner = nh * hd
    hpg = nh // G
    cd = inner + 2 * G * N
    wz, wxbc, wdt = w_in[j][:, :inner], w_in[j][:, inner:inner + cd], w_in[j][:, inner + cd:]
    wdt = jnp.pad(wdt.reshape(-1, G, hpg), ((0, 0), (0, 0), (0, LANES - hpg))).reshape(-1, G * LANES)
    w = jnp.concatenate([wz, wxbc, wdt], axis=1).astype(BF16)
    lane_form = lambda v: jnp.pad(v.reshape(G, hpg), ((0, 0), (0, LANES - hpg))).reshape(1, G * LANES)
    e = np.zeros((LANES, hpg * hd), np.float32)
    for r in range(hpg):
        e[r, r * hd:(r + 1) * hd] = 1.0
    return {
        "groups": G,
        "w_in": w,
        "conv_w": conv_w[j], "conv_b": conv_b[j].reshape(1, cd),
        "dtb_row": lane_form(dt_bias[j]), "dtb_col": dt_bias[j].reshape(nh, 1),
        "alog_row": lane_form(a_log[j]), "alog_col": a_log[j].reshape(nh, 1),
        "dskip": jnp.repeat(d_skip[j], hd).reshape(1, inner),
        "norm": norm_g[j].reshape(1, inner),
        "expand": jnp.asarray(e, dtype=BF16),
        "w_out": w_out[j].astype(BF16),
    }


def _pad_time(P, B, T, Tp):
    if T == Tp:
        return P
    return jnp.pad(P.reshape(B, T, -1), ((0, 0), (0, Tp - T), (0, 0))).reshape(B * Tp, -1)


def _unpad_time(Y, B, T, Tp):
    if T == Tp:
        return Y
    return Y.reshape(B, Tp, -1)[:, :T].reshape(B * T, -1)


def _trunk(x, states, ab_prm, ssd_prm, mlp, lns, alpha, chunk):
    B, T, D = x.shape
    mC, mn, mm, gS, sh, sconv = states
    if T % chunk == 0:
        L, nc = chunk, T // chunk
    else:
        L, nc = max(SUBLANES, 1 << int(math.ceil(math.log2(T)))), 1
    Tp = nc * L
    M = B * T
    tm = 512 if M % 512 == 0 else M
    tmm = 1024 if M % 1024 == 0 else tm
    X = x.reshape(M, D)
    nC, nn_, nm, nS, nh, ncv = [], [], [], [], [], []
    depth = len(mlp)
    for l in range(depth):
        j = l // 2
        if l % 2 == 0:
            p = ab_prm[j]
            P = _proj(X, p["w_in"], tm, p["w_in"].shape[1])
            ml, gla, c, n, m, s = _ab_scan(_pad_time(P, B, T, Tp), B, nc, L, T if nc == 1 else L,
                                           mC[j], mn[j], mm[j], gS[j], p)
            nC.append(c)
            nn_.append(n.reshape(mn[j].shape))
            nm.append(m.reshape(mm[j].shape))
            nS.append(s)
            ys = [_unpad_time(ml, B, T, Tp), _unpad_time(gla, B, T, Tp)]
            ws = [p["w_out_ml"], p["w_out_gla"]]
        else:
            p = ssd_prm[j]
            P = _proj(X, p["w_in"], tm, p["w_in"].shape[1] // 2)
            inner = p["norm"].shape[1]
            cd = p["conv_b"].shape[1]
            W1 = sconv[j].shape[1]
            assert T >= W1
            ncv.append(P[:, inner:inner + cd].reshape(B, T, cd)[:, T - W1:])
            y, h = _ssd_scan(_pad_time(P, B, T, Tp), B, nc, L, T if nc == 1 else L, sh[j], sconv[j], p)
            nh.append(h)
            ys = [_unpad_time(y, B, T, Tp)]
            ws = [p["w_out"]]
        g1, b1, g2, b2 = lns[l]
        X = _outproj_ln(ys, ws, X, g1, b1, alpha, tm)
        w1, w2 = mlp[l]
        X = _mlp_ln(X, w1, w2, g2, b2, alpha, tmm, 1024)
    return (X.reshape(B, T, D), jnp.stack(nC), jnp.stack(nn_), jnp.stack(nm), jnp.stack(nS),
            jnp.stack(nh), jnp.stack(ncv))


def kernel(x_prompt, x_sample, state_mlstm_C, state_mlstm_n, state_mlstm_m, state_gla_S, state_ssd_h,
           state_ssd_conv, ab_w_in, ab_ig_bias, ab_fg_bias, ab_ml_norm, ab_gla_wa2, ab_gla_ba, ab_gla_norm,
           ab_w_out, ssd_w_in, ssd_conv_w, ssd_conv_b, ssd_dt_bias, ssd_a_log, ssd_d, ssd_norm, ssd_w_out,
           mlp_w1, mlp_w2, ln_mix_g, ln_mix_b, ln_mlp_g, ln_mlp_b):
    depth = mlp_w1.shape[0]
    D = x_prompt.shape[2]
    alpha = (2 * depth) ** 0.25
    n_ab, _, H, dk, dv = state_mlstm_C.shape
    n_ssd, _, nh, hd, N = state_ssd_h.shape
    cd = state_ssd_conv.shape[3]
    G = (cd - nh * hd) // (2 * N)
    ab_prm = [_prep_ab(j, H, dk, dv, ab_w_in, ab_ig_bias, ab_fg_bias, ab_ml_norm, ab_gla_wa2,
                       ab_gla_ba, ab_gla_norm, ab_w_out) for j in range(n_ab)]
    ssd_prm = [_prep_ssd(j, G, nh, hd, N, ssd_w_in, ssd_conv_w, ssd_conv_b, ssd_dt_bias, ssd_a_log,
                         ssd_d, ssd_norm, ssd_w_out) for j in range(n_ssd)]
    mlp = [(mlp_w1[l].astype(BF16), mlp_w2[l].astype(BF16)) for l in range(depth)]
    lns = [(ln_mix_g[l].reshape(1, D), ln_mix_b[l].reshape(1, D),
            ln_mlp_g[l].reshape(1, D), ln_mlp_b[l].reshape(1, D)) for l in range(depth)]

    Bp = x_prompt.shape[0]
    zeros = (jnp.zeros((n_ab, Bp, H, dk, dv), F32), jnp.zeros((n_ab, Bp, H, dk), F32),
             jnp.zeros((n_ab, Bp, H), F32), jnp.zeros((n_ab, Bp, H, dk, dv), F32),
             jnp.zeros((n_ssd, Bp, nh, hd, N), F32), jnp.zeros((n_ssd, Bp) + state_ssd_conv.shape[2:], F32))
    carried = (state_mlstm_C, state_mlstm_n, state_mlstm_m, state_gla_S, state_ssd_h, state_ssd_conv)
    chunk = 128
    yp = _trunk(x_prompt, zeros, ab_prm, ssd_prm, mlp, lns, alpha, chunk)
    ys = _trunk(x_sample, carried, ab_prm, ssd_prm, mlp, lns, alpha, chunk)
    return (yp[0], ys[0]) + yp[1:] + ys[1:]
```

```python
import functools
import math

import jax
import jax.numpy as jnp
import numpy as np
from jax import lax
from jax.experimental import pallas as pl
from jax.experimental.pallas import tpu as pltpu

F32 = jnp.float32
BF16 = jnp.bfloat16

LN_EPS = 1e-5
GLA_TAU = 16.0
LANES = 128
SUBLANES = 8
VMEM_LIMIT = 48 * 1024 * 1024


def _dot(a, b):
    return jnp.dot(a.astype(BF16), b.astype(BF16), preferred_element_type=F32)


def _dot_nt(a, b):
    return lax.dot_general(a.astype(BF16), b.astype(BF16), (((1,), (1,)), ((), ())),
                           preferred_element_type=F32)


def _dot_tn(a, b):
    return lax.dot_general(a.astype(BF16), b.astype(BF16), (((0,), (0,)), ((), ())),
                           preferred_element_type=F32)


def _split3(x):
    hi = x.astype(BF16)
    r = x - hi.astype(F32)
    mid = r.astype(BF16)
    lo = (r - mid.astype(F32)).astype(BF16)
    return hi, mid, lo


def _dot3(t, x):
    hi, mid, lo = _split3(x)
    f = lambda p: jnp.dot(t, p, preferred_element_type=F32)
    return f(hi) + f(mid) + f(lo)


def _dot3_right(x, t):
    hi, mid, lo = _split3(x)
    f = lambda p: jnp.dot(p, t, preferred_element_type=F32)
    return f(hi) + f(mid) + f(lo)


def _softplus(x):
    return jnp.maximum(x, 0.0) + jnp.log1p(jnp.exp(-jnp.abs(x)))


def _log_sigmoid(x):
    return -_softplus(-x)


def _sigmoid(x):
    return 1.0 / (1.0 + jnp.exp(-x))


def _layernorm_rows(r, g, b):
    mu = jnp.mean(r, axis=1, keepdims=True)
    d = r - mu
    var = jnp.mean(d * d, axis=1, keepdims=True)
    return d * lax.rsqrt(var + LN_EPS) * g + b


def _headnorm(h, g):
    mu = jnp.mean(h, axis=1, keepdims=True)
    d = h - mu
    var = jnp.mean(d * d, axis=1, keepdims=True)
    return d * lax.rsqrt(var + LN_EPS) * g


def _select_chain(idx, pieces, width):
    out = pieces[0]
    for u in range(1, len(pieces)):
        out = jnp.where(idx < u * width, out, pieces[u])
    return out


def _cumsum_mats(L):
    t = np.arange(L)[:, None]
    j = np.arange(L)[None, :]
    upper = (t <= j)
    lower = (j <= t)
    return upper, lower


def _ab_mats(L):
    nlev = int(round(math.log2(L)))
    assert 1 << nlev == L
    upper, lower = _cumsum_mats(L)
    t = np.arange(L)[:, None]
    j = np.arange(L)[None, :]
    mats = [upper, lower]
    for i in range(nlev):
        n = L >> (i + 1)
        mid = (t // (2 * n)) * (2 * n) + n - 1
        second = (t % (2 * n)) >= n
        m = np.where(second, (j > mid) & (j <= t), (j > t) & (j <= mid))
        mats.append(m)
    return jnp.asarray(np.concatenate(mats, axis=0).astype(np.float32), dtype=BF16), nlev


def _ssd_mats(L):
    upper, lower = _cumsum_mats(L)
    return jnp.asarray(np.concatenate([upper, lower], axis=0).astype(np.float32), dtype=BF16)


def _proj_kernel(x_ref, w_ref, o_ref):
    o_ref[...] = jnp.dot(x_ref[...].astype(BF16), w_ref[...], preferred_element_type=F32)


def _proj(x, w, tm, tn):
    M, K = x.shape
    N = w.shape[1]
    assert M % tm == 0 and N % tn == 0
    return pl.pallas_call(
        _proj_kernel,
        grid=(N // tn, M // tm),
        in_specs=[pl.BlockSpec((tm, K), lambda j, i: (i, 0)),
                  pl.BlockSpec((K, tn), lambda j, i: (0, j))],
        out_specs=pl.BlockSpec((tm, tn), lambda j, i: (i, j)),
        out_shape=jax.ShapeDtypeStruct((M, N), F32),
        compiler_params=pltpu.CompilerParams(
            dimension_semantics=("parallel", "parallel"), vmem_limit_bytes=VMEM_LIMIT),
        name="proj",
    )(x, w)


def _outproj_ln_kernel(alpha, n_in, *refs):
    ys = refs[:n_in]
    ws = refs[n_in:2 * n_in]
    x_ref, g_ref, b_ref, o_ref = refs[2 * n_in:]
    r = alpha * x_ref[...]
    for y_ref, w_ref in zip(ys, ws):
        r = r + jnp.dot(y_ref[...].astype(BF16), w_ref[...], preferred_element_type=F32)
    o_ref[...] = _layernorm_rows(r, g_ref[...], b_ref[...])


def _outproj_ln(ys, ws, x, g, b, alpha, tm):
    M, D = x.shape
    assert M % tm == 0
    n_in = len(ys)
    in_specs = ([pl.BlockSpec((tm, y.shape[1]), lambda i: (i, 0)) for y in ys]
                + [pl.BlockSpec(w.shape, lambda i: (0, 0)) for w in ws]
                + [pl.BlockSpec((tm, D), lambda i: (i, 0)),
                   pl.BlockSpec((1, D), lambda i: (0, 0)),
                   pl.BlockSpec((1, D), lambda i: (0, 0))])
    return pl.pallas_call(
        functools.partial(_outproj_ln_kernel, alpha, n_in),
        grid=(M // tm,),
        in_specs=in_specs,
        out_specs=pl.BlockSpec((tm, D), lambda i: (i, 0)),
        out_shape=jax.ShapeDtypeStruct((M, D), F32),
        compiler_params=pltpu.CompilerParams(
            dimension_semantics=("parallel",), vmem_limit_bytes=VMEM_LIMIT),
        name="outproj_ln",
    )(*ys, *ws, x, g, b)


def _mlp_kernel(alpha, nf, x_ref, w1_ref, w2_ref, g_ref, b_ref, o_ref, acc_ref):
    f = pl.program_id(1)

    @pl.when(f == 0)
    def _():
        acc_ref[...] = jnp.zeros_like(acc_ref)

    h = jnp.dot(x_ref[...].astype(BF16), w1_ref[...], preferred_element_type=F32)
    h = jnp.square(jnp.maximum(h, 0.0))
    acc_ref[...] += jnp.dot(h.astype(BF16), w2_ref[...], preferred_element_type=F32)

    @pl.when(f == nf - 1)
    def _():
        r = alpha * x_ref[...] + acc_ref[...]
        o_ref[...] = _layernorm_rows(r, g_ref[...], b_ref[...])


def _mlp_ln(x, w1, w2, g, b, alpha, tm, tf):
    M, D = x.shape
    Fdim = w1.shape[1]
    assert M % tm == 0 and Fdim % tf == 0
    nf = Fdim // tf
    return pl.pallas_call(
        functools.partial(_mlp_kernel, alpha, nf),
        grid=(M // tm, nf),
        in_specs=[pl.BlockSpec((tm, D), lambda i, f: (i, 0)),
                  pl.BlockSpec((D, tf), lambda i, f: (0, f)),
                  pl.BlockSpec((tf, D), lambda i, f: (f, 0)),
                  pl.BlockSpec((1, D), lambda i, f: (0, 0)),
                  pl.BlockSpec((1, D), lambda i, f: (0, 0))],
        out_specs=pl.BlockSpec((tm, D), lambda i, f: (i, 0)),
        out_shape=jax.ShapeDtypeStruct((M, D), F32),
        scratch_shapes=[pltpu.VMEM((tm, D), F32)],
        compiler_params=pltpu.CompilerParams(
            dimension_semantics=("parallel", "arbitrary"), vmem_limit_bytes=VMEM_LIMIT),
        name="mlp_ln",
    )(x, w1, w2, g, b)


def _ab_kernel(L, valid, nlev, nc, bt, H, dk, dv, n_alias,
               qk_ref, v_ref, mo_ref, gqk_ref, gv_ref, gg_ref, sm_ref, smt_ref,
               c0_ref, n0_ref, m0_ref, s0_ref, gb_ref, mln_ref, wa2_ref, ba_ref, gln_ref, tmat_ref,
               *rest):
    ml_ref, gla_ref, c_ref, n_ref, m_ref, s_ref, cs, ns, ms, ss = rest[n_alias:]
    c = pl.program_id(1)
    per = LANES // dk
    HK = H * dk

    @pl.when(c == 0)
    def _():
        cs[...] = c0_ref[0].reshape(bt, HK, dv)
        ns[...] = n0_ref[0]
        ms[...] = m0_ref[0]
        ss[...] = s0_ref[0].reshape(bt, HK, dv)

    t_col = lax.broadcasted_iota(jnp.int32, (L, 1), 0)
    s_row = lax.broadcasted_iota(jnp.int32, (1, L), 1)
    tt = lax.broadcasted_iota(jnp.int32, (L, L), 0)
    sc = lax.broadcasted_iota(jnp.int32, (L, L), 1)
    causal = sc <= tt
    eye = sc == tt
    lane = lax.broadcasted_iota(jnp.int32, (L, LANES), 1)
    lane1 = lax.broadcasted_iota(jnp.int32, (1, LANES), 1)
    laneh = lax.broadcasted_iota(jnp.int32, (1, H), 1)
    rowp = lax.broadcasted_iota(jnp.int32, (LANES, dv), 0)
    ek = lax.broadcasted_iota(jnp.int32, (LANES, LANES), 0) == lax.broadcasted_iota(jnp.int32, (LANES, LANES), 1)
    inhead = [(lane >= u * dk) & (lane < (u + 1) * dk) for u in range(per)]
    upper = tmat_ref[0:L, :]
    scale = dk ** -0.5

    def to_col(row):
        return jnp.sum(jnp.where(eye, row, 0.0), axis=1, keepdims=True)

    for bi in range(bt):
        rows = slice(bi * L, (bi + 1) * L)
        g8 = smt_ref[bi] + gb_ref[...]
        ig8 = g8
        lf8 = _log_sigmoid(g8)
        if valid < L:
            ok = s_row < valid
            ig8 = jnp.where(ok, ig8, -jnp.inf)
            lf8 = jnp.where(ok, lf8, 0.0)
        b8 = _dot3_right(lf8, upper)
        m_prev = ms[bi]
        m_new = m_prev

        qk = qk_ref[rows, :]
        for p in range(H // per):
            ps = slice(p * LANES, (p + 1) * LANES)
            Qp = qk[:, ps]
            Kp = qk[:, HK + p * LANES:HK + (p + 1) * LANES] * scale
            Kpb = Kp.astype(BF16)
            C0p = cs[bi, ps, :]
            n0p = ns[bi][:, ps]
            c_new, n_new = [], []
            for u in range(per):
                h = p * per + u
                hs_ = slice(h * dv, (h + 1) * dv)
                ig_row = ig8[h:h + 1, :]
                b_row = b8[H + h:H + h + 1, :]
                b_col = to_col(b_row)
                ig_col = to_col(ig_row)
                m0 = m_prev[:, h:h + 1]
                D = jnp.where(causal, b_col - b_row + ig_row, -jnp.inf)
                g_col = b_col + m0
                m_col = jnp.maximum(g_col, jnp.max(D, axis=1, keepdims=True))
                w_intra = jnp.exp(D - m_col)
                w_inter = jnp.exp(g_col - m_col)
                Qh = jnp.where(inhead[u], Qp, 0.0)
                Qhb = Qh.astype(BF16)
                vb = v_ref[rows, hs_].astype(BF16)
                s = _dot_nt(Qhb, Kpb) * w_intra
                num = _dot(s, vb) + w_inter * _dot(Qhb, C0p)
                den = (jnp.sum(s, axis=1, keepdims=True)
                       + w_inter * jnp.sum(Qh * n0p, axis=1, keepdims=True))
                hh = num / jnp.maximum(jnp.abs(den), jnp.exp(-m_col))
                ml_ref[rows, hs_] = _headnorm(hh, mln_ref[:, hs_]) * _sigmoid(mo_ref[rows, hs_])

                mL = m_col[L - 1:L, :]
                wL_col = jnp.exp(b_col[L - 1:L, :] - b_col + ig_col - mL)
                wL0 = jnp.exp(g_col[L - 1:L, :] - mL)
                kw = Kp * wL_col
                c_new.append(wL0 * C0p + _dot_tn(kw, vb))
                n_new.append(wL0 * n0p + jnp.sum(kw, axis=0, keepdims=True))
                m_new = jnp.where(laneh == h, mL, m_new)
            cs[bi, ps, :] = _select_chain(rowp, c_new, dk)
            ns[bi, :, ps] = _select_chain(lane1, n_new, dk)
        ms[bi] = m_new

        gqk = gqk_ref[rows, :]
        Q2 = gqk[:, :HK] * scale
        K2 = gqk[:, HK:]
        la = _log_sigmoid(_dot(sm_ref[rows, :], wa2_ref[...]) + ba_ref[...]) * (1.0 / GLA_TAU)
        if valid < L:
            la = jnp.where(t_col < valid, la, 0.0)
        TL = _dot3(tmat_ref[L:(2 + nlev) * L, :], la)
        A = TL[0:L]
        Q2b = Q2.astype(BF16)
        K2b = K2.astype(BF16)
        scores = []
        for h in range(H):
            p, u = divmod(h, per)
            ps = slice(p * LANES, (p + 1) * LANES)
            scores.append(jnp.where(eye, _dot_nt(jnp.where(inhead[u], Q2b[:, ps], 0), K2b[:, ps]), 0.0))
        for i in range(nlev):
            n = L >> (i + 1)
            En = jnp.exp(TL[(1 + i) * L:(2 + i) * L])
            second = (t_col & n) != 0
            X = (jnp.where(second, Q2, K2) * En).astype(BF16)
            Xk = jnp.where(second, 0, X)
            if i > 0:
                sh = int(round(math.log2(2 * n)))
                same = (tt >> sh) == (sc >> sh)
            for h in range(H):
                p, u = divmod(h, per)
                ps = slice(p * LANES, (p + 1) * LANES)
                sn = _dot_nt(jnp.where(second & inhead[u], X[:, ps], 0), Xk[:, ps])
                if i > 0:
                    sn = jnp.where(same, sn, 0.0)
                scores[h] = scores[h] + sn
        QA = (Q2 * jnp.exp(A)).astype(BF16)
        AL = A[L - 1:L, :]
        kd = (K2 * jnp.exp(AL - A)).astype(BF16)
        eAL = jnp.exp(AL)
        for p in range(H // per):
            ps = slice(p * LANES, (p + 1) * LANES)
            S0p = ss[bi, ps, :]
            dec_col = jnp.sum(jnp.where(ek, eAL[:, ps], 0.0), axis=1, keepdims=True)
            upd = []
            for u in range(per):
                h = p * per + u
                hs_ = slice(h * dv, (h + 1) * dv)
                v2b = gv_ref[rows, hs_].astype(BF16)
                o = _dot(scores[h], v2b) + _dot(jnp.where(inhead[u], QA[:, ps], 0), S0p)
                gg = gg_ref[rows, hs_]
                gla_ref[rows, hs_] = _headnorm(o, gln_ref[:, hs_]) * (gg * _sigmoid(gg))
                upd.append(_dot_tn(kd[:, ps], v2b))
            ss[bi, ps, :] = dec_col * S0p + _select_chain(rowp, upd, dk)

    @pl.when(c == nc - 1)
    def _():
        c_ref[0] = cs[...].reshape(bt, H, dk, dv)
        n_ref[0] = ns[...]
        m_ref[0] = ms[...]
        s_ref[0] = ss[...].reshape(bt, H, dk, dv)


def _ab_scan(P, B, nc, L, valid, bt, j, states, prev, prm):
    M = P.shape[0]
    C0, n0, m0, S0 = states
    nst, _, H, dk, dv = C0.shape
    HK = H * dk
    assert M == B * nc * L and dv == LANES and LANES % dk == 0 and B % bt == 0
    assert bt == 1 or nc == 1
    tmat, nlev = _ab_mats(L)
    wq = 2 * HK
    wv = H * dv
    assert wq == wv
    nsm = (2 * wq + 4 * wv) // LANES
    smt = P[:, nsm * LANES:nsm * LANES + 8].reshape(B * nc, L, 8).transpose(0, 2, 1)
    R = bt * L
    row = lambda b, c: b * nc + c
    sec = lambda k: pl.BlockSpec((R, wv), lambda b, c: (row(b, c), k))
    st_c = pl.BlockSpec((1, bt, H, dk, dv), lambda b, c: (j, b, 0, 0, 0))
    st_n = pl.BlockSpec((1, bt, 1, HK), lambda b, c: (j, b, 0, 0))
    st_m = pl.BlockSpec((1, bt, 1, H), lambda b, c: (j, b, 0, 0))
    full = lambda a: pl.BlockSpec(a.shape, lambda b, c: (0,) * a.ndim)
    n_alias = 0 if prev is None else 4
    in_specs = [
        sec(0), sec(1), sec(2), sec(3), sec(4), sec(5),
        pl.BlockSpec((R, LANES), lambda b, c: (row(b, c), nsm)),
        pl.BlockSpec((bt, 8, L), lambda b, c: (row(b, c), 0, 0)),
        st_c, st_n, st_m, st_c,
        full(prm["gate_bias"]), full(prm["ml_norm"]), full(prm["wa2"]), full(prm["ba"]),
        full(prm["gla_norm"]), full(tmat),
    ] + [pl.BlockSpec(memory_space=pl.ANY)] * n_alias
    out_specs = [
        pl.BlockSpec((R, wv), lambda b, c: (row(b, c), 0)),
        pl.BlockSpec((R, wv), lambda b, c: (row(b, c), 0)),
        st_c, st_n, st_m, st_c,
    ]
    out_shape = [
        jax.ShapeDtypeStruct((M, wv), F32), jax.ShapeDtypeStruct((M, wv), F32),
        jax.ShapeDtypeStruct(C0.shape, F32), jax.ShapeDtypeStruct(n0.shape, F32),
        jax.ShapeDtypeStruct(m0.shape, F32), jax.ShapeDtypeStruct(S0.shape, F32),
    ]
    n_in = len(in_specs) - n_alias
    aliases = {n_in + k: 2 + k for k in range(n_alias)}
    args = [P, P, P, P, P, P, P, smt, C0, n0, m0, S0,
            prm["gate_bias"], prm["ml_norm"], prm["wa2"], prm["ba"], prm["gla_norm"], tmat]
    if prev is not None:
        args += list(prev)
    return pl.pallas_call(
        functools.partial(_ab_kernel, L, valid, nlev, nc, bt, H, dk, dv, n_alias),
        grid=(B // bt, nc),
        in_specs=in_specs,
        out_specs=out_specs,
        out_shape=out_shape,
        input_output_aliases=aliases,
        scratch_shapes=[pltpu.VMEM((bt, HK, dv), F32), pltpu.VMEM((bt, 1, HK), F32),
                        pltpu.VMEM((bt, 1, H), F32), pltpu.VMEM((bt, HK, dv), F32)],
        compiler_params=pltpu.CompilerParams(
            dimension_semantics=("parallel", "arbitrary"), vmem_limit_bytes=VMEM_LIMIT),
        name="ab_scan",
    )(*args)


def _ssd_kernel(L, valid, nc, bt, G, hpg, hd, n_alias,
                z_ref, x_ref, b_ref, c_ref, dt_ref, dtt_ref, cv_ref, h0_ref,
                cw_ref, cb_ref, dtbr_ref, dtbc_ref, alr_ref, alc_ref, dsk_ref, nrm_ref,
                tmat_ref, exp_ref, *rest):
    y_ref, h_ref, catx, catb, catc, hs = rest[n_alias:]
    cidx = pl.program_id(1)
    gw = hpg * hd
    nh = G * hpg
    inner = nh * hd
    N = h0_ref.shape[4]
    W = cw_ref.shape[0]
    P0 = SUBLANES - (W - 1)
    per = LANES // hd
    cats = ((catx, 0, inner), (catb, inner, G * N), (catc, inner + G * N, G * N))

    @pl.when(cidx == 0)
    def _():
        for cat, off, wid in cats:
            cat[:, P0:SUBLANES, :] = cv_ref[0, :, :, off:off + wid]
        hs[...] = h0_ref[0].reshape(bt, inner, N)

    t_col = lax.broadcasted_iota(jnp.int32, (L, 1), 0)
    s_row = lax.broadcasted_iota(jnp.int32, (1, L), 1)
    tt = lax.broadcasted_iota(jnp.int32, (L, L), 0)
    sc = lax.broadcasted_iota(jnp.int32, (L, L), 1)
    causal = sc <= tt
    lane = lax.broadcasted_iota(jnp.int32, (L, LANES), 1)
    upper = tmat_ref[0:L, :]
    lower = tmat_ref[L:2 * L, :]
    expand = exp_ref[...]

    def conv_silu(cat, bi, off, lo, wid):
        acc = cb_ref[:, off + lo:off + lo + wid]
        for w in range(W):
            acc = acc + cat[bi, P0 + w:P0 + w + L, lo:lo + wid] * cw_ref[w:w + 1, off + lo:off + lo + wid]
        return acc * _sigmoid(acc)

    for bi in range(bt):
        rows = slice(bi * L, (bi + 1) * L)
        catx[bi, SUBLANES:SUBLANES + L, :] = x_ref[rows, :]
        catb[bi, SUBLANES:SUBLANES + L, :] = b_ref[rows, :]
        catc[bi, SUBLANES:SUBLANES + L, :] = c_ref[rows, :]

        dtc = _softplus(dt_ref[rows, :] + dtbr_ref[...])
        dtr = _softplus(dtt_ref[bi] + dtbc_ref[...])
        if valid < L:
            dtc = jnp.where(t_col < valid, dtc, 0.0)
            dtr = jnp.where(s_row < valid, dtr, 0.0)
        cs_col = _dot3(lower, dtc * (-jnp.exp(alr_ref[...])))
        cs_row = _dot3_right(dtr * (-jnp.exp(alc_ref[...])), upper)

        for g in range(G):
            gl = slice(g * LANES, (g + 1) * LANES)
            gs = slice(g * gw, (g + 1) * gw)
            xa = conv_silu(catx, bi, 0, g * gw, gw)
            Bm = conv_silu(catb, bi, inner, g * N, N)
            Cm = conv_silu(catc, bi, inner + G * N, g * N, N)
            cs_g = cs_col[:, gl]
            cs_exp = _dot3_right(cs_g, expand)
            dt_exp = _dot3_right(dtc[:, gl], expand)
            Bmb = Bm.astype(BF16)
            Cmb = Cm.astype(BF16)
            CB = _dot_nt(Cmb, Bmb)
            hs0 = hs[bi, gs, :]
            y = _dot_nt(Cmb, hs0) * jnp.exp(cs_exp)
            xab = xa.astype(BF16)
            parts = []
            for p in range(gw // LANES):
                xp = xab[:, p * LANES:(p + 1) * LANES]
                ys = []
                for u in range(per):
                    jl = p * per + u
                    jr = g * hpg + jl
                    seg = jnp.where(causal, cs_g[:, jl:jl + 1] - cs_row[jr:jr + 1, :], -jnp.inf)
                    mj = CB * (jnp.exp(seg) * dtr[jr:jr + 1, :])
                    ys.append(_dot(mj, xp))
                parts.append(_select_chain(lane, ys, hd))
            y = y + jnp.concatenate(parts, axis=1) + dsk_ref[:, gs] * xa
            z = z_ref[rows, gs]
            y = y * (z * _sigmoid(z))
            y = y * lax.rsqrt(jnp.mean(y * y, axis=1, keepdims=True) + LN_EPS) * nrm_ref[:, gs]
            y_ref[rows, gs] = y

            xw = xa * (jnp.exp(cs_exp[L - 1:L, :] - cs_exp) * dt_exp)
            upd = _dot_tn(xw, Bmb)
            for jl in range(hpg):
                jr = g * hpg + jl
                dec = jnp.exp(cs_row[jr:jr + 1, L - 1:L])
                r0 = g * gw + jl * hd
                hs[bi, r0:r0 + hd, :] = dec * hs0[jl * hd:(jl + 1) * hd, :] + upd[jl * hd:(jl + 1) * hd, :]

        if nc > 1:
            for cat, _, _ in cats:
                tail = cat[bi, P0 + L:SUBLANES + L, :]
                cat[bi, P0:SUBLANES, :] = tail

    @pl.when(cidx == nc - 1)
    def _():
        h_ref[0] = hs[...].reshape(h_ref.shape[1:])


def _ssd_scan(P, B, nc, L, valid, bt, j, h0, conv0, prev, prm):
    M = P.shape[0]
    nst, _, nh, hd, N = h0.shape
    G = prm["groups"]
    hpg = nh // G
    gw = hpg * hd
    inner = nh * hd
    GN = G * N
    W1, cd = conv0.shape[2], conv0.shape[3]
    assert M == B * nc * L and N == LANES and gw % LANES == 0 and hpg == SUBLANES and B % bt == 0
    assert bt == 1 or nc == 1
    assert inner % GN == 0 and cd == inner + 2 * GN
    tmat = _ssd_mats(L)
    o_dt = 2 * inner + 2 * GN
    dtt = (P[:, o_dt:o_dt + G * LANES].reshape(M, G, LANES)[:, :, :hpg]
           .reshape(B * nc, L, nh).transpose(0, 2, 1))
    R = bt * L
    row = lambda b, c: b * nc + c
    full = lambda a: pl.BlockSpec(a.shape, lambda b, c: (0,) * a.ndim)
    st_h = pl.BlockSpec((1, bt, nh, hd, N), lambda b, c: (j, b, 0, 0, 0))
    n_alias = 0 if prev is None else 1
    in_specs = [
        pl.BlockSpec((R, inner), lambda b, c: (row(b, c), 0)),
        pl.BlockSpec((R, inner), lambda b, c: (row(b, c), 1)),
        pl.BlockSpec((R, GN), lambda b, c: (row(b, c), 2 * inner // GN)),
        pl.BlockSpec((R, GN), lambda b, c: (row(b, c), 2 * inner // GN + 1)),
        pl.BlockSpec((R, G * LANES), lambda b, c: (row(b, c), o_dt // (G * LANES))),
        pl.BlockSpec((bt, nh, L), lambda b, c: (row(b, c), 0, 0)),
        pl.BlockSpec((1, bt, W1, cd), lambda b, c: (j, b, 0, 0)),
        st_h,
        full(prm["conv_w"]), full(prm["conv_b"]), full(prm["dtb_row"]), full(prm["dtb_col"]),
        full(prm["alog_row"]), full(prm["alog_col"]), full(prm["dskip"]), full(prm["norm"]),
        full(tmat), full(prm["expand"]),
    ] + [pl.BlockSpec(memory_space=pl.ANY)] * n_alias
    assert o_dt % (G * LANES) == 0
    out_specs = [pl.BlockSpec((R, inner), lambda b, c: (row(b, c), 0)), st_h]
    out_shape = [jax.ShapeDtypeStruct((M, inner), F32), jax.ShapeDtypeStruct(h0.shape, F32)]
    n_in = len(in_specs) - n_alias
    args = [P, P, P, P, P, dtt, conv0, h0,
            prm["conv_w"], prm["conv_b"], prm["dtb_row"], prm["dtb_col"], prm["alog_row"],
            prm["alog_col"], prm["dskip"], prm["norm"], tmat, prm["expand"]]
    if prev is not None:
        args.append(prev)
    return pl.pallas_call(
        functools.partial(_ssd_kernel, L, valid, nc, bt, G, hpg, hd, n_alias),
        grid=(B // bt, nc),
        in_specs=in_specs,
        out_specs=out_specs,
        out_shape=out_shape,
        input_output_aliases={n_in: 1} if n_alias else {},
        scratch_shapes=[pltpu.VMEM((bt, SUBLANES + L, inner), F32), pltpu.VMEM((bt, SUBLANES + L, GN), F32),
                        pltpu.VMEM((bt, SUBLANES + L, GN), F32), pltpu.VMEM((bt, inner, N), F32)],
        compiler_params=pltpu.CompilerParams(
            dimension_semantics=("parallel", "arbitrary"), vmem_limit_bytes=VMEM_LIMIT),
        name="ssd_scan",
    )(*args)


def _prep_ab(j, H, dk, dv, w_in, ig_bias, fg_bias, ml_norm, wa2, ba, gla_norm, w_out):
    rank = wa2.shape[1]
    qk, vv = H * dk, H * dv
    o = np.cumsum([0, qk, qk, vv, vv, H, H, qk, qk, vv, vv, rank])
    assert int(o[-1]) == w_in.shape[2]
    wj = w_in[j]
    small = jnp.concatenate([wj[:, o[4]:o[6]], wj[:, o[10]:o[11]]], axis=1)
    small = jnp.pad(small, ((0, 0), (0, LANES - small.shape[1])))
    w = jnp.concatenate([wj[:, :o[4]], wj[:, o[6]:o[10]], small], axis=1).astype(BF16)
    wa2p = jnp.zeros((LANES, qk), F32).at[2 * H:2 * H + rank, :].set(wa2[j]).astype(BF16)
    return {
        "w_in": w,
        "gate_bias": jnp.concatenate([ig_bias[j], fg_bias[j]]).astype(F32).reshape(2 * H, 1),
        "ml_norm": ml_norm[j].reshape(1, vv),
        "gla_norm": gla_norm[j].reshape(1, vv),
        "wa2": wa2p,
        "ba": ba[j].reshape(1, qk),
        "w_out_ml": w_out[j][:vv].astype(BF16),
        "w_out_gla": w_out[j][vv:].astype(BF16),
    }


def _prep_ssd(j, G, nh, hd, N, w_in, conv_w, conv_b, dt_bias, a_log, d_skip, norm_g, w_out):
    inner = nh * hd
    hpg = nh // G
    cd = inner + 2 * G * N
    wz, wxbc, wdt = w_in[j][:, :inner], w_in[j][:, inner:inner + cd], w_in[j][:, inner + cd:]
    wdt = jnp.pad(wdt.reshape(-1, G, hpg), ((0, 0), (0, 0), (0, LANES - hpg))).reshape(-1, G * LANES)
    w = jnp.concatenate([wz, wxbc, wdt], axis=1).astype(BF16)
    lane_form = lambda v: jnp.pad(v.reshape(G, hpg), ((0, 0), (0, LANES - hpg))).reshape(1, G * LANES)
    e = np.zeros((LANES, hpg * hd), np.float32)
    for r in range(hpg):
        e[r, r * hd:(r + 1) * hd] = 1.0
    return {
        "groups": G,
        "w_in": w,
        "conv_w": conv_w[j], "conv_b": conv_b[j].reshape(1, cd),
        "dtb_row": lane_form(dt_bias[j]), "dtb_col": dt_bias[j].reshape(nh, 1),
        "alog_row": lane_form(a_log[j]), "alog_col": a_log[j].reshape(nh, 1),
        "dskip": jnp.repeat(d_skip[j], hd).reshape(1, inner),
        "norm": norm_g[j].reshape(1, inner),
        "expand": jnp.asarray(e, dtype=BF16),
        "w_out": w_out[j].astype(BF16),
    }


def _pad_time(P, B, T, Tp):
    if T == Tp:
        return P
    return jnp.pad(P.reshape(B, T, -1), ((0, 0), (0, Tp - T), (0, 0))).reshape(B * Tp, -1)


def _unpad_time(Y, B, T, Tp):
    if T == Tp:
        return Y
    return Y.reshape(B, Tp, -1)[:, :T].reshape(B * T, -1)


def _trunk(x, states, ab_prm, ssd_prm, mlp, lns, alpha, chunk, bt_short):
    B, T, D = x.shape
    mC, mn, mm, gS, sh, sconv = states
    n_ab, _, H, dk, dv = mC.shape
    if T % chunk == 0:
        L, nc, bt = chunk, T // chunk, 1
    else:
        L, nc, bt = max(SUBLANES, 1 << int(math.ceil(math.log2(T)))), 1, bt_short
    Tp = nc * L
    valid = T if nc == 1 else L
    M = B * T
    tm = 512 if M % 512 == 0 else M
    tmm = 1024 if M % 1024 == 0 else tm
    X = x.reshape(M, D)
    ab_states = (mC, mn.reshape(n_ab, B, 1, H * dk), mm.reshape(n_ab, B, 1, H), gS)
    ab_out, h_out, ncv = None, None, []
    depth = len(mlp)
    for l in range(depth):
        j = l // 2
        if l % 2 == 0:
            p = ab_prm[j]
            P = _proj(X, p["w_in"], tm, p["w_in"].shape[1])
            res = _ab_scan(_pad_time(P, B, T, Tp), B, nc, L, valid, bt, j, ab_states, ab_out, p)
            ab_out = res[2:]
            ys = [_unpad_time(res[0], B, T, Tp), _unpad_time(res[1], B, T, Tp)]
            ws = [p["w_out_ml"], p["w_out_gla"]]
        else:
            p = ssd_prm[j]
            P = _proj(X, p["w_in"], tm, p["w_in"].shape[1] // 2)
            inner = p["norm"].shape[1]
            cd = p["conv_b"].shape[1]
            W1 = sconv.shape[2]
            assert T >= W1
            ncv.append(P.reshape(B, T, -1)[:, T - W1:, inner:inner + cd])
            y, h_out = _ssd_scan(_pad_time(P, B, T, Tp), B, nc, L, valid, bt, j, sh, sconv, h_out, p)
            ys = [_unpad_time(y, B, T, Tp)]
            ws = [p["w_out"]]
        g1, b1, g2, b2 = lns[l]
        X = _outproj_ln(ys, ws, X, g1, b1, alpha, tm)
        w1, w2 = mlp[l]
        X = _mlp_ln(X, w1, w2, g2, b2, alpha, tmm, 1024)
    nC, nn_, nm, nS = ab_out
    return (X.reshape(B, T, D), nC, nn_.reshape(mn.shape), nm.reshape(mm.shape), nS, h_out, jnp.stack(ncv))


def kernel(x_prompt, x_sample, state_mlstm_C, state_mlstm_n, state_mlstm_m, state_gla_S, state_ssd_h,
           state_ssd_conv, ab_w_in, ab_ig_bias, ab_fg_bias, ab_ml_norm, ab_gla_wa2, ab_gla_ba, ab_gla_norm,
           ab_w_out, ssd_w_in, ssd_conv_w, ssd_conv_b, ssd_dt_bias, ssd_a_log, ssd_d, ssd_norm, ssd_w_out,
           mlp_w1, mlp_w2, ln_mix_g, ln_mix_b, ln_mlp_g, ln_mlp_b):
    depth = mlp_w1.shape[0]
    D = x_prompt.shape[2]
    alpha = (2 * depth) ** 0.25
    n_ab, _, H, dk, dv = state_mlstm_C.shape
    n_ssd, _, nh, hd, N = state_ssd_h.shape
    cd = state_ssd_conv.shape[3]
    G = (cd - nh * hd) // (2 * N)
    ab_prm = [_prep_ab(j, H, dk, dv, ab_w_in, ab_ig_bias, ab_fg_bias, ab_ml_norm, ab_gla_wa2,
                       ab_gla_ba, ab_gla_norm, ab_w_out) for j in range(n_ab)]
    ssd_prm = [_prep_ssd(j, G, nh, hd, N, ssd_w_in, ssd_conv_w, ssd_conv_b, ssd_dt_bias, ssd_a_log,
                         ssd_d, ssd_norm, ssd_w_out) for j in range(n_ssd)]
    mlp = [(mlp_w1[l].astype(BF16), mlp_w2[l].astype(BF16)) for l in range(depth)]
    lns = [(ln_mix_g[l].reshape(1, D), ln_mix_b[l].reshape(1, D),
            ln_mlp_g[l].reshape(1, D), ln_mlp_b[l].reshape(1, D)) for l in range(depth)]

    Bp = x_prompt.shape[0]
    zeros = (jnp.zeros((n_ab, Bp, H, dk, dv), F32), jnp.zeros((n_ab, Bp, H, dk), F32),
             jnp.zeros((n_ab, Bp, H), F32), jnp.zeros((n_ab, Bp, H, dk, dv), F32),
             jnp.zeros((n_ssd, Bp, nh, hd, N), F32), jnp.zeros((n_ssd, Bp) + state_ssd_conv.shape[2:], F32))
    carried = (state_mlstm_C, state_mlstm_n, state_mlstm_m, state_gla_S, state_ssd_h, state_ssd_conv)
    chunk = 128
    bt_short = 2 if x_sample.shape[0] % 2 == 0 else 1
    yp = _trunk(x_prompt, zeros, ab_prm, ssd_prm, mlp, lns, alpha, chunk, bt_short)
    ys = _trunk(x_sample, carried, ab_prm, ssd_prm, mlp, lns, alpha, chunk, bt_short)
    return (yp[0], ys[0]) + yp[1:] + ys[1:]
```

```python
import functools
import math

import jax
import jax.numpy as jnp
import numpy as np
from jax import lax
from jax.experimental import pallas as pl
from jax.experimental.pallas import tpu as pltpu

F32 = jnp.float32
BF16 = jnp.bfloat16

LN_EPS = 1e-5
GLA_TAU = 16.0
LANES = 128
SUBLANES = 8
VMEM_LIMIT = 48 * 1024 * 1024


def _dot(a, b):
    return jnp.dot(a.astype(BF16), b.astype(BF16), preferred_element_type=F32)


def _dot_nt(a, b):
    return lax.dot_general(a.astype(BF16), b.astype(BF16), (((1,), (1,)), ((), ())),
                           preferred_element_type=F32)


def _dot_tn(a, b):
    return lax.dot_general(a.astype(BF16), b.astype(BF16), (((0,), (0,)), ((), ())),
                           preferred_element_type=F32)


def _split3(x):
    hi = x.astype(BF16)
    r = x - hi.astype(F32)
    mid = r.astype(BF16)
    lo = (r - mid.astype(F32)).astype(BF16)
    return hi, mid, lo


def _dot3(t, x):
    hi, mid, lo = _split3(x)
    f = lambda p: jnp.dot(t, p, preferred_element_type=F32)
    return f(hi) + f(mid) + f(lo)


def _dot3_right(x, t):
    hi, mid, lo = _split3(x)
    f = lambda p: jnp.dot(p, t, preferred_element_type=F32)
    return f(hi) + f(mid) + f(lo)


def _rows_of_transpose(sel, x):
    hi, mid, lo = _split3(x)
    f = lambda p: lax.dot_general(sel, p, (((1,), (1,)), ((), ())), preferred_element_type=F32)
    return f(hi) + f(mid) + f(lo)


def _softplus(x):
    return jnp.maximum(x, 0.0) + jnp.log1p(jnp.exp(-jnp.abs(x)))


def _log_sigmoid(x):
    return -_softplus(-x)


def _sigmoid(x):
    return 1.0 / (1.0 + jnp.exp(-x))


def _layernorm_rows(r, g, b):
    mu = jnp.mean(r, axis=1, keepdims=True)
    d = r - mu
    var = jnp.mean(d * d, axis=1, keepdims=True)
    return d * lax.rsqrt(var + LN_EPS) * g + b


def _headnorm(h, g):
    mu = jnp.mean(h, axis=1, keepdims=True)
    d = h - mu
    var = jnp.mean(d * d, axis=1, keepdims=True)
    return d * lax.rsqrt(var + LN_EPS) * g


def _select_chain(idx, pieces, width):
    out = pieces[0]
    for u in range(1, len(pieces)):
        out = jnp.where(idx < u * width, out, pieces[u])
    return out


def _cumsum_mats(L):
    t = np.arange(L)[:, None]
    j = np.arange(L)[None, :]
    upper = (t <= j)
    lower = (j <= t)
    return upper, lower


def _ab_mats(L):
    nlev = int(round(math.log2(L)))
    assert 1 << nlev == L
    upper, lower = _cumsum_mats(L)
    t = np.arange(L)[:, None]
    j = np.arange(L)[None, :]
    mats = [upper, lower]
    for i in range(nlev):
        n = L >> (i + 1)
        mid = (t // (2 * n)) * (2 * n) + n - 1
        second = (t % (2 * n)) >= n
        m = np.where(second, (j > mid) & (j <= t), (j > t) & (j <= mid))
        mats.append(m)
    return jnp.asarray(np.concatenate(mats, axis=0).astype(np.float32), dtype=BF16), nlev


def _ssd_mats(L):
    upper, lower = _cumsum_mats(L)
    return jnp.asarray(np.concatenate([upper, lower], axis=0).astype(np.float32), dtype=BF16)


def _proj_kernel(x_ref, w_ref, o_ref):
    o_ref[...] = jnp.dot(x_ref[...].astype(BF16), w_ref[...], preferred_element_type=F32)


def _proj(x, w, tm, tn):
    M, K = x.shape
    N = w.shape[1]
    assert M % tm == 0 and N % tn == 0
    return pl.pallas_call(
        _proj_kernel,
        grid=(N // tn, M // tm),
        in_specs=[pl.BlockSpec((tm, K), lambda j, i: (i, 0)),
                  pl.BlockSpec((K, tn), lambda j, i: (0, j))],
        out_specs=pl.BlockSpec((tm, tn), lambda j, i: (i, j)),
        out_shape=jax.ShapeDtypeStruct((M, N), F32),
        compiler_params=pltpu.CompilerParams(
            dimension_semantics=("parallel", "parallel"), vmem_limit_bytes=VMEM_LIMIT),
        name="proj",
    )(x, w)


def _outproj_ln_kernel(alpha, n_in, *refs):
    ys = refs[:n_in]
    ws = refs[n_in:2 * n_in]
    x_ref, g_ref, b_ref, o_ref = refs[2 * n_in:]
    r = alpha * x_ref[...]
    for y_ref, w_ref in zip(ys, ws):
        r = r + jnp.dot(y_ref[...].astype(BF16), w_ref[...], preferred_element_type=F32)
    o_ref[...] = _layernorm_rows(r, g_ref[...], b_ref[...])


def _outproj_ln(ys, ws, x, g, b, alpha, tm):
    M, D = x.shape
    assert M % tm == 0
    n_in = len(ys)
    in_specs = ([pl.BlockSpec((tm, y.shape[1]), lambda i: (i, 0)) for y in ys]
                + [pl.BlockSpec(w.shape, lambda i: (0, 0)) for w in ws]
                + [pl.BlockSpec((tm, D), lambda i: (i, 0)),
                   pl.BlockSpec((1, D), lambda i: (0, 0)),
                   pl.BlockSpec((1, D), lambda i: (0, 0))])
    return pl.pallas_call(
        functools.partial(_outproj_ln_kernel, alpha, n_in),
        grid=(M // tm,),
        in_specs=in_specs,
        out_specs=pl.BlockSpec((tm, D), lambda i: (i, 0)),
        out_shape=jax.ShapeDtypeStruct((M, D), F32),
        compiler_params=pltpu.CompilerParams(
            dimension_semantics=("parallel",), vmem_limit_bytes=VMEM_LIMIT),
        name="outproj_ln",
    )(*ys, *ws, x, g, b)


def _mlp_kernel(alpha, nf, x_ref, w1_ref, w2_ref, g_ref, b_ref, o_ref, acc_ref):
    f = pl.program_id(1)

    @pl.when(f == 0)
    def _():
        acc_ref[...] = jnp.zeros_like(acc_ref)

    h = jnp.dot(x_ref[...].astype(BF16), w1_ref[...], preferred_element_type=F32)
    h = jnp.square(jnp.maximum(h, 0.0))
    acc_ref[...] += jnp.dot(h.astype(BF16), w2_ref[...], preferred_element_type=F32)

    @pl.when(f == nf - 1)
    def _():
        r = alpha * x_ref[...] + acc_ref[...]
        o_ref[...] = _layernorm_rows(r, g_ref[...], b_ref[...])


def _mlp_ln(x, w1, w2, g, b, alpha, tm, tf):
    M, D = x.shape
    Fdim = w1.shape[1]
    assert M % tm == 0 and Fdim % tf == 0
    nf = Fdim // tf
    return pl.pallas_call(
        functools.partial(_mlp_kernel, alpha, nf),
        grid=(M // tm, nf),
        in_specs=[pl.BlockSpec((tm, D), lambda i, f: (i, 0)),
                  pl.BlockSpec((D, tf), lambda i, f: (0, f)),
                  pl.BlockSpec((tf, D), lambda i, f: (f, 0)),
                  pl.BlockSpec((1, D), lambda i, f: (0, 0)),
                  pl.BlockSpec((1, D), lambda i, f: (0, 0))],
        out_specs=pl.BlockSpec((tm, D), lambda i, f: (i, 0)),
        out_shape=jax.ShapeDtypeStruct((M, D), F32),
        scratch_shapes=[pltpu.VMEM((tm, D), F32)],
        compiler_params=pltpu.CompilerParams(
            dimension_semantics=("parallel", "arbitrary"), vmem_limit_bytes=VMEM_LIMIT),
        name="mlp_ln",
    )(x, w1, w2, g, b)


def _ab_kernel(L, valid, nlev, nc, bt, H, dk, dv, n_alias,
               qk_ref, v_ref, mo_ref, gqk_ref, gv_ref, gg_ref, sm_ref,
               c0_ref, n0_ref, m0_ref, s0_ref, gb_ref, mln_ref, wa2_ref, ba_ref, gln_ref, tmat_ref,
               sel_ref, *rest):
    ml_ref, gla_ref, c_ref, n_ref, m_ref, s_ref, cs, ns, ms, ss = rest[n_alias:]
    c = pl.program_id(1)
    per = LANES // dk
    HK = H * dk

    @pl.when(c == 0)
    def _():
        cs[...] = c0_ref[0].reshape(bt, HK, dv)
        ns[...] = n0_ref[0]
        ms[...] = m0_ref[0]
        ss[...] = s0_ref[0].reshape(bt, HK, dv)

    t_col = lax.broadcasted_iota(jnp.int32, (L, 1), 0)
    s_row = lax.broadcasted_iota(jnp.int32, (1, L), 1)
    tt = lax.broadcasted_iota(jnp.int32, (L, L), 0)
    sc = lax.broadcasted_iota(jnp.int32, (L, L), 1)
    causal = sc <= tt
    eye = sc == tt
    lane = lax.broadcasted_iota(jnp.int32, (L, LANES), 1)
    lane1 = lax.broadcasted_iota(jnp.int32, (1, LANES), 1)
    laneh = lax.broadcasted_iota(jnp.int32, (1, H), 1)
    rowp = lax.broadcasted_iota(jnp.int32, (LANES, dv), 0)
    ek = lax.broadcasted_iota(jnp.int32, (LANES, LANES), 0) == lax.broadcasted_iota(jnp.int32, (LANES, LANES), 1)
    inhead = [(lane >= u * dk) & (lane < (u + 1) * dk) for u in range(per)]
    upper = tmat_ref[0:L, :]
    scale = dk ** -0.5

    def to_col(row):
        return jnp.sum(jnp.where(eye, row, 0.0), axis=1, keepdims=True)

    BI = range(bt)
    pairs = [(bi, p) for bi in BI for p in range(H // per)]
    heads = [(bi, h) for bi in BI for h in range(H)]
    psl = lambda p: slice(p * LANES, (p + 1) * LANES)
    hsl = lambda h: slice(h * dv, (h + 1) * dv)
    each = lambda keys, f: {k: f(*k) for k in keys}
    neg_inf = -jnp.inf

    sm = [sm_ref[bi] for bi in BI]
    ig8 = [_rows_of_transpose(sel_ref[...], sm[bi]) + gb_ref[...] for bi in BI]
    lf8 = [_log_sigmoid(g) for g in ig8]
    if valid < L:
        ok = s_row < valid
        ig8 = [jnp.where(ok, g, neg_inf) for g in ig8]
        lf8 = [jnp.where(ok, g, 0.0) for g in lf8]
    b8 = [_dot3_right(g, upper) for g in lf8]
    m_prev = [ms[bi] for bi in BI]

    qk = [qk_ref[bi] for bi in BI]
    Qp = each(pairs, lambda bi, p: qk[bi][:, psl(p)])
    Kp = each(pairs, lambda bi, p: qk[bi][:, HK + p * LANES:HK + (p + 1) * LANES] * scale)
    Kpb = each(pairs, lambda bi, p: Kp[bi, p].astype(BF16))
    C0p = each(pairs, lambda bi, p: cs[bi, psl(p), :])
    n0p = each(pairs, lambda bi, p: ns[bi][:, psl(p)])
    ig_row = each(heads, lambda bi, h: ig8[bi][h:h + 1, :])
    b_row = each(heads, lambda bi, h: b8[bi][H + h:H + h + 1, :])
    b_col = each(heads, lambda bi, h: to_col(b_row[bi, h]))
    ig_col = each(heads, lambda bi, h: to_col(ig_row[bi, h]))
    D = each(heads, lambda bi, h: jnp.where(causal, b_col[bi, h] - b_row[bi, h] + ig_row[bi, h], neg_inf))
    g_col = each(heads, lambda bi, h: b_col[bi, h] + m_prev[bi][:, h:h + 1])
    m_col = each(heads, lambda bi, h: jnp.maximum(g_col[bi, h], jnp.max(D[bi, h], axis=1, keepdims=True)))
    w_intra = each(heads, lambda bi, h: jnp.exp(D[bi, h] - m_col[bi, h]))
    w_inter = each(heads, lambda bi, h: jnp.exp(g_col[bi, h] - m_col[bi, h]))
    Qh = each(heads, lambda bi, h: jnp.where(inhead[h % per], Qp[bi, h // per], 0.0))
    Qhb = each(heads, lambda bi, h: Qh[bi, h].astype(BF16))
    vb = each(heads, lambda bi, h: v_ref[bi, :, hsl(h)].astype(BF16))
    s = each(heads, lambda bi, h: _dot_nt(Qhb[bi, h], Kpb[bi, h // per]) * w_intra[bi, h])
    qc = each(heads, lambda bi, h: _dot(Qhb[bi, h], C0p[bi, h // per]))
    num = each(heads, lambda bi, h: _dot(s[bi, h], vb[bi, h]) + w_inter[bi, h] * qc[bi, h])
    den = each(heads, lambda bi, h: (jnp.sum(s[bi, h], axis=1, keepdims=True) + w_inter[bi, h]
                                     * jnp.sum(Qh[bi, h] * n0p[bi, h // per], axis=1, keepdims=True)))
    hh = each(heads, lambda bi, h: num[bi, h] / jnp.maximum(jnp.abs(den[bi, h]), jnp.exp(-m_col[bi, h])))
    for bi, h in heads:
        ml_ref[bi, :, hsl(h)] = _headnorm(hh[bi, h], mln_ref[:, hsl(h)]) * _sigmoid(mo_ref[bi, :, hsl(h)])
    mL = each(heads, lambda bi, h: m_col[bi, h][L - 1:L, :])
    wL_col = each(heads, lambda bi, h: jnp.exp(b_col[bi, h][L - 1:L, :] - b_col[bi, h] + ig_col[bi, h] - mL[bi, h]))
    wL0 = each(heads, lambda bi, h: jnp.exp(g_col[bi, h][L - 1:L, :] - mL[bi, h]))
    kw = each(heads, lambda bi, h: Kp[bi, h // per] * wL_col[bi, h])
    c_new = each(heads, lambda bi, h: wL0[bi, h] * C0p[bi, h // per] + _dot_tn(kw[bi, h], vb[bi, h]))
    n_new = each(heads, lambda bi, h: wL0[bi, h] * n0p[bi, h // per] + jnp.sum(kw[bi, h], axis=0, keepdims=True))
    for bi, p in pairs:
        cs[bi, psl(p), :] = _select_chain(rowp, [c_new[bi, p * per + u] for u in range(per)], dk)
        ns[bi, :, psl(p)] = _select_chain(lane1, [n_new[bi, p * per + u] for u in range(per)], dk)
    for bi in BI:
        m_new = m_prev[bi]
        for h in range(H):
            m_new = jnp.where(laneh == h, mL[bi, h], m_new)
        ms[bi] = m_new

    gqk = [gqk_ref[bi] for bi in BI]
    Q2 = [g[:, :HK] * scale for g in gqk]
    K2 = [g[:, HK:] for g in gqk]
    la = [_log_sigmoid(_dot(sm[bi], wa2_ref[...]) + ba_ref[...]) * (1.0 / GLA_TAU) for bi in BI]
    if valid < L:
        la = [jnp.where(t_col < valid, x, 0.0) for x in la]
    TL = [_dot3(tmat_ref[L:(2 + nlev) * L, :], x) for x in la]
    Q2b = [x.astype(BF16) for x in Q2]
    K2b = [x.astype(BF16) for x in K2]
    scores = each(heads, lambda bi, h: jnp.where(
        eye, _dot_nt(jnp.where(inhead[h % per], Q2b[bi][:, psl(h // per)], 0), K2b[bi][:, psl(h // per)]), 0.0))
    for i in range(nlev):
        n = L >> (i + 1)
        second = (t_col & n) != 0
        En = [jnp.exp(t[(1 + i) * L:(2 + i) * L]) for t in TL]
        X = [(jnp.where(second, Q2[bi], K2[bi]) * En[bi]).astype(BF16) for bi in BI]
        Xk = [jnp.where(second, 0, x) for x in X]
        sn = each(heads, lambda bi, h: _dot_nt(
            jnp.where(second & inhead[h % per], X[bi][:, psl(h // per)], 0), Xk[bi][:, psl(h // per)]))
        if i > 0:
            sh = int(round(math.log2(2 * n)))
            same = (tt >> sh) == (sc >> sh)
            sn = each(heads, lambda bi, h: jnp.where(same, sn[bi, h], 0.0))
        scores = each(heads, lambda bi, h: scores[bi, h] + sn[bi, h])
    A = [t[0:L] for t in TL]
    AL = [a[L - 1:L, :] for a in A]
    QA = [(Q2[bi] * jnp.exp(A[bi])).astype(BF16) for bi in BI]
    kd = [(K2[bi] * jnp.exp(AL[bi] - A[bi])).astype(BF16) for bi in BI]
    eAL = [jnp.exp(a) for a in AL]
    S0p = each(pairs, lambda bi, p: ss[bi, psl(p), :])
    dec_col = each(pairs, lambda bi, p: jnp.sum(jnp.where(ek, eAL[bi][:, psl(p)], 0.0), axis=1, keepdims=True))
    v2b = each(heads, lambda bi, h: gv_ref[bi, :, hsl(h)].astype(BF16))
    qs = each(heads, lambda bi, h: _dot(jnp.where(inhead[h % per], QA[bi][:, psl(h // per)], 0), S0p[bi, h // per]))
    o = each(heads, lambda bi, h: _dot(scores[bi, h], v2b[bi, h]) + qs[bi, h])
    for bi, h in heads:
        gg = gg_ref[bi, :, hsl(h)]
        gla_ref[bi, :, hsl(h)] = _headnorm(o[bi, h], gln_ref[:, hsl(h)]) * (gg * _sigmoid(gg))
    upd = each(heads, lambda bi, h: _dot_tn(kd[bi][:, psl(h // per)], v2b[bi, h]))
    for bi, p in pairs:
        ss[bi, psl(p), :] = (dec_col[bi, p] * S0p[bi, p]
                             + _select_chain(rowp, [upd[bi, p * per + u] for u in range(per)], dk))

    @pl.when(c == nc - 1)
    def _():
        c_ref[0] = cs[...].reshape(bt, H, dk, dv)
        n_ref[0] = ns[...]
        m_ref[0] = ms[...]
        s_ref[0] = ss[...].reshape(bt, H, dk, dv)


def _sel_rows(r):
    return jnp.asarray(np.eye(r, LANES, dtype=np.float32), dtype=BF16)


def _ab_scan(P, B, nc, L, valid, bt, j, states, prev, prm):
    C0, n0, m0, S0 = states
    nst, _, H, dk, dv = C0.shape
    HK = H * dk
    assert P.shape[:2] == (B, nc * L) and dv == LANES and LANES % dk == 0 and B % bt == 0
    tmat, nlev = _ab_mats(L)
    wv = H * dv
    assert 2 * HK == wv
    nsm = 6 * wv // LANES
    sel = _sel_rows(2 * H)
    sec = lambda k: pl.BlockSpec((bt, L, wv), lambda b, c: (b, c, k))
    st_c = pl.BlockSpec((1, bt, H, dk, dv), lambda b, c: (j, b, 0, 0, 0))
    st_n = pl.BlockSpec((1, bt, 1, HK), lambda b, c: (j, b, 0, 0))
    st_m = pl.BlockSpec((1, bt, 1, H), lambda b, c: (j, b, 0, 0))
    full = lambda a: pl.BlockSpec(a.shape, lambda b, c: (0,) * a.ndim)
    n_alias = 0 if prev is None else 4
    consts = [prm["gate_bias"], prm["ml_norm"], prm["wa2"], prm["ba"], prm["gla_norm"], tmat, sel]
    in_specs = ([sec(k) for k in range(6)]
                + [pl.BlockSpec((bt, L, LANES), lambda b, c: (b, c, nsm)), st_c, st_n, st_m, st_c]
                + [full(a) for a in consts]
                + [pl.BlockSpec(memory_space=pl.ANY)] * n_alias)
    out_specs = [sec(0), sec(0), st_c, st_n, st_m, st_c]
    out_shape = [
        jax.ShapeDtypeStruct((B, nc * L, wv), F32), jax.ShapeDtypeStruct((B, nc * L, wv), F32),
        jax.ShapeDtypeStruct(C0.shape, F32), jax.ShapeDtypeStruct(n0.shape, F32),
        jax.ShapeDtypeStruct(m0.shape, F32), jax.ShapeDtypeStruct(S0.shape, F32),
    ]
    n_in = len(in_specs) - n_alias
    aliases = {n_in + k: 2 + k for k in range(n_alias)}
    args = [P] * 7 + [C0, n0, m0, S0] + consts + (list(prev) if prev is not None else [])
    return pl.pallas_call(
        functools.partial(_ab_kernel, L, valid, nlev, nc, bt, H, dk, dv, n_alias),
        grid=(B // bt, nc),
        in_specs=in_specs,
        out_specs=out_specs,
        out_shape=out_shape,
        input_output_aliases=aliases,
        scratch_shapes=[pltpu.VMEM((bt, HK, dv), F32), pltpu.VMEM((bt, 1, HK), F32),
                        pltpu.VMEM((bt, 1, H), F32), pltpu.VMEM((bt, HK, dv), F32)],
        compiler_params=pltpu.CompilerParams(
            dimension_semantics=("parallel", "arbitrary"), vmem_limit_bytes=VMEM_LIMIT),
        name="ab_scan",
    )(*args)


def _ssd_kernel(L, valid, nc, bt, G, hpg, hd, wave, n_alias,
                z_ref, x_ref, b_ref, c_ref, dt_ref, cv_ref, h0_ref,
                cw_ref, cb_ref, dtbr_ref, dtbc_ref, alr_ref, alc_ref, dsk_ref, nrm_ref,
                tmat_ref, exp_ref, sel_ref, *rest):
    y_ref, h_ref, catx, catb, catc, hs = rest[n_alias:]
    cidx = pl.program_id(1)
    gw = hpg * hd
    nh = G * hpg
    inner = nh * hd
    N = h0_ref.shape[4]
    W = cw_ref.shape[0]
    P0 = SUBLANES - (W - 1)
    per = LANES // hd
    cats = ((catx, 0, inner), (catb, inner, G * N), (catc, inner + G * N, G * N))

    @pl.when(cidx == 0)
    def _():
        for cat, off, wid in cats:
            cat[:, P0:SUBLANES, :] = cv_ref[0, :, :, off:off + wid]
        hs[...] = h0_ref[0].reshape(bt, inner, N)

    t_col = lax.broadcasted_iota(jnp.int32, (L, 1), 0)
    s_row = lax.broadcasted_iota(jnp.int32, (1, L), 1)
    tt = lax.broadcasted_iota(jnp.int32, (L, L), 0)
    sc = lax.broadcasted_iota(jnp.int32, (L, L), 1)
    causal = sc <= tt
    lane = lax.broadcasted_iota(jnp.int32, (L, LANES), 1)
    upper = tmat_ref[0:L, :]
    lower = tmat_ref[L:2 * L, :]

    def conv_silu(cat, bi, off, lo, wid):
        xall = cat[bi, :, lo:lo + wid]
        cols = slice(off + lo, off + lo + wid)
        acc = cb_ref[:, cols] + xall[SUBLANES:SUBLANES + L] * cw_ref[W - 1:W, cols]
        for w in range(W - 1):
            tap = pltpu.roll(xall, W - 1 - w, axis=0)[SUBLANES:SUBLANES + L]
            acc = acc + tap * cw_ref[w:w + 1, cols]
        return acc * _sigmoid(acc)

    for bi in range(bt):
        catx[bi, SUBLANES:SUBLANES + L, :] = x_ref[bi]
        catb[bi, SUBLANES:SUBLANES + L, :] = b_ref[bi]
        catc[bi, SUBLANES:SUBLANES + L, :] = c_ref[bi]

    BI = range(bt)
    each = lambda keys, f: {k: f(*k) for k in keys}
    gsl = lambda g: slice(g * gw, (g + 1) * gw)
    dt_raw = [dt_ref[bi] for bi in BI]
    dtc = [_softplus(x + dtbr_ref[...]) for x in dt_raw]
    dtr = [_softplus(_rows_of_transpose(sel_ref[...], x) + dtbc_ref[...]) for x in dt_raw]
    if valid < L:
        dtc = [jnp.where(t_col < valid, x, 0.0) for x in dtc]
        dtr = [jnp.where(s_row < valid, x, 0.0) for x in dtr]
    cs_col = [_dot3(lower, x * (-jnp.exp(alr_ref[...]))) for x in dtc]
    cs_row = [_dot3_right(x * (-jnp.exp(alc_ref[...])), upper) for x in dtr]
    cd3 = [_split3(jnp.concatenate([cs_col[bi], dtc[bi]], axis=0)) for bi in BI]

    problems = [(bi, g) for bi in BI for g in range(G)]
    for w0 in range(0, len(problems), wave):
        keys = problems[w0:w0 + wave]
        hkeys = [(bi, g, jl) for bi, g in keys for jl in range(hpg)]
        xa = each(keys, lambda bi, g: conv_silu(catx, bi, 0, g * gw, gw))
        Bm = each(keys, lambda bi, g: conv_silu(catb, bi, inner, g * N, N))
        Cm = each(keys, lambda bi, g: conv_silu(catc, bi, inner + G * N, g * N, N))
        ce = each(keys, lambda bi, g: (
            jnp.dot(cd3[bi][0], exp_ref[:, gsl(g)], preferred_element_type=F32)
            + jnp.dot(cd3[bi][1], exp_ref[:, gsl(g)], preferred_element_type=F32)
            + jnp.dot(cd3[bi][2], exp_ref[:, gsl(g)], preferred_element_type=F32)))
        cs_exp = each(keys, lambda bi, g: ce[bi, g][0:L])
        dt_exp = each(keys, lambda bi, g: ce[bi, g][L:2 * L])
        Bmb = each(keys, lambda bi, g: Bm[bi, g].astype(BF16))
        Cmb = each(keys, lambda bi, g: Cm[bi, g].astype(BF16))
        CB = each(keys, lambda bi, g: _dot_nt(Cmb[bi, g], Bmb[bi, g]))
        hs0 = each(keys, lambda bi, g: hs[bi, gsl(g), :])
        yc = each(keys, lambda bi, g: _dot_nt(Cmb[bi, g], hs0[bi, g]) * jnp.exp(cs_exp[bi, g]))
        xab = each(keys, lambda bi, g: xa[bi, g].astype(BF16))
        seg = each(hkeys, lambda bi, g, jl: jnp.where(
            causal, cs_col[bi][:, g * hpg + jl:g * hpg + jl + 1] - cs_row[bi][g * hpg + jl:g * hpg + jl + 1, :],
            -jnp.inf))
        mj = each(hkeys, lambda bi, g, jl: CB[bi, g] * (
            jnp.exp(seg[bi, g, jl]) * dtr[bi][g * hpg + jl:g * hpg + jl + 1, :]))
        yh = each(hkeys, lambda bi, g, jl: _dot(
            mj[bi, g, jl], xab[bi, g][:, (jl // per) * LANES:(jl // per + 1) * LANES]))
        yi = each(keys, lambda bi, g: jnp.concatenate(
            [_select_chain(lane, [yh[bi, g, p * per + u] for u in range(per)], hd)
             for p in range(gw // LANES)], axis=1))
        y = each(keys, lambda bi, g: yc[bi, g] + yi[bi, g] + dsk_ref[:, gsl(g)] * xa[bi, g])
        z = each(keys, lambda bi, g: z_ref[bi, :, gsl(g)])
        y = each(keys, lambda bi, g: y[bi, g] * (z[bi, g] * _sigmoid(z[bi, g])))
        y = each(keys, lambda bi, g: y[bi, g] * lax.rsqrt(
            jnp.mean(y[bi, g] * y[bi, g], axis=1, keepdims=True) + LN_EPS) * nrm_ref[:, gsl(g)])
        for bi, g in keys:
            y_ref[bi, :, gsl(g)] = y[bi, g]

        xw = each(keys, lambda bi, g: xa[bi, g] * (
            jnp.exp(cs_exp[bi, g][L - 1:L, :] - cs_exp[bi, g]) * dt_exp[bi, g]))
        upd = each(keys, lambda bi, g: _dot_tn(xw[bi, g], Bmb[bi, g]))
        dec = each(hkeys, lambda bi, g, jl: jnp.exp(cs_row[bi][g * hpg + jl:g * hpg + jl + 1, L - 1:L]))
        for bi, g, jl in hkeys:
            r0 = g * gw + jl * hd
            hs[bi, r0:r0 + hd, :] = (dec[bi, g, jl] * hs0[bi, g][jl * hd:(jl + 1) * hd, :]
                                     + upd[bi, g][jl * hd:(jl + 1) * hd, :])

    if nc > 1:
        for bi in BI:
            for cat, _, _ in cats:
                tail = cat[bi, P0 + L:SUBLANES + L, :]
                cat[bi, P0:SUBLANES, :] = tail

    @pl.when(cidx == nc - 1)
    def _():
        h_ref[0] = hs[...].reshape(h_ref.shape[1:])


def _ssd_scan(P, B, nc, L, valid, bt, j, h0, conv0, prev, prm):
    nst, _, nh, hd, N = h0.shape
    G = prm["groups"]
    hpg = nh // G
    gw = hpg * hd
    inner = nh * hd
    GN = G * N
    W1, cd = conv0.shape[2], conv0.shape[3]
    assert P.shape[:2] == (B, nc * L) and N == LANES and gw % LANES == 0 and B % bt == 0
    assert nh <= LANES and inner % GN == 0 and cd == inner + 2 * GN
    tmat = _ssd_mats(L)
    sel = _sel_rows(nh)
    wave = bt * G if L <= 2 * SUBLANES else 1
    o_dt = 2 * inner + 2 * GN
    assert o_dt % LANES == 0
    full = lambda a: pl.BlockSpec(a.shape, lambda b, c: (0,) * a.ndim)
    st_h = pl.BlockSpec((1, bt, nh, hd, N), lambda b, c: (j, b, 0, 0, 0))
    n_alias = 0 if prev is None else 1
    consts = [prm["conv_w"], prm["conv_b"], prm["dtb_row"], prm["dtb_col"], prm["alog_row"],
              prm["alog_col"], prm["dskip"], prm["norm"], tmat, prm["expand"], sel]
    in_specs = [
        pl.BlockSpec((bt, L, inner), lambda b, c: (b, c, 0)),
        pl.BlockSpec((bt, L, inner), lambda b, c: (b, c, 1)),
        pl.BlockSpec((bt, L, GN), lambda b, c: (b, c, 2 * inner // GN)),
        pl.BlockSpec((bt, L, GN), lambda b, c: (b, c, 2 * inner // GN + 1)),
        pl.BlockSpec((bt, L, LANES), lambda b, c: (b, c, o_dt // LANES)),
        pl.BlockSpec((1, bt, W1, cd), lambda b, c: (j, b, 0, 0)),
        st_h,
    ] + [full(a) for a in consts] + [pl.BlockSpec(memory_space=pl.ANY)] * n_alias
    out_specs = [pl.BlockSpec((bt, L, inner), lambda b, c: (b, c, 0)), st_h]
    out_shape = [jax.ShapeDtypeStruct((B, nc * L, inner), F32), jax.ShapeDtypeStruct(h0.shape, F32)]
    n_in = len(in_specs) - n_alias
    args = [P] * 5 + [conv0, h0] + consts + ([prev] if prev is not None else [])
    return pl.pallas_call(
        functools.partial(_ssd_kernel, L, valid, nc, bt, G, hpg, hd, wave, n_alias),
        grid=(B // bt, nc),
        in_specs=in_specs,
        out_specs=out_specs,
        out_shape=out_shape,
        input_output_aliases={n_in: 1} if n_alias else {},
        scratch_shapes=[pltpu.VMEM((bt, SUBLANES + L, inner), F32), pltpu.VMEM((bt, SUBLANES + L, GN), F32),
                        pltpu.VMEM((bt, SUBLANES + L, GN), F32), pltpu.VMEM((bt, inner, N), F32)],
        compiler_params=pltpu.CompilerParams(
            dimension_semantics=("parallel", "arbitrary"), vmem_limit_bytes=VMEM_LIMIT),
        name="ssd_scan",
    )(*args)


def _prep_ab(j, H, dk, dv, w_in, ig_bias, fg_bias, ml_norm, wa2, ba, gla_norm, w_out):
    rank = wa2.shape[1]
    qk, vv = H * dk, H * dv
    o = np.cumsum([0, qk, qk, vv, vv, H, H, qk, qk, vv, vv, rank])
    assert int(o[-1]) == w_in.shape[2]
    wj = w_in[j]
    small = jnp.concatenate([wj[:, o[4]:o[6]], wj[:, o[10]:o[11]]], axis=1)
    small = jnp.pad(small, ((0, 0), (0, LANES - small.shape[1])))
    w = jnp.concatenate([wj[:, :o[4]], wj[:, o[6]:o[10]], small], axis=1).astype(BF16)
    wa2p = jnp.zeros((LANES, qk), F32).at[2 * H:2 * H + rank, :].set(wa2[j]).astype(BF16)
    return {
        "w_in": w,
        "gate_bias": jnp.concatenate([ig_bias[j], fg_bias[j]]).astype(F32).reshape(2 * H, 1),
        "ml_norm": ml_norm[j].reshape(1, vv),
        "gla_norm": gla_norm[j].reshape(1, vv),
        "wa2": wa2p,
        "ba": ba[j].reshape(1, qk),
        "w_out_ml": w_out[j][:vv].astype(BF16),
        "w_out_gla": w_out[j][vv:].astype(BF16),
    }


def _prep_ssd(j, G, nh, hd, N, w_in, conv_w, conv_b, dt_bias, a_log, d_skip, norm_g, w_out):
    inner = nh * hd
    hpg = nh // G
    cd = inner + 2 * G * N
    wz, wxbc, wdt = w_in[j][:, :inner], w_in[j][:, inner:inner + cd], w_in[j][:, inner + cd:]
    wdt = jnp.pad(wdt, ((0, 0), (0, 2 * LANES - nh)))
    w = jnp.concatenate([wz, wxbc, wdt], axis=1).astype(BF16)
    lane_form = lambda v: jnp.pad(v, (0, LANES - nh)).reshape(1, LANES)
    e = np.zeros((LANES, inner), np.float32)
    for r in range(nh):
        e[r, r * hd:(r + 1) * hd] = 1.0
    return {
        "groups": G,
        "w_in": w,
        "conv_w": conv_w[j], "conv_b": conv_b[j].reshape(1, cd),
        "dtb_row": lane_form(dt_bias[j]), "dtb_col": dt_bias[j].reshape(nh, 1),
        "alog_row": lane_form(a_log[j]), "alog_col": a_log[j].reshape(nh, 1),
        "dskip": jnp.repeat(d_skip[j], hd).reshape(1, inner),
        "norm": norm_g[j].reshape(1, inner),
        "expand": jnp.asarray(e, dtype=BF16),
        "w_out": w_out[j].astype(BF16),
    }


def _pad_time(P, B, T, Tp):
    P = P.reshape(B, T, -1)
    return P if T == Tp else jnp.pad(P, ((0, 0), (0, Tp - T), (0, 0)))


def _unpad_time(Y, B, T, Tp):
    Y = Y if T == Tp else Y[:, :T]
    return Y.reshape(B * T, -1)


def _largest_divisor(n, cap):
    return max(d for d in range(1, cap + 1) if n % d == 0)


def _plan(B, T, chunk):
    if T % chunk == 0:
        return chunk, T // chunk, _largest_divisor(B, 2), 1
    L = max(SUBLANES, 1 << int(math.ceil(math.log2(T))))
    return L, 1, _largest_divisor(B, 8), _largest_divisor(B, 4)


def _trunk(x, states, ab_prm, ssd_prm, mlp, lns, alpha, chunk):
    B, T, D = x.shape
    mC, mn, mm, gS, sh, sconv = states
    n_ab, _, H, dk, dv = mC.shape
    L, nc, bt_ab, bt_ssd = _plan(B, T, chunk)
    Tp = nc * L
    valid = T if nc == 1 else L
    M = B * T
    tm = 512 if M % 512 == 0 else M
    tmm = 1024 if M % 1024 == 0 else tm
    X = x.reshape(M, D)
    ab_states = (mC, mn.reshape(n_ab, B, 1, H * dk), mm.reshape(n_ab, B, 1, H), gS)
    ab_out, h_out, ncv = None, None, []
    depth = len(mlp)
    for l in range(depth):
        j = l // 2
        if l % 2 == 0:
            p = ab_prm[j]
            P = _proj(X, p["w_in"], tm, p["w_in"].shape[1])
            res = _ab_scan(_pad_time(P, B, T, Tp), B, nc, L, valid, bt_ab, j, ab_states, ab_out, p)
            ab_out = res[2:]
            ys = [_unpad_time(res[0], B, T, Tp), _unpad_time(res[1], B, T, Tp)]
            ws = [p["w_out_ml"], p["w_out_gla"]]
        else:
            p = ssd_prm[j]
            P = _proj(X, p["w_in"], tm, p["w_in"].shape[1] // 2)
            inner = p["norm"].shape[1]
            cd = p["conv_b"].shape[1]
            W1 = sconv.shape[2]
            assert T >= W1
            ncv.append(P.reshape(B, T, -1)[:, T - W1:, inner:inner + cd])
            y, h_out = _ssd_scan(_pad_time(P, B, T, Tp), B, nc, L, valid, bt_ssd, j, sh, sconv, h_out, p)
            ys = [_unpad_time(y, B, T, Tp)]
            ws = [p["w_out"]]
        g1, b1, g2, b2 = lns[l]
        X = _outproj_ln(ys, ws, X, g1, b1, alpha, tm)
        w1, w2 = mlp[l]
        X = _mlp_ln(X, w1, w2, g2, b2, alpha, tmm, 1024)
    nC, nn_, nm, nS = ab_out
    return (X.reshape(B, T, D), nC, nn_.reshape(mn.shape), nm.reshape(mm.shape), nS, h_out, jnp.stack(ncv))


def kernel(x_prompt, x_sample, state_mlstm_C, state_mlstm_n, state_mlstm_m, state_gla_S, state_ssd_h,
           state_ssd_conv, ab_w_in, ab_ig_bias, ab_fg_bias, ab_ml_norm, ab_gla_wa2, ab_gla_ba, ab_gla_norm,
           ab_w_out, ssd_w_in, ssd_conv_w, ssd_conv_b, ssd_dt_bias, ssd_a_log, ssd_d, ssd_norm, ssd_w_out,
           mlp_w1, mlp_w2, ln_mix_g, ln_mix_b, ln_mlp_g, ln_mlp_b):
    depth = mlp_w1.shape[0]
    D = x_prompt.shape[2]
    alpha = (2 * depth) ** 0.25
    n_ab, _, H, dk, dv = state_mlstm_C.shape
    n_ssd, _, nh, hd, N = state_ssd_h.shape
    cd = state_ssd_conv.shape[3]
    G = (cd - nh * hd) // (2 * N)
    ab_prm = [_prep_ab(j, H, dk, dv, ab_w_in, ab_ig_bias, ab_fg_bias, ab_ml_norm, ab_gla_wa2,
                       ab_gla_ba, ab_gla_norm, ab_w_out) for j in range(n_ab)]
    ssd_prm = [_prep_ssd(j, G, nh, hd, N, ssd_w_in, ssd_conv_w, ssd_conv_b, ssd_dt_bias, ssd_a_log,
                         ssd_d, ssd_norm, ssd_w_out) for j in range(n_ssd)]
    mlp = [(mlp_w1[l].astype(BF16), mlp_w2[l].astype(BF16)) for l in range(depth)]
    lns = [(ln_mix_g[l].reshape(1, D), ln_mix_b[l].reshape(1, D),
            ln_mlp_g[l].reshape(1, D), ln_mlp_b[l].reshape(1, D)) for l in range(depth)]

    Bp = x_prompt.shape[0]
    zeros = (jnp.zeros((n_ab, Bp, H, dk, dv), F32), jnp.zeros((n_ab, Bp, H, dk), F32),
             jnp.zeros((n_ab, Bp, H), F32), jnp.zeros((n_ab, Bp, H, dk, dv), F32),
             jnp.zeros((n_ssd, Bp, nh, hd, N), F32), jnp.zeros((n_ssd, Bp) + state_ssd_conv.shape[2:], F32))
    carried = (state_mlstm_C, state_mlstm_n, state_mlstm_m, state_gla_S, state_ssd_h, state_ssd_conv)
    chunk = 128
    yp = _trunk(x_prompt, zeros, ab_prm, ssd_prm, mlp, lns, alpha, chunk)
    ys = _trunk(x_sample, carried, ab_prm, ssd_prm, mlp, lns, alpha, chunk)
    return (yp[0], ys[0]) + yp[1:] + ys[1:]
```

```python
import functools
import math

import jax
import jax.numpy as jnp
import numpy as np
from jax import lax
from jax.experimental import pallas as pl
from jax.experimental.pallas import tpu as pltpu

F32 = jnp.float32
BF16 = jnp.bfloat16

LN_EPS = 1e-5
GLA_TAU = 16.0
LANES = 128
SUBLANES = 8
VMEM_LIMIT = 48 * 1024 * 1024


def _dot(a, b):
    return jnp.dot(a.astype(BF16), b.astype(BF16), preferred_element_type=F32)


def _dot_nt(a, b):
    return lax.dot_general(a.astype(BF16), b.astype(BF16), (((1,), (1,)), ((), ())),
                           preferred_element_type=F32)


def _dot_tn(a, b):
    return lax.dot_general(a.astype(BF16), b.astype(BF16), (((0,), (0,)), ((), ())),
                           preferred_element_type=F32)


def _split3(x):
    hi = x.astype(BF16)
    r = x - hi.astype(F32)
    mid = r.astype(BF16)
    lo = (r - mid.astype(F32)).astype(BF16)
    return hi, mid, lo


def _dot3(t, x):
    hi, mid, lo = _split3(x)
    f = lambda p: jnp.dot(t, p, preferred_element_type=F32)
    return f(hi) + f(mid) + f(lo)


def _dot3_right(x, t):
    hi, mid, lo = _split3(x)
    f = lambda p: jnp.dot(p, t, preferred_element_type=F32)
    return f(hi) + f(mid) + f(lo)


def _rows_of_transpose(sel, x):
    hi, mid, lo = _split3(x)
    f = lambda p: lax.dot_general(sel, p, (((1,), (1,)), ((), ())), preferred_element_type=F32)
    return f(hi) + f(mid) + f(lo)


def _softplus(x):
    e = jnp.exp(-jnp.abs(x))
    u = 1.0 + e
    d = u - 1.0
    return jnp.maximum(x, 0.0) + jnp.where(d == 0.0, e, jnp.log(u) * (e / d))


def _log_sigmoid(x):
    return -_softplus(-x)


def _sigmoid(x):
    return 0.5 + 0.5 * jnp.tanh(0.5 * x)


def _silu(x):
    h = 0.5 * x
    return h + h * jnp.tanh(h)


def _seq_tile(ref, bi, cols, L, T):
    if T == L:
        return ref[bi, :, cols]
    r0 = bi * T
    a = (r0 // L) * L
    tile = ref[0, a:a + L, cols]
    return tile if r0 == a else pltpu.roll(tile, L - (r0 - a), axis=0)


def _store_seqs(ref, cols, vals, L, T):
    if T == L:
        for bi, v in enumerate(vals):
            ref[bi, :, cols] = v
        return
    per = L // T
    row = lax.broadcasted_iota(jnp.int32, vals[0].shape, 0)
    for a in range(len(vals) // per):
        out = vals[a * per]
        for k in range(1, per):
            out = jnp.where(row < k * T, out, pltpu.roll(vals[a * per + k], k * T, axis=0))
        ref[0, a * L:(a + 1) * L, cols] = out


def _layernorm_rows(r, g, b):
    mu = jnp.mean(r, axis=1, keepdims=True)
    d = r - mu
    var = jnp.mean(d * d, axis=1, keepdims=True)
    return d * lax.rsqrt(var + LN_EPS) * g + b


def _headnorm(h, g):
    mu = jnp.mean(h, axis=1, keepdims=True)
    d = h - mu
    var = jnp.mean(d * d, axis=1, keepdims=True)
    return d * lax.rsqrt(var + LN_EPS) * g


def _select_chain(idx, pieces, width):
    out = pieces[0]
    for u in range(1, len(pieces)):
        out = jnp.where(idx < u * width, out, pieces[u])
    return out


def _cumsum_mats(L):
    t = np.arange(L)[:, None]
    j = np.arange(L)[None, :]
    upper = (t <= j)
    lower = (j <= t)
    return upper, lower


def _ab_mats(L):
    nlev = int(round(math.log2(L)))
    assert 1 << nlev == L
    upper, lower = _cumsum_mats(L)
    t = np.arange(L)[:, None]
    j = np.arange(L)[None, :]
    mats = [upper, lower]
    for i in range(nlev):
        n = L >> (i + 1)
        mid = (t // (2 * n)) * (2 * n) + n - 1
        second = (t % (2 * n)) >= n
        m = np.where(second, (j > mid) & (j <= t), (j > t) & (j <= mid))
        mats.append(m)
    return jnp.asarray(np.concatenate(mats, axis=0).astype(np.float32), dtype=BF16), nlev


def _ssd_mats(L):
    upper, lower = _cumsum_mats(L)
    return jnp.asarray(np.concatenate([upper, lower], axis=0).astype(np.float32), dtype=BF16)


def _proj_kernel(x_ref, w_ref, o_ref):
    o_ref[...] = jnp.dot(x_ref[...].astype(BF16), w_ref[...], preferred_element_type=F32)


def _proj(x, w, tm, tn):
    M, K = x.shape
    N = w.shape[1]
    assert M % tm == 0 and N % tn == 0
    return pl.pallas_call(
        _proj_kernel,
        grid=(N // tn, M // tm),
        in_specs=[pl.BlockSpec((tm, K), lambda j, i: (i, 0)),
                  pl.BlockSpec((K, tn), lambda j, i: (0, j))],
        out_specs=pl.BlockSpec((tm, tn), lambda j, i: (i, j)),
        out_shape=jax.ShapeDtypeStruct((M, N), F32),
        compiler_params=pltpu.CompilerParams(
            dimension_semantics=("parallel", "parallel"), vmem_limit_bytes=VMEM_LIMIT),
        name="proj",
    )(x, w)


def _outproj_ln_kernel(alpha, n_in, *refs):
    ys = refs[:n_in]
    ws = refs[n_in:2 * n_in]
    x_ref, g_ref, b_ref, o_ref = refs[2 * n_in:]
    r = alpha * x_ref[...]
    for y_ref, w_ref in zip(ys, ws):
        r = r + jnp.dot(y_ref[...].astype(BF16), w_ref[...], preferred_element_type=F32)
    o_ref[...] = _layernorm_rows(r, g_ref[0], b_ref[0])


def _outproj_ln(ys, ws, x, g, b, l, alpha, tm):
    M, D = x.shape
    assert M % tm == 0
    n_in = len(ys)
    in_specs = ([pl.BlockSpec((tm, y.shape[1]), lambda i: (i, 0)) for y in ys]
                + [pl.BlockSpec(w.shape, lambda i: (0, 0)) for w in ws]
                + [pl.BlockSpec((tm, D), lambda i: (i, 0)),
                   pl.BlockSpec((1, 1, D), lambda i: (l, 0, 0)),
                   pl.BlockSpec((1, 1, D), lambda i: (l, 0, 0))])
    return pl.pallas_call(
        functools.partial(_outproj_ln_kernel, alpha, n_in),
        grid=(M // tm,),
        in_specs=in_specs,
        out_specs=pl.BlockSpec((tm, D), lambda i: (i, 0)),
        out_shape=jax.ShapeDtypeStruct((M, D), F32),
        compiler_params=pltpu.CompilerParams(
            dimension_semantics=("parallel",), vmem_limit_bytes=VMEM_LIMIT),
        name="outproj_ln",
    )(*ys, *ws, x, g, b)


def _mlp_kernel(alpha, nf, x_ref, w1_ref, w2_ref, g_ref, b_ref, o_ref, acc_ref):
    f = pl.program_id(1)

    @pl.when(f == 0)
    def _():
        acc_ref[...] = jnp.zeros_like(acc_ref)

    h = jnp.dot(x_ref[...].astype(BF16), w1_ref[0], preferred_element_type=F32)
    h = jnp.square(jnp.maximum(h, 0.0))
    acc_ref[...] += jnp.dot(h.astype(BF16), w2_ref[0], preferred_element_type=F32)

    @pl.when(f == nf - 1)
    def _():
        r = alpha * x_ref[...] + acc_ref[...]
        o_ref[...] = _layernorm_rows(r, g_ref[0], b_ref[0])


def _mlp_ln(x, w1, w2, g, b, l, alpha, tm, tf):
    M, D = x.shape
    Fdim = w1.shape[2]
    assert M % tm == 0 and Fdim % tf == 0
    nf = Fdim // tf
    return pl.pallas_call(
        functools.partial(_mlp_kernel, alpha, nf),
        grid=(M // tm, nf),
        in_specs=[pl.BlockSpec((tm, D), lambda i, f: (i, 0)),
                  pl.BlockSpec((1, D, tf), lambda i, f: (l, 0, f)),
                  pl.BlockSpec((1, tf, D), lambda i, f: (l, f, 0)),
                  pl.BlockSpec((1, 1, D), lambda i, f: (l, 0, 0)),
                  pl.BlockSpec((1, 1, D), lambda i, f: (l, 0, 0))],
        out_specs=pl.BlockSpec((tm, D), lambda i, f: (i, 0)),
        out_shape=jax.ShapeDtypeStruct((M, D), F32),
        scratch_shapes=[pltpu.VMEM((tm, D), F32)],
        compiler_params=pltpu.CompilerParams(
            dimension_semantics=("parallel", "arbitrary"), vmem_limit_bytes=VMEM_LIMIT),
        name="mlp_ln",
    )(x, w1, w2, g, b)


def _ab_kernel(L, valid, nlev, nc, bt, H, dk, dv, n_alias,
               qk_ref, v_ref, mo_ref, gqk_ref, gv_ref, gg_ref, sm_ref,
               c0_ref, n0_ref, m0_ref, s0_ref, gb_ref, mln_ref, wa2_ref, ba_ref, gln_ref, tmat_ref,
               sel_ref, *rest):
    ml_ref, gla_ref, c_ref, n_ref, m_ref, s_ref, cs, ns, ms, ss = rest[n_alias:]
    c = pl.program_id(1)
    per = LANES // dk
    HK = H * dk

    @pl.when(c == 0)
    def _():
        cs[...] = c0_ref[0].reshape(bt, HK, dv)
        ns[...] = n0_ref[0]
        ms[...] = m0_ref[0]
        ss[...] = s0_ref[0].reshape(bt, HK, dv)

    t_col = lax.broadcasted_iota(jnp.int32, (L, 1), 0)
    s_row = lax.broadcasted_iota(jnp.int32, (1, L), 1)
    tt = lax.broadcasted_iota(jnp.int32, (L, L), 0)
    sc = lax.broadcasted_iota(jnp.int32, (L, L), 1)
    causal = sc <= tt
    eye = sc == tt
    lane = lax.broadcasted_iota(jnp.int32, (L, LANES), 1)
    lane1 = lax.broadcasted_iota(jnp.int32, (1, LANES), 1)
    laneh = lax.broadcasted_iota(jnp.int32, (1, H), 1)
    rowp = lax.broadcasted_iota(jnp.int32, (LANES, dv), 0)
    ek = lax.broadcasted_iota(jnp.int32, (LANES, LANES), 0) == lax.broadcasted_iota(jnp.int32, (LANES, LANES), 1)
    inhead = [(lane >= u * dk) & (lane < (u + 1) * dk) for u in range(per)]
    upper = tmat_ref[0:L, :]
    scale = dk ** -0.5

    def to_col(row):
        return jnp.sum(jnp.where(eye, row, 0.0), axis=1, keepdims=True)

    BI = range(bt)
    pairs = [(bi, p) for bi in BI for p in range(H // per)]
    heads = [(bi, h) for bi in BI for h in range(H)]
    psl = lambda p: slice(p * LANES, (p + 1) * LANES)
    hsl = lambda h: slice(h * dv, (h + 1) * dv)
    each = lambda keys, f: {k: f(*k) for k in keys}
    neg_inf = -jnp.inf

    allc = slice(None)
    tile = lambda ref, bi, cols: _seq_tile(ref, bi, cols, L, valid)
    sm = [tile(sm_ref, bi, allc) for bi in BI]
    ig8 = [_rows_of_transpose(sel_ref[...], sm[bi]) + gb_ref[...] for bi in BI]
    lf8 = [_log_sigmoid(g) for g in ig8]
    if valid < L:
        ok = s_row < valid
        ig8 = [jnp.where(ok, g, neg_inf) for g in ig8]
        lf8 = [jnp.where(ok, g, 0.0) for g in lf8]
    b8 = [_dot3_right(g, upper) for g in lf8]
    m_prev = [ms[bi] for bi in BI]

    qk = [tile(qk_ref, bi, allc) for bi in BI]
    Qp = each(pairs, lambda bi, p: qk[bi][:, psl(p)])
    Kp = each(pairs, lambda bi, p: qk[bi][:, HK + p * LANES:HK + (p + 1) * LANES] * scale)
    Kpb = each(pairs, lambda bi, p: Kp[bi, p].astype(BF16))
    C0p = each(pairs, lambda bi, p: cs[bi, psl(p), :])
    n0p = each(pairs, lambda bi, p: ns[bi][:, psl(p)])
    ig_row = each(heads, lambda bi, h: ig8[bi][h:h + 1, :])
    b_row = each(heads, lambda bi, h: b8[bi][H + h:H + h + 1, :])
    b_col = each(heads, lambda bi, h: to_col(b_row[bi, h]))
    ig_col = each(heads, lambda bi, h: to_col(ig_row[bi, h]))
    D = each(heads, lambda bi, h: jnp.where(causal, b_col[bi, h] - b_row[bi, h] + ig_row[bi, h], neg_inf))
    g_col = each(heads, lambda bi, h: b_col[bi, h] + m_prev[bi][:, h:h + 1])
    m_col = each(heads, lambda bi, h: jnp.maximum(g_col[bi, h], jnp.max(D[bi, h], axis=1, keepdims=True)))
    w_intra = each(heads, lambda bi, h: jnp.exp(D[bi, h] - m_col[bi, h]))
    w_inter = each(heads, lambda bi, h: jnp.exp(g_col[bi, h] - m_col[bi, h]))
    Qh = each(heads, lambda bi, h: jnp.where(inhead[h % per], Qp[bi, h // per], 0.0))
    Qhb = each(heads, lambda bi, h: Qh[bi, h].astype(BF16))
    vb = each(heads, lambda bi, h: tile(v_ref, bi, hsl(h)).astype(BF16))
    s = each(heads, lambda bi, h: _dot_nt(Qhb[bi, h], Kpb[bi, h // per]) * w_intra[bi, h])
    qc = each(heads, lambda bi, h: _dot(Qhb[bi, h], C0p[bi, h // per]))
    num = each(heads, lambda bi, h: _dot(s[bi, h], vb[bi, h]) + w_inter[bi, h] * qc[bi, h])
    den = each(heads, lambda bi, h: (jnp.sum(s[bi, h], axis=1, keepdims=True) + w_inter[bi, h]
                                     * jnp.sum(Qh[bi, h] * n0p[bi, h // per], axis=1, keepdims=True)))
    hh = each(heads, lambda bi, h: num[bi, h] / jnp.maximum(jnp.abs(den[bi, h]), jnp.exp(-m_col[bi, h])))
    ml = each(heads, lambda bi, h: (_headnorm(hh[bi, h], mln_ref[:, hsl(h)])
                                    * _sigmoid(tile(mo_ref, bi, hsl(h)))))
    for h in range(H):
        _store_seqs(ml_ref, hsl(h), [ml[bi, h] for bi in BI], L, valid)
    mL = each(heads, lambda bi, h: m_col[bi, h][L - 1:L, :])
    wL_col = each(heads, lambda bi, h: jnp.exp(b_col[bi, h][L - 1:L, :] - b_col[bi, h] + ig_col[bi, h] - mL[bi, h]))
    wL0 = each(heads, lambda bi, h: jnp.exp(g_col[bi, h][L - 1:L, :] - mL[bi, h]))
    kw = each(heads, lambda bi, h: Kp[bi, h // per] * wL_col[bi, h])
    c_new = each(heads, lambda bi, h: wL0[bi, h] * C0p[bi, h // per] + _dot_tn(kw[bi, h], vb[bi, h]))
    n_new = each(heads, lambda bi, h: wL0[bi, h] * n0p[bi, h // per] + jnp.sum(kw[bi, h], axis=0, keepdims=True))
    for bi, p in pairs:
        cs[bi, psl(p), :] = _select_chain(rowp, [c_new[bi, p * per + u] for u in range(per)], dk)
        ns[bi, :, psl(p)] = _select_chain(lane1, [n_new[bi, p * per + u] for u in range(per)], dk)
    for bi in BI:
        m_new = m_prev[bi]
        for h in range(H):
            m_new = jnp.where(laneh == h, mL[bi, h], m_new)
        ms[bi] = m_new

    gqk = [tile(gqk_ref, bi, allc) for bi in BI]
    Q2 = [g[:, :HK] * scale for g in gqk]
    K2 = [g[:, HK:] for g in gqk]
    if valid < L:
        K2 = [jnp.where(t_col < valid, x, 0.0) for x in K2]
    la = [_log_sigmoid(_dot(sm[bi], wa2_ref[...]) + ba_ref[...]) * (1.0 / GLA_TAU) for bi in BI]
    if valid < L:
        la = [jnp.where(t_col < valid, x, 0.0) for x in la]
    TL = [_dot3(tmat_ref[L:(2 + nlev) * L, :], x) for x in la]
    Q2b = [x.astype(BF16) for x in Q2]
    K2b = [x.astype(BF16) for x in K2]
    scores = each(heads, lambda bi, h: jnp.where(
        eye, _dot_nt(jnp.where(inhead[h % per], Q2b[bi][:, psl(h // per)], 0), K2b[bi][:, psl(h // per)]), 0.0))
    for i in range(nlev):
        n = L >> (i + 1)
        second = (t_col & n) != 0
        En = [jnp.exp(t[(1 + i) * L:(2 + i) * L]) for t in TL]
        X = [(jnp.where(second, Q2[bi], K2[bi]) * En[bi]).astype(BF16) for bi in BI]
        Xk = [jnp.where(second, 0, x) for x in X]
        sn = each(heads, lambda bi, h: _dot_nt(
            jnp.where(second & inhead[h % per], X[bi][:, psl(h // per)], 0), Xk[bi][:, psl(h // per)]))
        if i > 0:
            sh = int(round(math.log2(2 * n)))
            same = (tt >> sh) == (sc >> sh)
            sn = each(heads, lambda bi, h: jnp.where(same, sn[bi, h], 0.0))
        scores = each(heads, lambda bi, h: scores[bi, h] + sn[bi, h])
    A = [t[0:L] for t in TL]
    AL = [a[L - 1:L, :] for a in A]
    QA = [(Q2[bi] * jnp.exp(A[bi])).astype(BF16) for bi in BI]
    kd = [(K2[bi] * jnp.exp(AL[bi] - A[bi])).astype(BF16) for bi in BI]
    eAL = [jnp.exp(a) for a in AL]
    S0p = each(pairs, lambda bi, p: ss[bi, psl(p), :])
    dec_col = each(pairs, lambda bi, p: jnp.sum(jnp.where(ek, eAL[bi][:, psl(p)], 0.0), axis=1, keepdims=True))
    v2b = each(heads, lambda bi, h: tile(gv_ref, bi, hsl(h)).astype(BF16))
    qs = each(heads, lambda bi, h: _dot(jnp.where(inhead[h % per], QA[bi][:, psl(h // per)], 0), S0p[bi, h // per]))
    o = each(heads, lambda bi, h: _dot(scores[bi, h], v2b[bi, h]) + qs[bi, h])
    gla = each(heads, lambda bi, h: (_headnorm(o[bi, h], gln_ref[:, hsl(h)])
                                     * _silu(tile(gg_ref, bi, hsl(h)))))
    for h in range(H):
        _store_seqs(gla_ref, hsl(h), [gla[bi, h] for bi in BI], L, valid)
    upd = each(heads, lambda bi, h: _dot_tn(kd[bi][:, psl(h // per)], v2b[bi, h]))
    for bi, p in pairs:
        ss[bi, psl(p), :] = (dec_col[bi, p] * S0p[bi, p]
                             + _select_chain(rowp, [upd[bi, p * per + u] for u in range(per)], dk))

    @pl.when(c == nc - 1)
    def _():
        c_ref[0] = cs[...].reshape(bt, H, dk, dv)
        n_ref[0] = ns[...]
        m_ref[0] = ms[...]
        s_ref[0] = ss[...].reshape(bt, H, dk, dv)


def _sel_rows(r):
    return jnp.asarray(np.eye(r, LANES, dtype=np.float32), dtype=BF16)


def _row_blocks(B, nc, L, valid, bt):
    if valid == L:
        return (B, nc * L), lambda w, k: pl.BlockSpec((bt, L, w), lambda b, c: (b, c, k))
    assert nc == 1 and L % valid == 0 and (bt * valid) % L == 0
    return (B // bt, bt * valid), lambda w, k: pl.BlockSpec((1, bt * valid, w), lambda b, c: (b, 0, k))


def _ab_scan(P, B, nc, L, valid, bt, j, states, prev, prm):
    C0, n0, m0, S0 = states
    nst, _, H, dk, dv = C0.shape
    HK = H * dk
    lead, blk = _row_blocks(B, nc, L, valid, bt)
    assert P.shape[:2] == lead and dv == LANES and LANES % dk == 0 and B % bt == 0
    tmat, nlev = _ab_mats(L)
    wv = H * dv
    assert 2 * HK == wv
    nsm = 6 * wv // LANES
    sel = _sel_rows(2 * H)
    sec = lambda k: blk(wv, k)
    st_c = pl.BlockSpec((1, bt, H, dk, dv), lambda b, c: (j, b, 0, 0, 0))
    st_n = pl.BlockSpec((1, bt, 1, HK), lambda b, c: (j, b, 0, 0))
    st_m = pl.BlockSpec((1, bt, 1, H), lambda b, c: (j, b, 0, 0))
    full = lambda a: pl.BlockSpec(a.shape, lambda b, c: (0,) * a.ndim)
    n_alias = 0 if prev is None else 4
    consts = [prm["gate_bias"], prm["ml_norm"], prm["wa2"], prm["ba"], prm["gla_norm"], tmat, sel]
    in_specs = ([sec(k) for k in range(6)]
                + [blk(LANES, nsm), st_c, st_n, st_m, st_c]
                + [full(a) for a in consts]
                + [pl.BlockSpec(memory_space=pl.ANY)] * n_alias)
    out_specs = [sec(0), sec(0), st_c, st_n, st_m, st_c]
    out_shape = [
        jax.ShapeDtypeStruct(lead + (wv,), F32), jax.ShapeDtypeStruct(lead + (wv,), F32),
        jax.ShapeDtypeStruct(C0.shape, F32), jax.ShapeDtypeStruct(n0.shape, F32),
        jax.ShapeDtypeStruct(m0.shape, F32), jax.ShapeDtypeStruct(S0.shape, F32),
    ]
    n_in = len(in_specs) - n_alias
    aliases = {n_in + k: 2 + k for k in range(n_alias)}
    args = [P] * 7 + [C0, n0, m0, S0] + consts + (list(prev) if prev is not None else [])
    return pl.pallas_call(
        functools.partial(_ab_kernel, L, valid, nlev, nc, bt, H, dk, dv, n_alias),
        grid=(B // bt, nc),
        in_specs=in_specs,
        out_specs=out_specs,
        out_shape=out_shape,
        input_output_aliases=aliases,
        scratch_shapes=[pltpu.VMEM((bt, HK, dv), F32), pltpu.VMEM((bt, 1, HK), F32),
                        pltpu.VMEM((bt, 1, H), F32), pltpu.VMEM((bt, HK, dv), F32)],
        compiler_params=pltpu.CompilerParams(
            dimension_semantics=("parallel", "arbitrary"), vmem_limit_bytes=VMEM_LIMIT),
        name="ab_scan",
    )(*args)


def _ssd_kernel(L, valid, nc, bt, G, hpg, hd, wave, n_alias,
                z_ref, x_ref, b_ref, c_ref, dt_ref, cv_ref, h0_ref,
                cw_ref, cb_ref, dtbr_ref, dtbc_ref, alr_ref, alc_ref, dsk_ref, nrm_ref,
                tmat_ref, exp_ref, sel_ref, *rest):
    y_ref, h_ref, catx, catb, catc, hs = rest[n_alias:]
    cidx = pl.program_id(1)
    gw = hpg * hd
    nh = G * hpg
    inner = nh * hd
    N = h0_ref.shape[4]
    W = cw_ref.shape[0]
    P0 = SUBLANES - (W - 1)
    per = LANES // hd
    cats = ((catx, 0, inner), (catb, inner, G * N), (catc, inner + G * N, G * N))

    @pl.when(cidx == 0)
    def _():
        for cat, off, wid in cats:
            cat[:, P0:SUBLANES, :] = cv_ref[0, :, :, off:off + wid]
        hs[...] = h0_ref[0].reshape(bt, inner, N)

    t_col = lax.broadcasted_iota(jnp.int32, (L, 1), 0)
    s_row = lax.broadcasted_iota(jnp.int32, (1, L), 1)
    tt = lax.broadcasted_iota(jnp.int32, (L, L), 0)
    sc = lax.broadcasted_iota(jnp.int32, (L, L), 1)
    causal = sc <= tt
    lane = lax.broadcasted_iota(jnp.int32, (L, LANES), 1)
    upper = tmat_ref[0:L, :]
    lower = tmat_ref[L:2 * L, :]

    def conv_silu(cat, bi, off, lo, wid):
        xall = cat[bi, :, lo:lo + wid]
        cols = slice(off + lo, off + lo + wid)
        acc = cb_ref[:, cols] + xall[SUBLANES:SUBLANES + L] * cw_ref[W - 1:W, cols]
        for w in range(W - 1):
            tap = pltpu.roll(xall, W - 1 - w, axis=0)[SUBLANES:SUBLANES + L]
            acc = acc + tap * cw_ref[w:w + 1, cols]
        return _silu(acc)

    allc = slice(None)
    tile = lambda ref, bi, cols: _seq_tile(ref, bi, cols, L, valid)
    for bi in range(bt):
        catx[bi, SUBLANES:SUBLANES + L, :] = tile(x_ref, bi, allc)
        catb[bi, SUBLANES:SUBLANES + L, :] = tile(b_ref, bi, allc)
        catc[bi, SUBLANES:SUBLANES + L, :] = tile(c_ref, bi, allc)

    BI = range(bt)
    each = lambda keys, f: {k: f(*k) for k in keys}
    gsl = lambda g: slice(g * gw, (g + 1) * gw)
    dt_raw = [tile(dt_ref, bi, allc) for bi in BI]
    dtc = [_softplus(x + dtbr_ref[...]) for x in dt_raw]
    dtr = [_softplus(_rows_of_transpose(sel_ref[...], x) + dtbc_ref[...]) for x in dt_raw]
    if valid < L:
        dtc = [jnp.where(t_col < valid, x, 0.0) for x in dtc]
        dtr = [jnp.where(s_row < valid, x, 0.0) for x in dtr]
    cs_col = [_dot3(lower, x * (-jnp.exp(alr_ref[...]))) for x in dtc]
    cs_row = [_dot3_right(x * (-jnp.exp(alc_ref[...])), upper) for x in dtr]
    cd3 = [_split3(jnp.concatenate([cs_col[bi], dtc[bi]], axis=0)) for bi in BI]

    problems = [(bi, g) for bi in BI for g in range(G)]
    for w0 in range(0, len(problems), wave):
        keys = problems[w0:w0 + wave]
        hkeys = [(bi, g, jl) for bi, g in keys for jl in range(hpg)]
        xa = each(keys, lambda bi, g: conv_silu(catx, bi, 0, g * gw, gw))
        Bm = each(keys, lambda bi, g: conv_silu(catb, bi, inner, g * N, N))
        Cm = each(keys, lambda bi, g: conv_silu(catc, bi, inner + G * N, g * N, N))
        ce = each(keys, lambda bi, g: (
            jnp.dot(cd3[bi][0], exp_ref[:, gsl(g)], preferred_element_type=F32)
            + jnp.dot(cd3[bi][1], exp_ref[:, gsl(g)], preferred_element_type=F32)
            + jnp.dot(cd3[bi][2], exp_ref[:, gsl(g)], preferred_element_type=F32)))
        cs_exp = each(keys, lambda bi, g: ce[bi, g][0:L])
        dt_exp = each(keys, lambda bi, g: ce[bi, g][L:2 * L])
        Bmb = each(keys, lambda bi, g: Bm[bi, g].astype(BF16))
        Cmb = each(keys, lambda bi, g: Cm[bi, g].astype(BF16))
        CB = each(keys, lambda bi, g: _dot_nt(Cmb[bi, g], Bmb[bi, g]))
        hs0 = each(keys, lambda bi, g: hs[bi, gsl(g), :])
        yc = each(keys, lambda bi, g: _dot_nt(Cmb[bi, g], hs0[bi, g]) * jnp.exp(cs_exp[bi, g]))
        xab = each(keys, lambda bi, g: xa[bi, g].astype(BF16))
        seg = each(hkeys, lambda bi, g, jl: jnp.where(
            causal, cs_col[bi][:, g * hpg + jl:g * hpg + jl + 1] - cs_row[bi][g * hpg + jl:g * hpg + jl + 1, :],
            -jnp.inf))
        mj = each(hkeys, lambda bi, g, jl: CB[bi, g] * (
            jnp.exp(seg[bi, g, jl]) * dtr[bi][g * hpg + jl:g * hpg + jl + 1, :]))
        yh = each(hkeys, lambda bi, g, jl: _dot(
            mj[bi, g, jl], xab[bi, g][:, (jl // per) * LANES:(jl // per + 1) * LANES]))
        yi = each(keys, lambda bi, g: jnp.concatenate(
            [_select_chain(lane, [yh[bi, g, p * per + u] for u in range(per)], hd)
             for p in range(gw // LANES)], axis=1))
        y = each(keys, lambda bi, g: yc[bi, g] + yi[bi, g] + dsk_ref[:, gsl(g)] * xa[bi, g])
        y = each(keys, lambda bi, g: y[bi, g] * _silu(tile(z_ref, bi, gsl(g))))
        y = each(keys, lambda bi, g: y[bi, g] * lax.rsqrt(
            jnp.mean(y[bi, g] * y[bi, g], axis=1, keepdims=True) + LN_EPS) * nrm_ref[:, gsl(g)])
        if valid == L:
            for bi, g in keys:
                y_ref[bi, :, gsl(g)] = y[bi, g]
        else:
            assert len(keys) == bt * G
            for g in range(G):
                _store_seqs(y_ref, gsl(g), [y[bi, g] for bi in BI], L, valid)

        xw = each(keys, lambda bi, g: xa[bi, g] * (
            jnp.exp(cs_exp[bi, g][L - 1:L, :] - cs_exp[bi, g]) * dt_exp[bi, g]))
        upd = each(keys, lambda bi, g: _dot_tn(xw[bi, g], Bmb[bi, g]))
        dec = each(hkeys, lambda bi, g, jl: jnp.exp(cs_row[bi][g * hpg + jl:g * hpg + jl + 1, L - 1:L]))
        for bi, g, jl in hkeys:
            r0 = g * gw + jl * hd
            hs[bi, r0:r0 + hd, :] = (dec[bi, g, jl] * hs0[bi, g][jl * hd:(jl + 1) * hd, :]
                                     + upd[bi, g][jl * hd:(jl + 1) * hd, :])

    if nc > 1:
        for bi in BI:
            for cat, _, _ in cats:
                tail = cat[bi, P0 + L:SUBLANES + L, :]
                cat[bi, P0:SUBLANES, :] = tail

    @pl.when(cidx == nc - 1)
    def _():
        h_ref[0] = hs[...].reshape(h_ref.shape[1:])


def _ssd_scan(P, B, nc, L, valid, bt, j, h0, conv0, prev, prm):
    nst, _, nh, hd, N = h0.shape
    G = prm["groups"]
    hpg = nh // G
    gw = hpg * hd
    inner = nh * hd
    GN = G * N
    W1, cd = conv0.shape[2], conv0.shape[3]
    lead, blk = _row_blocks(B, nc, L, valid, bt)
    assert P.shape[:2] == lead and N == LANES and gw % LANES == 0 and B % bt == 0
    assert nh <= LANES and inner % GN == 0 and cd == inner + 2 * GN
    tmat = _ssd_mats(L)
    sel = _sel_rows(nh)
    wave = bt * G if L <= 2 * SUBLANES else 1
    o_dt = 2 * inner + 2 * GN
    assert o_dt % LANES == 0
    full = lambda a: pl.BlockSpec(a.shape, lambda b, c: (0,) * a.ndim)
    st_h = pl.BlockSpec((1, bt, nh, hd, N), lambda b, c: (j, b, 0, 0, 0))
    n_alias = 0 if prev is None else 1
    consts = [prm["conv_w"], prm["conv_b"], prm["dtb_row"], prm["dtb_col"], prm["alog_row"],
              prm["alog_col"], prm["dskip"], prm["norm"], tmat, prm["expand"], sel]
    in_specs = [
        blk(inner, 0), blk(inner, 1), blk(GN, 2 * inner // GN), blk(GN, 2 * inner // GN + 1),
        blk(LANES, o_dt // LANES),
        pl.BlockSpec((1, bt, W1, cd), lambda b, c: (j, b, 0, 0)),
        st_h,
    ] + [full(a) for a in consts] + [pl.BlockSpec(memory_space=pl.ANY)] * n_alias
    out_specs = [blk(inner, 0), st_h]
    out_shape = [jax.ShapeDtypeStruct(lead + (inner,), F32), jax.ShapeDtypeStruct(h0.shape, F32)]
    n_in = len(in_specs) - n_alias
    args = [P] * 5 + [conv0, h0] + consts + ([prev] if prev is not None else [])
    return pl.pallas_call(
        functools.partial(_ssd_kernel, L, valid, nc, bt, G, hpg, hd, wave, n_alias),
        grid=(B // bt, nc),
        in_specs=in_specs,
        out_specs=out_specs,
        out_shape=out_shape,
        input_output_aliases={n_in: 1} if n_alias else {},
        scratch_shapes=[pltpu.VMEM((bt, SUBLANES + L, inner), F32), pltpu.VMEM((bt, SUBLANES + L, GN), F32),
                        pltpu.VMEM((bt, SUBLANES + L, GN), F32), pltpu.VMEM((bt, inner, N), F32)],
        compiler_params=pltpu.CompilerParams(
            dimension_semantics=("parallel", "arbitrary"), vmem_limit_bytes=VMEM_LIMIT),
        name="ssd_scan",
    )(*args)


def _prep_ab(j, H, dk, dv, w_in, ig_bias, fg_bias, ml_norm, wa2, ba, gla_norm, w_out):
    rank = wa2.shape[1]
    qk, vv = H * dk, H * dv
    o = np.cumsum([0, qk, qk, vv, vv, H, H, qk, qk, vv, vv, rank])
    assert int(o[-1]) == w_in.shape[2]
    wj = w_in[j]
    small = jnp.concatenate([wj[:, o[4]:o[6]], wj[:, o[10]:o[11]]], axis=1)
    small = jnp.pad(small, ((0, 0), (0, LANES - small.shape[1])))
    w = jnp.concatenate([wj[:, :o[4]], wj[:, o[6]:o[10]], small], axis=1).astype(BF16)
    wa2p = jnp.zeros((LANES, qk), F32).at[2 * H:2 * H + rank, :].set(wa2[j]).astype(BF16)
    return {
        "w_in": w,
        "gate_bias": jnp.concatenate([ig_bias[j], fg_bias[j]]).astype(F32).reshape(2 * H, 1),
        "ml_norm": ml_norm[j].reshape(1, vv),
        "gla_norm": gla_norm[j].reshape(1, vv),
        "wa2": wa2p,
        "ba": ba[j].reshape(1, qk),
        "w_out_ml": w_out[j][:vv].astype(BF16),
        "w_out_gla": w_out[j][vv:].astype(BF16),
    }


def _prep_ssd(j, G, nh, hd, N, w_in, conv_w, conv_b, dt_bias, a_log, d_skip, norm_g, w_out):
    inner = nh * hd
    hpg = nh // G
    cd = inner + 2 * G * N
    wz, wxbc, wdt = w_in[j][:, :inner], w_in[j][:, inner:inner + cd], w_in[j][:, inner + cd:]
    wdt = jnp.pad(wdt, ((0, 0), (0, 2 * LANES - nh)))
    w = jnp.concatenate([wz, wxbc, wdt], axis=1).astype(BF16)
    lane_form = lambda v: jnp.pad(v, (0, LANES - nh)).reshape(1, LANES)
    e = np.zeros((LANES, inner), np.float32)
    for r in range(nh):
        e[r, r * hd:(r + 1) * hd] = 1.0
    return {
        "groups": G,
        "w_in": w,
        "conv_w": conv_w[j], "conv_b": conv_b[j].reshape(1, cd),
        "dtb_row": lane_form(dt_bias[j]), "dtb_col": dt_bias[j].reshape(nh, 1),
        "alog_row": lane_form(a_log[j]), "alog_col": a_log[j].reshape(nh, 1),
        "dskip": jnp.repeat(d_skip[j], hd).reshape(1, inner),
        "norm": norm_g[j].reshape(1, inner),
        "expand": jnp.asarray(e, dtype=BF16),
        "w_out": w_out[j].astype(BF16),
    }


def _largest_divisor(n, cap, step=1):
    return max(d for d in range(step, cap + 1, step) if n % d == 0)


def _plan(B, T, chunk):
    if T % chunk == 0:
        return chunk, T // chunk, _largest_divisor(B, 2), 1
    L = max(SUBLANES, 1 << int(math.ceil(math.log2(T))))
    per = L // T
    assert L % T == 0 and B % per == 0
    return L, 1, _largest_divisor(B, 8, per), _largest_divisor(B, 4, per)


def _trunk(x, states, ab_prm, ssd_prm, mlp, lns, alpha, chunk):
    B, T, D = x.shape
    mC, mn, mm, gS, sh, sconv = states
    n_ab, _, H, dk, dv = mC.shape
    L, nc, bt_ab, bt_ssd = _plan(B, T, chunk)
    valid = T if nc == 1 else L
    M = B * T
    tm = 512 if M % 512 == 0 else M
    tmm = 1024 if M % 1024 == 0 else tm
    X = x.reshape(M, D)
    ab_states = (mC, mn.reshape(n_ab, B, 1, H * dk), mm.reshape(n_ab, B, 1, H), gS)
    ab_out, h_out, ncv = None, None, []
    depth = mlp[0].shape[0]
    for l in range(depth):
        j = l // 2
        if l % 2 == 0:
            p = ab_prm[j]
            P = _proj(X, p["w_in"], tm, p["w_in"].shape[1])
            lead = _row_blocks(B, nc, L, valid, bt_ab)[0]
            res = _ab_scan(P.reshape(lead + (-1,)), B, nc, L, valid, bt_ab, j, ab_states, ab_out, p)
            ab_out = res[2:]
            ys = [res[0].reshape(M, -1), res[1].reshape(M, -1)]
            ws = [p["w_out_ml"], p["w_out_gla"]]
        else:
            p = ssd_prm[j]
            P = _proj(X, p["w_in"], tm, p["w_in"].shape[1] // 2)
            inner = p["norm"].shape[1]
            cd = p["conv_b"].shape[1]
            W1 = sconv.shape[2]
            assert T >= W1
            ncv.append(P.reshape(B, T, -1)[:, T - W1:, inner:inner + cd])
            lead = _row_blocks(B, nc, L, valid, bt_ssd)[0]
            y, h_out = _ssd_scan(P.reshape(lead + (-1,)), B, nc, L, valid, bt_ssd, j, sh, sconv, h_out, p)
            ys = [y.reshape(M, -1)]
            ws = [p["w_out"]]
        X = _outproj_ln(ys, ws, X, lns[0], lns[1], l, alpha, tm)
        X = _mlp_ln(X, mlp[0], mlp[1], lns[2], lns[3], l, alpha, tmm, 1024)
    nC, nn_, nm, nS = ab_out
    return (X.reshape(B, T, D), nC, nn_.reshape(mn.shape), nm.reshape(mm.shape), nS, h_out, jnp.stack(ncv))


def kernel(x_prompt, x_sample, state_mlstm_C, state_mlstm_n, state_mlstm_m, state_gla_S, state_ssd_h,
           state_ssd_conv, ab_w_in, ab_ig_bias, ab_fg_bias, ab_ml_norm, ab_gla_wa2, ab_gla_ba, ab_gla_norm,
           ab_w_out, ssd_w_in, ssd_conv_w, ssd_conv_b, ssd_dt_bias, ssd_a_log, ssd_d, ssd_norm, ssd_w_out,
           mlp_w1, mlp_w2, ln_mix_g, ln_mix_b, ln_mlp_g, ln_mlp_b):
    depth = mlp_w1.shape[0]
    D = x_prompt.shape[2]
    alpha = (2 * depth) ** 0.25
    n_ab, _, H, dk, dv = state_mlstm_C.shape
    n_ssd, _, nh, hd, N = state_ssd_h.shape
    cd = state_ssd_conv.shape[3]
    G = (cd - nh * hd) // (2 * N)
    ab_prm = [_prep_ab(j, H, dk, dv, ab_w_in, ab_ig_bias, ab_fg_bias, ab_ml_norm, ab_gla_wa2,
                       ab_gla_ba, ab_gla_norm, ab_w_out) for j in range(n_ab)]
    ssd_prm = [_prep_ssd(j, G, nh, hd, N, ssd_w_in, ssd_conv_w, ssd_conv_b, ssd_dt_bias, ssd_a_log,
                         ssd_d, ssd_norm, ssd_w_out) for j in range(n_ssd)]
    mlp = (mlp_w1.astype(BF16), mlp_w2.astype(BF16))
    lns = tuple(a.reshape(depth, 1, D) for a in (ln_mix_g, ln_mix_b, ln_mlp_g, ln_mlp_b))

    Bp = x_prompt.shape[0]
    zeros = (jnp.zeros((n_ab, Bp, H, dk, dv), F32), jnp.zeros((n_ab, Bp, H, dk), F32),
             jnp.zeros((n_ab, Bp, H), F32), jnp.zeros((n_ab, Bp, H, dk, dv), F32),
             jnp.zeros((n_ssd, Bp, nh, hd, N), F32), jnp.zeros((n_ssd, Bp) + state_ssd_conv.shape[2:], F32))
    carried = (state_mlstm_C, state_mlstm_n, state_mlstm_m, state_gla_S, state_ssd_h, state_ssd_conv)
    chunk = 128
    yp = _trunk(x_prompt, zeros, ab_prm, ssd_prm, mlp, lns, alpha, chunk)
    ys = _trunk(x_sample, carried, ab_prm, ssd_prm, mlp, lns, alpha, chunk)
    return (yp[0], ys[0]) + yp[1:] + ys[1:]
```

```python
import functools
import math

import jax
import jax.numpy as jnp
import numpy as np
from jax import lax
from jax.experimental import pallas as pl
from jax.experimental.pallas import tpu as pltpu

F32 = jnp.float32
BF16 = jnp.bfloat16

LN_EPS = 1e-5
GLA_TAU = 16.0
LANES = 128
SUBLANES = 8
VMEM_LIMIT = 48 * 1024 * 1024


def _dot(a, b):
    return jnp.dot(a.astype(BF16), b.astype(BF16), preferred_element_type=F32)


def _dot_nt(a, b):
    return lax.dot_general(a.astype(BF16), b.astype(BF16), (((1,), (1,)), ((), ())),
                           preferred_element_type=F32)


def _dot_tn(a, b):
    return lax.dot_general(a.astype(BF16), b.astype(BF16), (((0,), (0,)), ((), ())),
                           preferred_element_type=F32)


def _split3(x):
    hi = x.astype(BF16)
    r = x - hi.astype(F32)
    mid = r.astype(BF16)
    lo = (r - mid.astype(F32)).astype(BF16)
    return hi, mid, lo


def _dot3(t, x):
    hi, mid, lo = _split3(x)
    f = lambda p: jnp.dot(t, p, preferred_element_type=F32)
    return f(hi) + f(mid) + f(lo)


def _dot3_right(x, t):
    hi, mid, lo = _split3(x)
    f = lambda p: jnp.dot(p, t, preferred_element_type=F32)
    return f(hi) + f(mid) + f(lo)


def _rows_of_transpose(sel, x):
    hi, mid, lo = _split3(x)
    f = lambda p: lax.dot_general(sel, p, (((1,), (1,)), ((), ())), preferred_element_type=F32)
    return f(hi) + f(mid) + f(lo)


def _softplus(x):
    e = jnp.exp(-jnp.abs(x))
    u = 1.0 + e
    d = u - 1.0
    return jnp.maximum(x, 0.0) + jnp.where(d == 0.0, e, jnp.log(u) * (e / d))


def _log_sigmoid(x):
    return -_softplus(-x)


def _sigmoid(x):
    return 0.5 + 0.5 * jnp.tanh(0.5 * x)


def _silu(x):
    h = 0.5 * x
    return h + h * jnp.tanh(h)


def _seq_tile(ref, bi, cols, L, T):
    if T == L:
        return ref[bi, :, cols]
    r0 = bi * T
    a = (r0 // L) * L
    tile = ref[0, a:a + L, cols]
    return tile if r0 == a else pltpu.roll(tile, L - (r0 - a), axis=0)


def _store_seqs(ref, cols, vals, L, T):
    if T == L:
        for bi, v in enumerate(vals):
            ref[bi, :, cols] = v.astype(ref.dtype)
        return
    per = L // T
    row = lax.broadcasted_iota(jnp.int32, vals[0].shape, 0)
    for a in range(len(vals) // per):
        out = vals[a * per]
        for k in range(1, per):
            out = jnp.where(row < k * T, out, pltpu.roll(vals[a * per + k], k * T, axis=0))
        ref[0, a * L:(a + 1) * L, cols] = out


def _layernorm_rows(r, g, b):
    mu = jnp.mean(r, axis=1, keepdims=True)
    d = r - mu
    var = jnp.mean(d * d, axis=1, keepdims=True)
    return d * lax.rsqrt(var + LN_EPS) * g + b


def _headnorm(h, g):
    mu = jnp.mean(h, axis=1, keepdims=True)
    d = h - mu
    var = jnp.mean(d * d, axis=1, keepdims=True)
    return d * lax.rsqrt(var + LN_EPS) * g


def _select_chain(idx, pieces, width):
    out = pieces[0]
    for u in range(1, len(pieces)):
        out = jnp.where(idx < u * width, out, pieces[u])
    return out


def _cumsum_mats(L):
    t = np.arange(L)[:, None]
    j = np.arange(L)[None, :]
    upper = (t <= j)
    lower = (j <= t)
    return upper, lower


def _ab_mats(L):
    nlev = int(round(math.log2(L)))
    assert 1 << nlev == L
    upper, lower = _cumsum_mats(L)
    t = np.arange(L)[:, None]
    j = np.arange(L)[None, :]
    mats = [upper, lower]
    for i in range(nlev):
        n = L >> (i + 1)
        mid = (t // (2 * n)) * (2 * n) + n - 1
        second = (t % (2 * n)) >= n
        m = np.where(second, (j > mid) & (j <= t), (j > t) & (j <= mid))
        mats.append(m)
    return jnp.asarray(np.concatenate(mats, axis=0).astype(np.float32), dtype=BF16), nlev


def _ssd_mats(L):
    upper, lower = _cumsum_mats(L)
    return jnp.asarray(np.concatenate([upper, lower], axis=0).astype(np.float32), dtype=BF16)


def _proj_kernel(x_ref, w_ref, o_ref):
    o_ref[...] = jnp.dot(x_ref[...].astype(BF16), w_ref[...], preferred_element_type=F32)


def _proj(x, w, tm, tn):
    M, K = x.shape
    N = w.shape[1]
    assert M % tm == 0 and N % tn == 0
    return pl.pallas_call(
        _proj_kernel,
        grid=(N // tn, M // tm),
        in_specs=[pl.BlockSpec((tm, K), lambda j, i: (i, 0)),
                  pl.BlockSpec((K, tn), lambda j, i: (0, j))],
        out_specs=pl.BlockSpec((tm, tn), lambda j, i: (i, j)),
        out_shape=jax.ShapeDtypeStruct((M, N), F32),
        compiler_params=pltpu.CompilerParams(
            dimension_semantics=("parallel", "parallel"), vmem_limit_bytes=VMEM_LIMIT),
        name="proj",
    )(x, w)


def _outproj_ln_kernel(alpha, n_in, *refs):
    ys = refs[:n_in]
    ws = refs[n_in:2 * n_in]
    x_ref, g_ref, b_ref, o_ref = refs[2 * n_in:]
    r = alpha * x_ref[...]
    for y_ref, w_ref in zip(ys, ws):
        r = r + jnp.dot(y_ref[...].astype(BF16), w_ref[...], preferred_element_type=F32)
    o_ref[...] = _layernorm_rows(r, g_ref[0], b_ref[0])


def _outproj_ln(ys, ws, x, g, b, l, alpha, tm):
    M, D = x.shape
    assert M % tm == 0
    n_in = len(ys)
    in_specs = ([pl.BlockSpec((tm, y.shape[1]), lambda i: (i, 0)) for y in ys]
                + [pl.BlockSpec(w.shape, lambda i: (0, 0)) for w in ws]
                + [pl.BlockSpec((tm, D), lambda i: (i, 0)),
                   pl.BlockSpec((1, 1, D), lambda i: (l, 0, 0)),
                   pl.BlockSpec((1, 1, D), lambda i: (l, 0, 0))])
    return pl.pallas_call(
        functools.partial(_outproj_ln_kernel, alpha, n_in),
        grid=(M // tm,),
        in_specs=in_specs,
        out_specs=pl.BlockSpec((tm, D), lambda i: (i, 0)),
        out_shape=jax.ShapeDtypeStruct((M, D), F32),
        compiler_params=pltpu.CompilerParams(
            dimension_semantics=("parallel",), vmem_limit_bytes=VMEM_LIMIT),
        name="outproj_ln",
    )(*ys, *ws, x, g, b)


def _mlp_kernel(alpha, nf, x_ref, w1_ref, w2_ref, g_ref, b_ref, o_ref, acc_ref):
    f = pl.program_id(1)

    @pl.when(f == 0)
    def _():
        acc_ref[...] = jnp.zeros_like(acc_ref)

    h = jnp.dot(x_ref[...].astype(BF16), w1_ref[0], preferred_element_type=F32)
    h = jnp.square(jnp.maximum(h, 0.0))
    acc_ref[...] += jnp.dot(h.astype(BF16), w2_ref[0], preferred_element_type=F32)

    @pl.when(f == nf - 1)
    def _():
        r = alpha * x_ref[...] + acc_ref[...]
        o_ref[...] = _layernorm_rows(r, g_ref[0], b_ref[0])


def _mlp_ln(x, w1, w2, g, b, l, alpha, tm, tf):
    M, D = x.shape
    Fdim = w1.shape[2]
    assert M % tm == 0 and Fdim % tf == 0
    nf = Fdim // tf
    return pl.pallas_call(
        functools.partial(_mlp_kernel, alpha, nf),
        grid=(M // tm, nf),
        in_specs=[pl.BlockSpec((tm, D), lambda i, f: (i, 0)),
                  pl.BlockSpec((1, D, tf), lambda i, f: (l, 0, f)),
                  pl.BlockSpec((1, tf, D), lambda i, f: (l, f, 0)),
                  pl.BlockSpec((1, 1, D), lambda i, f: (l, 0, 0)),
                  pl.BlockSpec((1, 1, D), lambda i, f: (l, 0, 0))],
        out_specs=pl.BlockSpec((tm, D), lambda i, f: (i, 0)),
        out_shape=jax.ShapeDtypeStruct((M, D), F32),
        scratch_shapes=[pltpu.VMEM((tm, D), F32)],
        compiler_params=pltpu.CompilerParams(
            dimension_semantics=("parallel", "arbitrary"), vmem_limit_bytes=VMEM_LIMIT),
        name="mlp_ln",
    )(x, w1, w2, g, b)


def _ab_kernel(L, valid, nlev, nc, bt, H, dk, dv, n_alias,
               qk_ref, v_ref, mo_ref, gqk_ref, gv_ref, gg_ref, sm_ref,
               c0_ref, n0_ref, m0_ref, s0_ref, gb_ref, mln_ref, wa2_ref, ba_ref, gln_ref, tmat_ref,
               sel_ref, *rest):
    ml_ref, gla_ref, c_ref, n_ref, m_ref, s_ref, cs, ns, ms, ss = rest[n_alias:]
    c = pl.program_id(1)
    per = LANES // dk
    HK = H * dk

    @pl.when(c == 0)
    def _():
        cs[...] = c0_ref[0].reshape(bt, HK, dv)
        ns[...] = n0_ref[0]
        ms[...] = m0_ref[0]
        ss[...] = s0_ref[0].reshape(bt, HK, dv)

    t_col = lax.broadcasted_iota(jnp.int32, (L, 1), 0)
    s_row = lax.broadcasted_iota(jnp.int32, (1, L), 1)
    tt = lax.broadcasted_iota(jnp.int32, (L, L), 0)
    sc = lax.broadcasted_iota(jnp.int32, (L, L), 1)
    causal = sc <= tt
    eye = sc == tt
    lane = lax.broadcasted_iota(jnp.int32, (L, LANES), 1)
    lane1 = lax.broadcasted_iota(jnp.int32, (1, LANES), 1)
    laneh = lax.broadcasted_iota(jnp.int32, (1, H), 1)
    rowp = lax.broadcasted_iota(jnp.int32, (LANES, dv), 0)
    ek = lax.broadcasted_iota(jnp.int32, (LANES, LANES), 0) == lax.broadcasted_iota(jnp.int32, (LANES, LANES), 1)
    inhead = [(lane >= u * dk) & (lane < (u + 1) * dk) for u in range(per)]
    upper = tmat_ref[0:L, :]
    scale = dk ** -0.5

    def to_col(row):
        return jnp.sum(jnp.where(eye, row, 0.0), axis=1, keepdims=True)

    BI = range(bt)
    pairs = [(bi, p) for bi in BI for p in range(H // per)]
    heads = [(bi, h) for bi in BI for h in range(H)]
    psl = lambda p: slice(p * LANES, (p + 1) * LANES)
    hsl = lambda h: slice(h * dv, (h + 1) * dv)
    each = lambda keys, f: {k: f(*k) for k in keys}
    neg_inf = -jnp.inf

    allc = slice(None)
    tile = lambda ref, bi, cols: _seq_tile(ref, bi, cols, L, valid)
    sm = [tile(sm_ref, bi, allc) for bi in BI]
    ig8 = [_rows_of_transpose(sel_ref[...], sm[bi]) + gb_ref[...] for bi in BI]
    lf8 = [_log_sigmoid(g) for g in ig8]
    if valid < L:
        ok = s_row < valid
        ig8 = [jnp.where(ok, g, neg_inf) for g in ig8]
        lf8 = [jnp.where(ok, g, 0.0) for g in lf8]
    b8 = [_dot3_right(g, upper) for g in lf8]
    m_prev = [ms[bi] for bi in BI]

    qk = [tile(qk_ref, bi, allc) for bi in BI]
    Qp = each(pairs, lambda bi, p: qk[bi][:, psl(p)])
    Kp = each(pairs, lambda bi, p: qk[bi][:, HK + p * LANES:HK + (p + 1) * LANES] * scale)
    Kpb = each(pairs, lambda bi, p: Kp[bi, p].astype(BF16))
    C0p = each(pairs, lambda bi, p: cs[bi, psl(p), :])
    n0p = each(pairs, lambda bi, p: ns[bi][:, psl(p)])
    ig_row = each(heads, lambda bi, h: ig8[bi][h:h + 1, :])
    b_row = each(heads, lambda bi, h: b8[bi][H + h:H + h + 1, :])
    b_col = each(heads, lambda bi, h: to_col(b_row[bi, h]))
    ig_col = each(heads, lambda bi, h: to_col(ig_row[bi, h]))
    D = each(heads, lambda bi, h: jnp.where(causal, b_col[bi, h] - b_row[bi, h] + ig_row[bi, h], neg_inf))
    g_col = each(heads, lambda bi, h: b_col[bi, h] + m_prev[bi][:, h:h + 1])
    m_col = each(heads, lambda bi, h: jnp.maximum(g_col[bi, h], jnp.max(D[bi, h], axis=1, keepdims=True)))
    w_intra = each(heads, lambda bi, h: jnp.exp(D[bi, h] - m_col[bi, h]))
    w_inter = each(heads, lambda bi, h: jnp.exp(g_col[bi, h] - m_col[bi, h]))
    Qh = each(heads, lambda bi, h: jnp.where(inhead[h % per], Qp[bi, h // per], 0.0))
    Qhb = each(heads, lambda bi, h: Qh[bi, h].astype(BF16))
    vb = each(heads, lambda bi, h: tile(v_ref, bi, hsl(h)).astype(BF16))
    s = each(heads, lambda bi, h: _dot_nt(Qhb[bi, h], Kpb[bi, h // per]) * w_intra[bi, h])
    qc = each(heads, lambda bi, h: _dot(Qhb[bi, h], C0p[bi, h // per]))
    num = each(heads, lambda bi, h: _dot(s[bi, h], vb[bi, h]) + w_inter[bi, h] * qc[bi, h])
    den = each(heads, lambda bi, h: (jnp.sum(s[bi, h], axis=1, keepdims=True) + w_inter[bi, h]
                                     * jnp.sum(Qh[bi, h] * n0p[bi, h // per], axis=1, keepdims=True)))
    hh = each(heads, lambda bi, h: num[bi, h] / jnp.maximum(jnp.abs(den[bi, h]), jnp.exp(-m_col[bi, h])))
    ml = each(heads, lambda bi, h: (_headnorm(hh[bi, h], mln_ref[:, hsl(h)])
                                    * _sigmoid(tile(mo_ref, bi, hsl(h)))))
    for h in range(H):
        _store_seqs(ml_ref, hsl(h), [ml[bi, h] for bi in BI], L, valid)
    mL = each(heads, lambda bi, h: m_col[bi, h][L - 1:L, :])
    wL_col = each(heads, lambda bi, h: jnp.exp(b_col[bi, h][L - 1:L, :] - b_col[bi, h] + ig_col[bi, h] - mL[bi, h]))
    wL0 = each(heads, lambda bi, h: jnp.exp(g_col[bi, h][L - 1:L, :] - mL[bi, h]))
    kw = each(heads, lambda bi, h: Kp[bi, h // per] * wL_col[bi, h])
    c_new = each(heads, lambda bi, h: wL0[bi, h] * C0p[bi, h // per] + _dot_tn(kw[bi, h], vb[bi, h]))
    n_new = each(heads, lambda bi, h: wL0[bi, h] * n0p[bi, h // per] + jnp.sum(kw[bi, h], axis=0, keepdims=True))
    for bi, p in pairs:
        cs[bi, psl(p), :] = _select_chain(rowp, [c_new[bi, p * per + u] for u in range(per)], dk)
        ns[bi, :, psl(p)] = _select_chain(lane1, [n_new[bi, p * per + u] for u in range(per)], dk)
    for bi in BI:
        m_new = m_prev[bi]
        for h in range(H):
            m_new = jnp.where(laneh == h, mL[bi, h], m_new)
        ms[bi] = m_new

    gqk = [tile(gqk_ref, bi, allc) for bi in BI]
    Q2 = [g[:, :HK] * scale for g in gqk]
    K2 = [g[:, HK:] for g in gqk]
    if valid < L:
        K2 = [jnp.where(t_col < valid, x, 0.0) for x in K2]
    la = [_log_sigmoid(_dot(sm[bi], wa2_ref[...]) + ba_ref[...]) * (1.0 / GLA_TAU) for bi in BI]
    if valid < L:
        la = [jnp.where(t_col < valid, x, 0.0) for x in la]
    TL = [_dot3(tmat_ref[L:(2 + nlev) * L, :], x) for x in la]
    Q2b = [x.astype(BF16) for x in Q2]
    K2b = [x.astype(BF16) for x in K2]
    scores = each(heads, lambda bi, h: jnp.where(
        eye, _dot_nt(jnp.where(inhead[h % per], Q2b[bi][:, psl(h // per)], 0), K2b[bi][:, psl(h // per)]), 0.0))
    for i in range(nlev):
        n = L >> (i + 1)
        second = (t_col & n) != 0
        En = [jnp.exp(t[(1 + i) * L:(2 + i) * L]) for t in TL]
        X = [(jnp.where(second, Q2[bi], K2[bi]) * En[bi]).astype(BF16) for bi in BI]
        Xk = [jnp.where(second, 0, x) for x in X]
        sn = each(heads, lambda bi, h: _dot_nt(
            jnp.where(second & inhead[h % per], X[bi][:, psl(h // per)], 0), Xk[bi][:, psl(h // per)]))
        if i > 0:
            sh = int(round(math.log2(2 * n)))
            same = (tt >> sh) == (sc >> sh)
            sn = each(heads, lambda bi, h: jnp.where(same, sn[bi, h], 0.0))
        scores = each(heads, lambda bi, h: scores[bi, h] + sn[bi, h])
    A = [t[0:L] for t in TL]
    AL = [a[L - 1:L, :] for a in A]
    QA = [(Q2[bi] * jnp.exp(A[bi])).astype(BF16) for bi in BI]
    kd = [(K2[bi] * jnp.exp(AL[bi] - A[bi])).astype(BF16) for bi in BI]
    eAL = [jnp.exp(a) for a in AL]
    S0p = each(pairs, lambda bi, p: ss[bi, psl(p), :])
    dec_col = each(pairs, lambda bi, p: jnp.sum(jnp.where(ek, eAL[bi][:, psl(p)], 0.0), axis=1, keepdims=True))
    v2b = each(heads, lambda bi, h: tile(gv_ref, bi, hsl(h)).astype(BF16))
    qs = each(heads, lambda bi, h: _dot(jnp.where(inhead[h % per], QA[bi][:, psl(h // per)], 0), S0p[bi, h // per]))
    o = each(heads, lambda bi, h: _dot(scores[bi, h], v2b[bi, h]) + qs[bi, h])
    gla = each(heads, lambda bi, h: (_headnorm(o[bi, h], gln_ref[:, hsl(h)])
                                     * _silu(tile(gg_ref, bi, hsl(h)))))
    for h in range(H):
        _store_seqs(gla_ref, hsl(h), [gla[bi, h] for bi in BI], L, valid)
    upd = each(heads, lambda bi, h: _dot_tn(kd[bi][:, psl(h // per)], v2b[bi, h]))
    for bi, p in pairs:
        ss[bi, psl(p), :] = (dec_col[bi, p] * S0p[bi, p]
                             + _select_chain(rowp, [upd[bi, p * per + u] for u in range(per)], dk))

    @pl.when(c == nc - 1)
    def _():
        c_ref[0] = cs[...].reshape(bt, H, dk, dv)
        n_ref[0] = ns[...]
        m_ref[0] = ms[...]
        s_ref[0] = ss[...].reshape(bt, H, dk, dv)


def _sel_rows(r):
    return jnp.asarray(np.eye(r, LANES, dtype=np.float32), dtype=BF16)


def _mix_dtype(L, valid):
    return BF16 if valid == L else F32


def _row_blocks(B, nc, L, valid, bt):
    if valid == L:
        return (B, nc * L), lambda w, k: pl.BlockSpec((bt, L, w), lambda b, c: (b, c, k))
    assert nc == 1 and L % valid == 0 and (bt * valid) % L == 0
    return (B // bt, bt * valid), lambda w, k: pl.BlockSpec((1, bt * valid, w), lambda b, c: (b, 0, k))


def _ab_scan(P, B, nc, L, valid, bt, j, states, prev, prm):
    C0, n0, m0, S0 = states
    nst, _, H, dk, dv = C0.shape
    HK = H * dk
    lead, blk = _row_blocks(B, nc, L, valid, bt)
    assert P.shape[:2] == lead and dv == LANES and LANES % dk == 0 and B % bt == 0
    tmat, nlev = _ab_mats(L)
    wv = H * dv
    assert 2 * HK == wv
    nsm = 6 * wv // LANES
    sel = _sel_rows(2 * H)
    sec = lambda k: blk(wv, k)
    st_c = pl.BlockSpec((1, bt, H, dk, dv), lambda b, c: (j, b, 0, 0, 0))
    st_n = pl.BlockSpec((1, bt, 1, HK), lambda b, c: (j, b, 0, 0))
    st_m = pl.BlockSpec((1, bt, 1, H), lambda b, c: (j, b, 0, 0))
    full = lambda a: pl.BlockSpec(a.shape, lambda b, c: (0,) * a.ndim)
    n_alias = 0 if prev is None else 4
    consts = [prm["gate_bias"], prm["ml_norm"], prm["wa2"], prm["ba"], prm["gla_norm"], tmat, sel]
    in_specs = ([sec(k) for k in range(6)]
                + [blk(LANES, nsm), st_c, st_n, st_m, st_c]
                + [full(a) for a in consts]
                + [pl.BlockSpec(memory_space=pl.ANY)] * n_alias)
    out_specs = [sec(0), sec(0), st_c, st_n, st_m, st_c]
    out_shape = [
        jax.ShapeDtypeStruct(lead + (wv,), _mix_dtype(L, valid)),
        jax.ShapeDtypeStruct(lead + (wv,), _mix_dtype(L, valid)),
        jax.ShapeDtypeStruct(C0.shape, F32), jax.ShapeDtypeStruct(n0.shape, F32),
        jax.ShapeDtypeStruct(m0.shape, F32), jax.ShapeDtypeStruct(S0.shape, F32),
    ]
    n_in = len(in_specs) - n_alias
    aliases = {n_in + k: 2 + k for k in range(n_alias)}
    args = [P] * 7 + [C0, n0, m0, S0] + consts + (list(prev) if prev is not None else [])
    return pl.pallas_call(
        functools.partial(_ab_kernel, L, valid, nlev, nc, bt, H, dk, dv, n_alias),
        grid=(B // bt, nc),
        in_specs=in_specs,
        out_specs=out_specs,
        out_shape=out_shape,
        input_output_aliases=aliases,
        scratch_shapes=[pltpu.VMEM((bt, HK, dv), F32), pltpu.VMEM((bt, 1, HK), F32),
                        pltpu.VMEM((bt, 1, H), F32), pltpu.VMEM((bt, HK, dv), F32)],
        compiler_params=pltpu.CompilerParams(
            dimension_semantics=("parallel", "arbitrary"), vmem_limit_bytes=VMEM_LIMIT),
        name="ab_scan",
    )(*args)


def _ssd_kernel(L, valid, nc, bt, G, hpg, hd, wave, n_alias,
                z_ref, x_ref, b_ref, c_ref, dt_ref, cv_ref, h0_ref,
                cw_ref, cb_ref, dtbr_ref, dtbc_ref, alr_ref, alc_ref, dsk_ref, nrm_ref,
                tmat_ref, exp_ref, sel_ref, *rest):
    y_ref, h_ref, catx, catb, catc, hs = rest[n_alias:]
    cidx = pl.program_id(1)
    gw = hpg * hd
    nh = G * hpg
    inner = nh * hd
    N = h0_ref.shape[4]
    W = cw_ref.shape[0]
    P0 = SUBLANES - (W - 1)
    per = LANES // hd
    cats = ((catx, 0, inner), (catb, inner, G * N), (catc, inner + G * N, G * N))

    @pl.when(cidx == 0)
    def _():
        for cat, off, wid in cats:
            cat[:, P0:SUBLANES, :] = cv_ref[0, :, :, off:off + wid]
        hs[...] = h0_ref[0].reshape(bt, inner, N)

    t_col = lax.broadcasted_iota(jnp.int32, (L, 1), 0)
    s_row = lax.broadcasted_iota(jnp.int32, (1, L), 1)
    tt = lax.broadcasted_iota(jnp.int32, (L, L), 0)
    sc = lax.broadcasted_iota(jnp.int32, (L, L), 1)
    causal = sc <= tt
    lane = lax.broadcasted_iota(jnp.int32, (L, LANES), 1)
    upper = tmat_ref[0:L, :]
    lower = tmat_ref[L:2 * L, :]

    def conv_silu(cat, bi, off, lo, wid):
        xall = cat[bi, :, lo:lo + wid]
        cols = slice(off + lo, off + lo + wid)
        acc = cb_ref[:, cols] + xall[SUBLANES:SUBLANES + L] * cw_ref[W - 1:W, cols]
        for w in range(W - 1):
            tap = pltpu.roll(xall, W - 1 - w, axis=0)[SUBLANES:SUBLANES + L]
            acc = acc + tap * cw_ref[w:w + 1, cols]
        return _silu(acc)

    allc = slice(None)
    tile = lambda ref, bi, cols: _seq_tile(ref, bi, cols, L, valid)
    for bi in range(bt):
        catx[bi, SUBLANES:SUBLANES + L, :] = tile(x_ref, bi, allc)
        catb[bi, SUBLANES:SUBLANES + L, :] = tile(b_ref, bi, allc)
        catc[bi, SUBLANES:SUBLANES + L, :] = tile(c_ref, bi, allc)

    BI = range(bt)
    each = lambda keys, f: {k: f(*k) for k in keys}
    gsl = lambda g: slice(g * gw, (g + 1) * gw)
    dt_raw = [tile(dt_ref, bi, allc) for bi in BI]
    dtc = [_softplus(x + dtbr_ref[...]) for x in dt_raw]
    dtr = [_softplus(_rows_of_transpose(sel_ref[...], x) + dtbc_ref[...]) for x in dt_raw]
    if valid < L:
        dtc = [jnp.where(t_col < valid, x, 0.0) for x in dtc]
        dtr = [jnp.where(s_row < valid, x, 0.0) for x in dtr]
    cs_col = [_dot3(lower, x * (-jnp.exp(alr_ref[...]))) for x in dtc]
    cs_row = [_dot3_right(x * (-jnp.exp(alc_ref[...])), upper) for x in dtr]
    cd3 = [_split3(jnp.concatenate([cs_col[bi], dtc[bi]], axis=0)) for bi in BI]

    problems = [(bi, g) for bi in BI for g in range(G)]
    for w0 in range(0, len(problems), wave):
        keys = problems[w0:w0 + wave]
        hkeys = [(bi, g, jl) for bi, g in keys for jl in range(hpg)]
        xa = each(keys, lambda bi, g: conv_silu(catx, bi, 0, g * gw, gw))
        Bm = each(keys, lambda bi, g: conv_silu(catb, bi, inner, g * N, N))
        Cm = each(keys, lambda bi, g: conv_silu(catc, bi, inner + G * N, g * N, N))
        ce = each(keys, lambda bi, g: (
            jnp.dot(cd3[bi][0], exp_ref[:, gsl(g)], preferred_element_type=F32)
            + jnp.dot(cd3[bi][1], exp_ref[:, gsl(g)], preferred_element_type=F32)
            + jnp.dot(cd3[bi][2], exp_ref[:, gsl(g)], preferred_element_type=F32)))
        cs_exp = each(keys, lambda bi, g: ce[bi, g][0:L])
        dt_exp = each(keys, lambda bi, g: ce[bi, g][L:2 * L])
        Bmb = each(keys, lambda bi, g: Bm[bi, g].astype(BF16))
        Cmb = each(keys, lambda bi, g: Cm[bi, g].astype(BF16))
        CB = each(keys, lambda bi, g: _dot_nt(Cmb[bi, g], Bmb[bi, g]))
        hs0 = each(keys, lambda bi, g: hs[bi, gsl(g), :])
        yc = each(keys, lambda bi, g: _dot_nt(Cmb[bi, g], hs0[bi, g]) * jnp.exp(cs_exp[bi, g]))
        xab = each(keys, lambda bi, g: xa[bi, g].astype(BF16))
        seg = each(hkeys, lambda bi, g, jl: jnp.where(
            causal, cs_col[bi][:, g * hpg + jl:g * hpg + jl + 1] - cs_row[bi][g * hpg + jl:g * hpg + jl + 1, :],
            -jnp.inf))
        mj = each(hkeys, lambda bi, g, jl: CB[bi, g] * (
            jnp.exp(seg[bi, g, jl]) * dtr[bi][g * hpg + jl:g * hpg + jl + 1, :]))
        yh = each(hkeys, lambda bi, g, jl: _dot(
            mj[bi, g, jl], xab[bi, g][:, (jl // per) * LANES:(jl // per + 1) * LANES]))
        yi = each(keys, lambda bi, g: jnp.concatenate(
            [_select_chain(lane, [yh[bi, g, p * per + u] for u in range(per)], hd)
             for p in range(gw // LANES)], axis=1))
        y = each(keys, lambda bi, g: yc[bi, g] + yi[bi, g] + dsk_ref[:, gsl(g)] * xa[bi, g])
        y = each(keys, lambda bi, g: y[bi, g] * _silu(tile(z_ref, bi, gsl(g))))
        y = each(keys, lambda bi, g: y[bi, g] * lax.rsqrt(
            jnp.mean(y[bi, g] * y[bi, g], axis=1, keepdims=True) + LN_EPS) * nrm_ref[:, gsl(g)])
        if valid == L:
            for bi, g in keys:
                y_ref[bi, :, gsl(g)] = y[bi, g].astype(y_ref.dtype)
        else:
            assert len(keys) == bt * G
            for g in range(G):
                _store_seqs(y_ref, gsl(g), [y[bi, g] for bi in BI], L, valid)

        xw = each(keys, lambda bi, g: xa[bi, g] * (
            jnp.exp(cs_exp[bi, g][L - 1:L, :] - cs_exp[bi, g]) * dt_exp[bi, g]))
        upd = each(keys, lambda bi, g: _dot_tn(xw[bi, g], Bmb[bi, g]))
        dec = each(hkeys, lambda bi, g, jl: jnp.exp(cs_row[bi][g * hpg + jl:g * hpg + jl + 1, L - 1:L]))
        for bi, g, jl in hkeys:
            r0 = g * gw + jl * hd
            hs[bi, r0:r0 + hd, :] = (dec[bi, g, jl] * hs0[bi, g][jl * hd:(jl + 1) * hd, :]
                                     + upd[bi, g][jl * hd:(jl + 1) * hd, :])

    if nc > 1:
        for bi in BI:
            for cat, _, _ in cats:
                tail = cat[bi, P0 + L:SUBLANES + L, :]
                cat[bi, P0:SUBLANES, :] = tail

    @pl.when(cidx == nc - 1)
    def _():
        h_ref[0] = hs[...].reshape(h_ref.shape[1:])


def _ssd_scan(P, B, nc, L, valid, bt, j, h0, conv0, prev, prm):
    nst, _, nh, hd, N = h0.shape
    G = prm["groups"]
    hpg = nh // G
    gw = hpg * hd
    inner = nh * hd
    GN = G * N
    W1, cd = conv0.shape[2], conv0.shape[3]
    lead, blk = _row_blocks(B, nc, L, valid, bt)
    assert P.shape[:2] == lead and N == LANES and gw % LANES == 0 and B % bt == 0
    assert nh <= LANES and inner % GN == 0 and cd == inner + 2 * GN
    tmat = _ssd_mats(L)
    sel = _sel_rows(nh)
    wave = bt * G if L <= 2 * SUBLANES else 1
    o_dt = 2 * inner + 2 * GN
    assert o_dt % LANES == 0
    full = lambda a: pl.BlockSpec(a.shape, lambda b, c: (0,) * a.ndim)
    st_h = pl.BlockSpec((1, bt, nh, hd, N), lambda b, c: (j, b, 0, 0, 0))
    n_alias = 0 if prev is None else 1
    consts = [prm["conv_w"], prm["conv_b"], prm["dtb_row"], prm["dtb_col"], prm["alog_row"],
              prm["alog_col"], prm["dskip"], prm["norm"], tmat, prm["expand"], sel]
    in_specs = [
        blk(inner, 0), blk(inner, 1), blk(GN, 2 * inner // GN), blk(GN, 2 * inner // GN + 1),
        blk(LANES, o_dt // LANES),
        pl.BlockSpec((1, bt, W1, cd), lambda b, c: (j, b, 0, 0)),
        st_h,
    ] + [full(a) for a in consts] + [pl.BlockSpec(memory_space=pl.ANY)] * n_alias
    out_specs = [blk(inner, 0), st_h]
    out_shape = [jax.ShapeDtypeStruct(lead + (inner,), _mix_dtype(L, valid)),
                 jax.ShapeDtypeStruct(h0.shape, F32)]
    n_in = len(in_specs) - n_alias
    args = [P] * 5 + [conv0, h0] + consts + ([prev] if prev is not None else [])
    return pl.pallas_call(
        functools.partial(_ssd_kernel, L, valid, nc, bt, G, hpg, hd, wave, n_alias),
        grid=(B // bt, nc),
        in_specs=in_specs,
        out_specs=out_specs,
        out_shape=out_shape,
        input_output_aliases={n_in: 1} if n_alias else {},
        scratch_shapes=[pltpu.VMEM((bt, SUBLANES + L, inner), F32), pltpu.VMEM((bt, SUBLANES + L, GN), F32),
                        pltpu.VMEM((bt, SUBLANES + L, GN), F32), pltpu.VMEM((bt, inner, N), F32)],
        compiler_params=pltpu.CompilerParams(
            dimension_semantics=("parallel", "arbitrary"), vmem_limit_bytes=VMEM_LIMIT),
        name="ssd_scan",
    )(*args)


def _prep_ab(j, H, dk, dv, w_in, ig_bias, fg_bias, ml_norm, wa2, ba, gla_norm, w_out):
    rank = wa2.shape[1]
    qk, vv = H * dk, H * dv
    o = np.cumsum([0, qk, qk, vv, vv, H, H, qk, qk, vv, vv, rank])
    assert int(o[-1]) == w_in.shape[2]
    wj = w_in[j]
    small = jnp.concatenate([wj[:, o[4]:o[6]], wj[:, o[10]:o[11]]], axis=1)
    small = jnp.pad(small, ((0, 0), (0, LANES - small.shape[1])))
    w = jnp.concatenate([wj[:, :o[4]], wj[:, o[6]:o[10]], small], axis=1).astype(BF16)
    wa2p = jnp.zeros((LANES, qk), F32).at[2 * H:2 * H + rank, :].set(wa2[j]).astype(BF16)
    return {
        "w_in": w,
        "gate_bias": jnp.concatenate([ig_bias[j], fg_bias[j]]).astype(F32).reshape(2 * H, 1),
        "ml_norm": ml_norm[j].reshape(1, vv),
        "gla_norm": gla_norm[j].reshape(1, vv),
        "wa2": wa2p,
        "ba": ba[j].reshape(1, qk),
        "w_out_ml": w_out[j][:vv].astype(BF16),
        "w_out_gla": w_out[j][vv:].astype(BF16),
    }


def _prep_ssd(j, G, nh, hd, N, w_in, conv_w, conv_b, dt_bias, a_log, d_skip, norm_g, w_out):
    inner = nh * hd
    hpg = nh // G
    cd = inner + 2 * G * N
    wz, wxbc, wdt = w_in[j][:, :inner], w_in[j][:, inner:inner + cd], w_in[j][:, inner + cd:]
    wdt = jnp.pad(wdt, ((0, 0), (0, 2 * LANES - nh)))
    w = jnp.concatenate([wz, wxbc, wdt], axis=1).astype(BF16)
    lane_form = lambda v: jnp.pad(v, (0, LANES - nh)).reshape(1, LANES)
    e = np.zeros((LANES, inner), np.float32)
    for r in range(nh):
        e[r, r * hd:(r + 1) * hd] = 1.0
    return {
        "groups": G,
        "w_in": w,
        "conv_w": conv_w[j], "conv_b": conv_b[j].reshape(1, cd),
        "dtb_row": lane_form(dt_bias[j]), "dtb_col": dt_bias[j].reshape(nh, 1),
        "alog_row": lane_form(a_log[j]), "alog_col": a_log[j].reshape(nh, 1),
        "dskip": jnp.repeat(d_skip[j], hd).reshape(1, inner),
        "norm": norm_g[j].reshape(1, inner),
        "expand": jnp.asarray(e, dtype=BF16),
        "w_out": w_out[j].astype(BF16),
    }


def _largest_divisor(n, cap, step=1):
    return max(d for d in range(step, cap + 1, step) if n % d == 0)


def _plan(B, T, chunk):
    if T % chunk == 0:
        return chunk, T // chunk, _largest_divisor(B, 4), 1
    L = max(SUBLANES, 1 << int(math.ceil(math.log2(T))))
    per = L // T
    assert L % T == 0 and B % per == 0
    return L, 1, _largest_divisor(B, 8, per), _largest_divisor(B, 4, per)


def _trunk(x, states, ab_prm, ssd_prm, mlp, lns, alpha, chunk):
    B, T, D = x.shape
    mC, mn, mm, gS, sh, sconv = states
    n_ab, _, H, dk, dv = mC.shape
    L, nc, bt_ab, bt_ssd = _plan(B, T, chunk)
    valid = T if nc == 1 else L
    M = B * T
    tm = 512 if M % 512 == 0 else M
    tmm = 1024 if M % 1024 == 0 else tm
    X = x.reshape(M, D)
    ab_states = (mC, mn.reshape(n_ab, B, 1, H * dk), mm.reshape(n_ab, B, 1, H), gS)
    ab_out, h_out, ncv = None, None, []
    depth = mlp[0].shape[0]
    for l in range(depth):
        j = l // 2
        if l % 2 == 0:
            p = ab_prm[j]
            P = _proj(X, p["w_in"], tm, p["w_in"].shape[1])
            lead = _row_blocks(B, nc, L, valid, bt_ab)[0]
            res = _ab_scan(P.reshape(lead + (-1,)), B, nc, L, valid, bt_ab, j, ab_states, ab_out, p)
            ab_out = res[2:]
            ys = [res[0].reshape(M, -1), res[1].reshape(M, -1)]
            ws = [p["w_out_ml"], p["w_out_gla"]]
        else:
            p = ssd_prm[j]
            P = _proj(X, p["w_in"], tm, p["w_in"].shape[1] // 2)
            inner = p["norm"].shape[1]
            cd = p["conv_b"].shape[1]
            W1 = sconv.shape[2]
            assert T >= W1
            ncv.append(P.reshape(B, T, -1)[:, T - W1:, inner:inner + cd])
            lead = _row_blocks(B, nc, L, valid, bt_ssd)[0]
            y, h_out = _ssd_scan(P.reshape(lead + (-1,)), B, nc, L, valid, bt_ssd, j, sh, sconv, h_out, p)
            ys = [y.reshape(M, -1)]
            ws = [p["w_out"]]
        X = _outproj_ln(ys, ws, X, lns[0], lns[1], l, alpha, tm)
        X = _mlp_ln(X, mlp[0], mlp[1], lns[2], lns[3], l, alpha, tmm, 1024)
    nC, nn_, nm, nS = ab_out
    return (X.reshape(B, T, D), nC, nn_.reshape(mn.shape), nm.reshape(mm.shape), nS, h_out, jnp.stack(ncv))


def kernel(x_prompt, x_sample, state_mlstm_C, state_mlstm_n, state_mlstm_m, state_gla_S, state_ssd_h,
           state_ssd_conv, ab_w_in, ab_ig_bias, ab_fg_bias, ab_ml_norm, ab_gla_wa2, ab_gla_ba, ab_gla_norm,
           ab_w_out, ssd_w_in, ssd_conv_w, ssd_conv_b, ssd_dt_bias, ssd_a_log, ssd_d, ssd_norm, ssd_w_out,
           mlp_w1, mlp_w2, ln_mix_g, ln_mix_b, ln_mlp_g, ln_mlp_b):
    depth = mlp_w1.shape[0]
    D = x_prompt.shape[2]
    alpha = (2 * depth) ** 0.25
    n_ab, _, H, dk, dv = state_mlstm_C.shape
    n_ssd, _, nh, hd, N = state_ssd_h.shape
    cd = state_ssd_conv.shape[3]
    G = (cd - nh * hd) // (2 * N)
    ab_prm = [_prep_ab(j, H, dk, dv, ab_w_in, ab_ig_bias, ab_fg_bias, ab_ml_norm, ab_gla_wa2,
                       ab_gla_ba, ab_gla_norm, ab_w_out) for j in range(n_ab)]
    ssd_prm = [_prep_ssd(j, G, nh, hd, N, ssd_w_in, ssd_conv_w, ssd_conv_b, ssd_dt_bias, ssd_a_log,
                         ssd_d, ssd_norm, ssd_w_out) for j in range(n_ssd)]
    mlp = (mlp_w1.astype(BF16), mlp_w2.astype(BF16))
    lns = tuple(a.reshape(depth, 1, D) for a in (ln_mix_g, ln_mix_b, ln_mlp_g, ln_mlp_b))

    Bp = x_prompt.shape[0]
    zeros = (jnp.zeros((n_ab, Bp, H, dk, dv), F32), jnp.zeros((n_ab, Bp, H, dk), F32),
             jnp.zeros((n_ab, Bp, H), F32), jnp.zeros((n_ab, Bp, H, dk, dv), F32),
             jnp.zeros((n_ssd, Bp, nh, hd, N), F32), jnp.zeros((n_ssd, Bp) + state_ssd_conv.shape[2:], F32))
    carried = (state_mlstm_C, state_mlstm_n, state_mlstm_m, state_gla_S, state_ssd_h, state_ssd_conv)
    chunk = 128
    yp = _trunk(x_prompt, zeros, ab_prm, ssd_prm, mlp, lns, alpha, chunk)
    ys = _trunk(x_sample, carried, ab_prm, ssd_prm, mlp, lns, alpha, chunk)
    return (yp[0], ys[0]) + yp[1:] + ys[1:]
```

```python
import functools
import math

import jax
import jax.numpy as jnp
import numpy as np
from jax import lax
from jax.experimental import pallas as pl
from jax.experimental.pallas import tpu as pltpu

F32 = jnp.float32
BF16 = jnp.bfloat16

LN_EPS = 1e-5
GLA_TAU = 16.0
LANES = 128
SUBLANES = 8
VMEM_LIMIT = 48 * 1024 * 1024


def _dot(a, b):
    return jnp.dot(a.astype(BF16), b.astype(BF16), preferred_element_type=F32)


def _dot_nt(a, b):
    return lax.dot_general(a.astype(BF16), b.astype(BF16), (((1,), (1,)), ((), ())),
                           preferred_element_type=F32)


def _dot_tn(a, b):
    return lax.dot_general(a.astype(BF16), b.astype(BF16), (((0,), (0,)), ((), ())),
                           preferred_element_type=F32)


def _split3(x):
    hi = x.astype(BF16)
    r = x - hi.astype(F32)
    mid = r.astype(BF16)
    lo = (r - mid.astype(F32)).astype(BF16)
    return hi, mid, lo


def _dot3(t, x):
    hi, mid, lo = _split3(x)
    f = lambda p: jnp.dot(t, p, preferred_element_type=F32)
    return f(hi) + f(mid) + f(lo)


def _dot3_right(x, t):
    hi, mid, lo = _split3(x)
    f = lambda p: jnp.dot(p, t, preferred_element_type=F32)
    return f(hi) + f(mid) + f(lo)


def _rows_of_transpose(sel, x):
    hi, mid, lo = _split3(x)
    f = lambda p: lax.dot_general(sel, p, (((1,), (1,)), ((), ())), preferred_element_type=F32)
    return f(hi) + f(mid) + f(lo)


def _softplus(x):
    e = jnp.exp(-jnp.abs(x))
    u = 1.0 + e
    d = u - 1.0
    return jnp.maximum(x, 0.0) + jnp.where(d == 0.0, e, jnp.log(u) * (e / d))


def _log_sigmoid(x):
    return -_softplus(-x)


def _sigmoid(x):
    return 0.5 + 0.5 * jnp.tanh(0.5 * x)


def _silu(x):
    h = 0.5 * x
    return h + h * jnp.tanh(h)


def _seq_tile(ref, bi, cols, L, T):
    if T == L:
        return ref[bi, :, cols]
    r0 = bi * T
    a = (r0 // L) * L
    tile = ref[0, a:a + L, cols]
    return tile if r0 == a else pltpu.roll(tile, L - (r0 - a), axis=0)


def _store_seqs(ref, cols, vals, L, T):
    if T == L:
        for bi, v in enumerate(vals):
            ref[bi, :, cols] = v.astype(ref.dtype)
        return
    per = L // T
    row = lax.broadcasted_iota(jnp.int32, vals[0].shape, 0)
    for a in range(len(vals) // per):
        out = vals[a * per]
        for k in range(1, per):
            out = jnp.where(row < k * T, out, pltpu.roll(vals[a * per + k], k * T, axis=0))
        ref[0, a * L:(a + 1) * L, cols] = out


def _layernorm_rows(r, g, b):
    mu = jnp.mean(r, axis=1, keepdims=True)
    d = r - mu
    var = jnp.mean(d * d, axis=1, keepdims=True)
    return d * lax.rsqrt(var + LN_EPS) * g + b


def _headnorm(h, g):
    mu = jnp.mean(h, axis=1, keepdims=True)
    d = h - mu
    var = jnp.mean(d * d, axis=1, keepdims=True)
    return d * lax.rsqrt(var + LN_EPS) * g


def _select_chain(idx, pieces, width):
    out = pieces[0]
    for u in range(1, len(pieces)):
        out = jnp.where(idx < u * width, out, pieces[u])
    return out


def _cumsum_mats(L):
    t = np.arange(L)[:, None]
    j = np.arange(L)[None, :]
    upper = (t <= j)
    lower = (j <= t)
    return upper, lower


def _ab_mats(L):
    nlev = int(round(math.log2(L)))
    assert 1 << nlev == L
    upper, lower = _cumsum_mats(L)
    t = np.arange(L)[:, None]
    j = np.arange(L)[None, :]
    mats = [upper, lower]
    for i in range(nlev):
        n = L >> (i + 1)
        mid = (t // (2 * n)) * (2 * n) + n - 1
        second = (t % (2 * n)) >= n
        m = np.where(second, (j > mid) & (j <= t), (j > t) & (j <= mid))
        mats.append(m)
    return jnp.asarray(np.concatenate(mats, axis=0).astype(np.float32), dtype=BF16), nlev


def _ssd_mats(L):
    upper, lower = _cumsum_mats(L)
    return jnp.asarray(np.concatenate([upper, lower], axis=0).astype(np.float32), dtype=BF16)


def _proj_kernel(x_ref, w_ref, o_ref):
    o_ref[...] = jnp.dot(x_ref[...].astype(BF16), w_ref[...], preferred_element_type=F32)


def _proj(x, w, tm, tn):
    M, K = x.shape
    N = w.shape[1]
    assert M % tm == 0 and N % tn == 0
    return pl.pallas_call(
        _proj_kernel,
        grid=(N // tn, M // tm),
        in_specs=[pl.BlockSpec((tm, K), lambda j, i: (i, 0)),
                  pl.BlockSpec((K, tn), lambda j, i: (0, j))],
        out_specs=pl.BlockSpec((tm, tn), lambda j, i: (i, j)),
        out_shape=jax.ShapeDtypeStruct((M, N), F32),
        compiler_params=pltpu.CompilerParams(
            dimension_semantics=("parallel", "parallel"), vmem_limit_bytes=VMEM_LIMIT),
        name="proj",
    )(x, w)


class _SsdLayout:
    def __init__(self, inner, gn):
        self.cd = inner + 2 * gn
        self.half = self.cd // 2
        self.zhalf = inner // 2
        self.tile = self.half + self.zhalf + LANES
        self.dt = self.half + self.zhalf
        assert self.cd % (2 * LANES) == 0 and inner % (2 * LANES) == 0

    def xbc(self, c):
        return c if c < self.half else self.tile + c - self.half

    def z(self, c):
        return self.half + c if c < self.zhalf else self.tile + self.half + c - self.zhalf


def _proj_conv_kernel(tps, nrb, half, zhalf, x_ref, w_ref, cw_ref, cb_ref, cv_ref, o_ref, tail_ref, hist):
    i = pl.program_id(1)
    W = cw_ref.shape[0]
    P0 = SUBLANES - (W - 1)
    rb = x_ref.shape[0] // nrb

    @pl.when(i % tps == 0)
    def _():
        hist[...] = jnp.zeros_like(hist)
        hist[P0:SUBLANES, :] = cv_ref[0, 0]

    def raw_rows(r):
        return jnp.dot(x_ref[r * rb:(r + 1) * rb, :].astype(BF16), w_ref[...], preferred_element_type=F32)

    def finish(r, raw, prev):
        rows = slice(r * rb, (r + 1) * rb)
        xall = jnp.concatenate([prev, raw[:, :half]], axis=0)
        acc = cb_ref[...] + xall[SUBLANES:] * cw_ref[W - 1:W, :]
        for w in range(W - 1):
            acc = acc + pltpu.roll(xall, W - 1 - w, axis=0)[SUBLANES:] * cw_ref[w:w + 1, :]
        o_ref[rows, :half] = _silu(acc)
        o_ref[rows, half:half + zhalf] = _silu(raw[:, half:half + zhalf])
        o_ref[rows, half + zhalf:] = raw[:, half + zhalf:]
        return raw[rb - SUBLANES:, :half]

    prev = hist[...]
    raw = raw_rows(0)
    for r in range(nrb):
        nxt = raw_rows(r + 1) if r + 1 < nrb else None
        prev = finish(r, raw, prev)
        raw = nxt
    hist[...] = prev
    tail_ref[0] = prev


def _proj_conv(x, w, conv_w, conv_b, conv0, j, B, tm, lay):
    M, K = x.shape
    T = M // B
    assert M % tm == 0 and T % tm == 0 and w.shape[1] == 2 * lay.tile and (tm // 4) % SUBLANES == 0
    tps = T // tm
    W1 = conv0.shape[2]
    return pl.pallas_call(
        functools.partial(_proj_conv_kernel, tps, 4, lay.half, lay.zhalf),
        grid=(2, M // tm),
        in_specs=[pl.BlockSpec((tm, K), lambda c, i: (i, 0)),
                  pl.BlockSpec((K, lay.tile), lambda c, i: (0, c)),
                  pl.BlockSpec((W1 + 1, lay.half), lambda c, i: (0, c)),
                  pl.BlockSpec((1, lay.half), lambda c, i: (0, c)),
                  pl.BlockSpec((1, 1, W1, lay.half), lambda c, i: (j, i // tps, 0, c))],
        out_specs=[pl.BlockSpec((tm, lay.tile), lambda c, i: (i, c)),
                   pl.BlockSpec((1, SUBLANES, lay.half), lambda c, i: (i // tps, 0, c))],
        out_shape=[jax.ShapeDtypeStruct((M, 2 * lay.tile), F32),
                   jax.ShapeDtypeStruct((B, SUBLANES, lay.cd), F32)],
        scratch_shapes=[pltpu.VMEM((SUBLANES, lay.half), F32)],
        compiler_params=pltpu.CompilerParams(
            dimension_semantics=("parallel", "arbitrary"), vmem_limit_bytes=VMEM_LIMIT),
        name="proj_conv",
    )(x, w, conv_w, conv_b, conv0)


def _outproj_ln_kernel(alpha, n_in, *refs):
    ys = refs[:n_in]
    ws = refs[n_in:2 * n_in]
    x_ref, g_ref, b_ref, o_ref = refs[2 * n_in:]
    r = alpha * x_ref[...]
    for y_ref, w_ref in zip(ys, ws):
        r = r + jnp.dot(y_ref[...].astype(BF16), w_ref[...], preferred_element_type=F32)
    o_ref[...] = _layernorm_rows(r, g_ref[0], b_ref[0])


def _outproj_ln(ys, ws, x, g, b, l, alpha, tm):
    M, D = x.shape
    assert M % tm == 0
    n_in = len(ys)
    in_specs = ([pl.BlockSpec((tm, y.shape[1]), lambda i: (i, 0)) for y in ys]
                + [pl.BlockSpec(w.shape, lambda i: (0, 0)) for w in ws]
                + [pl.BlockSpec((tm, D), lambda i: (i, 0)),
                   pl.BlockSpec((1, 1, D), lambda i: (l, 0, 0)),
                   pl.BlockSpec((1, 1, D), lambda i: (l, 0, 0))])
    return pl.pallas_call(
        functools.partial(_outproj_ln_kernel, alpha, n_in),
        grid=(M // tm,),
        in_specs=in_specs,
        out_specs=pl.BlockSpec((tm, D), lambda i: (i, 0)),
        out_shape=jax.ShapeDtypeStruct((M, D), F32),
        compiler_params=pltpu.CompilerParams(
            dimension_semantics=("parallel",), vmem_limit_bytes=VMEM_LIMIT),
        name="outproj_ln",
    )(*ys, *ws, x, g, b)


def _mlp_kernel(alpha, nf, x_ref, w1_ref, w2_ref, g_ref, b_ref, o_ref, acc_ref):
    f = pl.program_id(1)

    @pl.when(f == 0)
    def _():
        acc_ref[...] = jnp.zeros_like(acc_ref)

    h = jnp.dot(x_ref[...].astype(BF16), w1_ref[0], preferred_element_type=F32)
    h = jnp.square(jnp.maximum(h, 0.0))
    acc_ref[...] += jnp.dot(h.astype(BF16), w2_ref[0], preferred_element_type=F32)

    @pl.when(f == nf - 1)
    def _():
        r = alpha * x_ref[...] + acc_ref[...]
        o_ref[...] = _layernorm_rows(r, g_ref[0], b_ref[0])


def _mlp_ln(x, w1, w2, g, b, l, alpha, tm, tf):
    M, D = x.shape
    Fdim = w1.shape[2]
    assert M % tm == 0 and Fdim % tf == 0
    nf = Fdim // tf
    return pl.pallas_call(
        functools.partial(_mlp_kernel, alpha, nf),
        grid=(M // tm, nf),
        in_specs=[pl.BlockSpec((tm, D), lambda i, f: (i, 0)),
                  pl.BlockSpec((1, D, tf), lambda i, f: (l, 0, f)),
                  pl.BlockSpec((1, tf, D), lambda i, f: (l, f, 0)),
                  pl.BlockSpec((1, 1, D), lambda i, f: (l, 0, 0)),
                  pl.BlockSpec((1, 1, D), lambda i, f: (l, 0, 0))],
        out_specs=pl.BlockSpec((tm, D), lambda i, f: (i, 0)),
        out_shape=jax.ShapeDtypeStruct((M, D), F32),
        scratch_shapes=[pltpu.VMEM((tm, D), F32)],
        compiler_params=pltpu.CompilerParams(
            dimension_semantics=("parallel", "arbitrary"), vmem_limit_bytes=VMEM_LIMIT),
        name="mlp_ln",
    )(x, w1, w2, g, b)


def _ab_kernel(L, valid, nlev, nc, bt, H, dk, dv, n_alias,
               qk_ref, v_ref, mo_ref, gqk_ref, gv_ref, gg_ref, sm_ref,
               c0_ref, n0_ref, m0_ref, s0_ref, gb_ref, mln_ref, wa2_ref, ba_ref, gln_ref, tmat_ref,
               sel_ref, *rest):
    ml_ref, gla_ref, c_ref, n_ref, m_ref, s_ref, cs, ns, ms, ss = rest[n_alias:]
    c = pl.program_id(1)
    per = LANES // dk
    HK = H * dk

    @pl.when(c == 0)
    def _():
        cs[...] = c0_ref[0].reshape(bt, HK, dv)
        ns[...] = n0_ref[0]
        ms[...] = m0_ref[0]
        ss[...] = s0_ref[0].reshape(bt, HK, dv)

    t_col = lax.broadcasted_iota(jnp.int32, (L, 1), 0)
    s_row = lax.broadcasted_iota(jnp.int32, (1, L), 1)
    tt = lax.broadcasted_iota(jnp.int32, (L, L), 0)
    sc = lax.broadcasted_iota(jnp.int32, (L, L), 1)
    causal = sc <= tt
    eye = sc == tt
    lane = lax.broadcasted_iota(jnp.int32, (L, LANES), 1)
    lane1 = lax.broadcasted_iota(jnp.int32, (1, LANES), 1)
    laneh = lax.broadcasted_iota(jnp.int32, (1, H), 1)
    rowp = lax.broadcasted_iota(jnp.int32, (LANES, dv), 0)
    ek = lax.broadcasted_iota(jnp.int32, (LANES, LANES), 0) == lax.broadcasted_iota(jnp.int32, (LANES, LANES), 1)
    inhead = [(lane >= u * dk) & (lane < (u + 1) * dk) for u in range(per)]
    upper = tmat_ref[0:L, :]
    scale = dk ** -0.5

    def to_col(row):
        return jnp.sum(jnp.where(eye, row, 0.0), axis=1, keepdims=True)

    BI = range(bt)
    pairs = [(bi, p) for bi in BI for p in range(H // per)]
    heads = [(bi, h) for bi in BI for h in range(H)]
    psl = lambda p: slice(p * LANES, (p + 1) * LANES)
    hsl = lambda h: slice(h * dv, (h + 1) * dv)
    each = lambda keys, f: {k: f(*k) for k in keys}
    neg_inf = -jnp.inf

    allc = slice(None)
    tile = lambda ref, bi, cols: _seq_tile(ref, bi, cols, L, valid)
    sm = [tile(sm_ref, bi, allc) for bi in BI]
    ig8 = [_rows_of_transpose(sel_ref[...], sm[bi]) + gb_ref[...] for bi in BI]
    lf8 = [_log_sigmoid(g) for g in ig8]
    if valid < L:
        ok = s_row < valid
        ig8 = [jnp.where(ok, g, neg_inf) for g in ig8]
        lf8 = [jnp.where(ok, g, 0.0) for g in lf8]
    b8 = [_dot3_right(g, upper) for g in lf8]
    m_prev = [ms[bi] for bi in BI]

    qk = [tile(qk_ref, bi, allc) for bi in BI]
    Qp = each(pairs, lambda bi, p: qk[bi][:, psl(p)])
    Kp = each(pairs, lambda bi, p: qk[bi][:, HK + p * LANES:HK + (p + 1) * LANES] * scale)
    Kpb = each(pairs, lambda bi, p: Kp[bi, p].astype(BF16))
    C0p = each(pairs, lambda bi, p: cs[bi, psl(p), :])
    n0p = each(pairs, lambda bi, p: ns[bi][:, psl(p)])
    ig_row = each(heads, lambda bi, h: ig8[bi][h:h + 1, :])
    b_row = each(heads, lambda bi, h: b8[bi][H + h:H + h + 1, :])
    b_col = each(heads, lambda bi, h: to_col(b_row[bi, h]))
    ig_col = each(heads, lambda bi, h: to_col(ig_row[bi, h]))
    D = each(heads, lambda bi, h: jnp.where(causal, b_col[bi, h] - b_row[bi, h] + ig_row[bi, h], neg_inf))
    g_col = each(heads, lambda bi, h: b_col[bi, h] + m_prev[bi][:, h:h + 1])
    m_col = each(heads, lambda bi, h: jnp.maximum(g_col[bi, h], jnp.max(D[bi, h], axis=1, keepdims=True)))
    w_intra = each(heads, lambda bi, h: jnp.exp(D[bi, h] - m_col[bi, h]))
    w_inter = each(heads, lambda bi, h: jnp.exp(g_col[bi, h] - m_col[bi, h]))
    Qh = each(heads, lambda bi, h: jnp.where(inhead[h % per], Qp[bi, h // per], 0.0))
    Qhb = each(heads, lambda bi, h: Qh[bi, h].astype(BF16))
    vb = each(heads, lambda bi, h: tile(v_ref, bi, hsl(h)).astype(BF16))
    s = each(heads, lambda bi, h: _dot_nt(Qhb[bi, h], Kpb[bi, h // per]) * w_intra[bi, h])
    qc = each(heads, lambda bi, h: _dot(Qhb[bi, h], C0p[bi, h // per]))
    num = each(heads, lambda bi, h: _dot(s[bi, h], vb[bi, h]) + w_inter[bi, h] * qc[bi, h])
    den = each(heads, lambda bi, h: (jnp.sum(s[bi, h], axis=1, keepdims=True) + w_inter[bi, h]
                                     * jnp.sum(Qh[bi, h] * n0p[bi, h // per], axis=1, keepdims=True)))
    hh = each(heads, lambda bi, h: num[bi, h] / jnp.maximum(jnp.abs(den[bi, h]), jnp.exp(-m_col[bi, h])))
    ml = each(heads, lambda bi, h: (_headnorm(hh[bi, h], mln_ref[:, hsl(h)])
                                    * _sigmoid(tile(mo_ref, bi, hsl(h)))))
    for h in range(H):
        _store_seqs(ml_ref, hsl(h), [ml[bi, h] for bi in BI], L, valid)
    mL = each(heads, lambda bi, h: m_col[bi, h][L - 1:L, :])
    wL_col = each(heads, lambda bi, h: jnp.exp(b_col[bi, h][L - 1:L, :] - b_col[bi, h] + ig_col[bi, h] - mL[bi, h]))
    wL0 = each(heads, lambda bi, h: jnp.exp(g_col[bi, h][L - 1:L, :] - mL[bi, h]))
    kw = each(heads, lambda bi, h: Kp[bi, h // per] * wL_col[bi, h])
    c_new = each(heads, lambda bi, h: wL0[bi, h] * C0p[bi, h // per] + _dot_tn(kw[bi, h], vb[bi, h]))
    n_new = each(heads, lambda bi, h: wL0[bi, h] * n0p[bi, h // per] + jnp.sum(kw[bi, h], axis=0, keepdims=True))
    for bi, p in pairs:
        cs[bi, psl(p), :] = _select_chain(rowp, [c_new[bi, p * per + u] for u in range(per)], dk)
        ns[bi, :, psl(p)] = _select_chain(lane1, [n_new[bi, p * per + u] for u in range(per)], dk)
    for bi in BI:
        m_new = m_prev[bi]
        for h in range(H):
            m_new = jnp.where(laneh == h, mL[bi, h], m_new)
        ms[bi] = m_new

    gqk = [tile(gqk_ref, bi, allc) for bi in BI]
    Q2 = [g[:, :HK] * scale for g in gqk]
    K2 = [g[:, HK:] for g in gqk]
    if valid < L:
        K2 = [jnp.where(t_col < valid, x, 0.0) for x in K2]
    la = [_log_sigmoid(_dot(sm[bi], wa2_ref[...]) + ba_ref[...]) * (1.0 / GLA_TAU) for bi in BI]
    if valid < L:
        la = [jnp.where(t_col < valid, x, 0.0) for x in la]
    TL = [_dot3(tmat_ref[L:(2 + nlev) * L, :], x) for x in la]
    Q2b = [x.astype(BF16) for x in Q2]
    K2b = [x.astype(BF16) for x in K2]
    scores = each(heads, lambda bi, h: jnp.where(
        eye, _dot_nt(jnp.where(inhead[h % per], Q2b[bi][:, psl(h // per)], 0), K2b[bi][:, psl(h // per)]), 0.0))
    for i in range(nlev):
        n = L >> (i + 1)
        second = (t_col & n) != 0
        En = [jnp.exp(t[(1 + i) * L:(2 + i) * L]) for t in TL]
        X = [(jnp.where(second, Q2[bi], K2[bi]) * En[bi]).astype(BF16) for bi in BI]
        Xk = [jnp.where(second, 0, x) for x in X]
        sn = each(heads, lambda bi, h: _dot_nt(
            jnp.where(second & inhead[h % per], X[bi][:, psl(h // per)], 0), Xk[bi][:, psl(h // per)]))
        if i > 0:
            sh = int(round(math.log2(2 * n)))
            same = (tt >> sh) == (sc >> sh)
            sn = each(heads, lambda bi, h: jnp.where(same, sn[bi, h], 0.0))
        scores = each(heads, lambda bi, h: scores[bi, h] + sn[bi, h])
    A = [t[0:L] for t in TL]
    AL = [a[L - 1:L, :] for a in A]
    QA = [(Q2[bi] * jnp.exp(A[bi])).astype(BF16) for bi in BI]
    kd = [(K2[bi] * jnp.exp(AL[bi] - A[bi])).astype(BF16) for bi in BI]
    eAL = [jnp.exp(a) for a in AL]
    S0p = each(pairs, lambda bi, p: ss[bi, psl(p), :])
    dec_col = each(pairs, lambda bi, p: jnp.sum(jnp.where(ek, eAL[bi][:, psl(p)], 0.0), axis=1, keepdims=True))
    v2b = each(heads, lambda bi, h: tile(gv_ref, bi, hsl(h)).astype(BF16))
    qs = each(heads, lambda bi, h: _dot(jnp.where(inhead[h % per], QA[bi][:, psl(h // per)], 0), S0p[bi, h // per]))
    o = each(heads, lambda bi, h: _dot(scores[bi, h], v2b[bi, h]) + qs[bi, h])
    gla = each(heads, lambda bi, h: (_headnorm(o[bi, h], gln_ref[:, hsl(h)])
                                     * _silu(tile(gg_ref, bi, hsl(h)))))
    for h in range(H):
        _store_seqs(gla_ref, hsl(h), [gla[bi, h] for bi in BI], L, valid)
    upd = each(heads, lambda bi, h: _dot_tn(kd[bi][:, psl(h // per)], v2b[bi, h]))
    for bi, p in pairs:
        ss[bi, psl(p), :] = (dec_col[bi, p] * S0p[bi, p]
                             + _select_chain(rowp, [upd[bi, p * per + u] for u in range(per)], dk))

    @pl.when(c == nc - 1)
    def _():
        c_ref[0] = cs[...].reshape(bt, H, dk, dv)
        n_ref[0] = ns[...]
        m_ref[0] = ms[...]
        s_ref[0] = ss[...].reshape(bt, H, dk, dv)


def _sel_rows(r):
    return jnp.asarray(np.eye(r, LANES, dtype=np.float32), dtype=BF16)


def _mix_dtype(L, valid):
    return BF16 if valid == L else F32


def _row_blocks(B, nc, L, valid, bt):
    if valid == L:
        return (B, nc * L), lambda w, k: pl.BlockSpec((bt, L, w), lambda b, c: (b, c, k))
    assert nc == 1 and L % valid == 0 and (bt * valid) % L == 0
    return (B // bt, bt * valid), lambda w, k: pl.BlockSpec((1, bt * valid, w), lambda b, c: (b, 0, k))


def _ab_scan(P, B, nc, L, valid, bt, j, states, prev, prm):
    C0, n0, m0, S0 = states
    nst, _, H, dk, dv = C0.shape
    HK = H * dk
    lead, blk = _row_blocks(B, nc, L, valid, bt)
    assert P.shape[:2] == lead and dv == LANES and LANES % dk == 0 and B % bt == 0
    tmat, nlev = _ab_mats(L)
    wv = H * dv
    assert 2 * HK == wv
    nsm = 6 * wv // LANES
    sel = _sel_rows(2 * H)
    sec = lambda k: blk(wv, k)
    st_c = pl.BlockSpec((1, bt, H, dk, dv), lambda b, c: (j, b, 0, 0, 0))
    st_n = pl.BlockSpec((1, bt, 1, HK), lambda b, c: (j, b, 0, 0))
    st_m = pl.BlockSpec((1, bt, 1, H), lambda b, c: (j, b, 0, 0))
    full = lambda a: pl.BlockSpec(a.shape, lambda b, c: (0,) * a.ndim)
    n_alias = 0 if prev is None else 4
    consts = [prm["gate_bias"], prm["ml_norm"], prm["wa2"], prm["ba"], prm["gla_norm"], tmat, sel]
    in_specs = ([sec(k) for k in range(6)]
                + [blk(LANES, nsm), st_c, st_n, st_m, st_c]
                + [full(a) for a in consts]
                + [pl.BlockSpec(memory_space=pl.ANY)] * n_alias)
    out_specs = [sec(0), sec(0), st_c, st_n, st_m, st_c]
    out_shape = [
        jax.ShapeDtypeStruct(lead + (wv,), _mix_dtype(L, valid)),
        jax.ShapeDtypeStruct(lead + (wv,), _mix_dtype(L, valid)),
        jax.ShapeDtypeStruct(C0.shape, F32), jax.ShapeDtypeStruct(n0.shape, F32),
        jax.ShapeDtypeStruct(m0.shape, F32), jax.ShapeDtypeStruct(S0.shape, F32),
    ]
    n_in = len(in_specs) - n_alias
    aliases = {n_in + k: 2 + k for k in range(n_alias)}
    args = [P] * 7 + [C0, n0, m0, S0] + consts + (list(prev) if prev is not None else [])
    return pl.pallas_call(
        functools.partial(_ab_kernel, L, valid, nlev, nc, bt, H, dk, dv, n_alias),
        grid=(B // bt, nc),
        in_specs=in_specs,
        out_specs=out_specs,
        out_shape=out_shape,
        input_output_aliases=aliases,
        scratch_shapes=[pltpu.VMEM((bt, HK, dv), F32), pltpu.VMEM((bt, 1, HK), F32),
                        pltpu.VMEM((bt, 1, H), F32), pltpu.VMEM((bt, HK, dv), F32)],
        compiler_params=pltpu.CompilerParams(
            dimension_semantics=("parallel", "arbitrary"), vmem_limit_bytes=VMEM_LIMIT),
        name="ab_scan",
    )(*args)


def _ssd_kernel(L, valid, nc, bt, G, hpg, hd, wave, preact, n_alias,
                p_ref, cv_ref, h0_ref,
                cw_ref, cb_ref, dtbr_ref, dtbc_ref, alr_ref, alc_ref, dsk_ref, nrm_ref,
                tmat_ref, exp_ref, sel_ref, *rest):
    y_ref, h_ref, catx, catb, catc, hs = rest[n_alias:]
    cidx = pl.program_id(1)
    gw = hpg * hd
    nh = G * hpg
    inner = nh * hd
    N = h0_ref.shape[4]
    W = cw_ref.shape[0]
    P0 = SUBLANES - (W - 1)
    per = LANES // hd
    cats = ((catx, 0, inner), (catb, inner, G * N), (catc, inner + G * N, G * N))

    lay = _SsdLayout(inner, G * N)

    @pl.when(cidx == 0)
    def _():
        if not preact:
            for cat, off, wid in cats:
                cat[:, P0:SUBLANES, :] = cv_ref[0, :, :, off:off + wid]
        hs[...] = h0_ref[0].reshape(bt, inner, N)

    t_col = lax.broadcasted_iota(jnp.int32, (L, 1), 0)
    s_row = lax.broadcasted_iota(jnp.int32, (1, L), 1)
    tt = lax.broadcasted_iota(jnp.int32, (L, L), 0)
    sc = lax.broadcasted_iota(jnp.int32, (L, L), 1)
    causal = sc <= tt
    lane = lax.broadcasted_iota(jnp.int32, (L, LANES), 1)
    upper = tmat_ref[0:L, :]
    lower = tmat_ref[L:2 * L, :]

    def conv_silu(cat, bi, off, lo, wid):
        xall = cat[bi, :, lo:lo + wid]
        cols = slice(off + lo, off + lo + wid)
        acc = cb_ref[:, cols] + xall[SUBLANES:SUBLANES + L] * cw_ref[W - 1:W, cols]
        for w in range(W - 1):
            tap = pltpu.roll(xall, W - 1 - w, axis=0)[SUBLANES:SUBLANES + L]
            acc = acc + tap * cw_ref[w:w + 1, cols]
        return _silu(acc)

    prow = lambda bi, off, wid: _seq_tile(p_ref, bi, slice(off, off + wid), L, valid)
    if not preact:
        assert nc == 1
        for bi in range(bt):
            for g in range(G):
                catx[bi, SUBLANES:SUBLANES + L, g * gw:(g + 1) * gw] = prow(bi, lay.xbc(g * gw), gw)
            catb[bi, SUBLANES:SUBLANES + L, :] = prow(bi, lay.xbc(inner), G * N)
            catc[bi, SUBLANES:SUBLANES + L, :] = prow(bi, lay.xbc(inner + G * N), G * N)

    BI = range(bt)
    each = lambda keys, f: {k: f(*k) for k in keys}
    gsl = lambda g: slice(g * gw, (g + 1) * gw)
    dt_raw = [prow(bi, lay.dt, LANES) for bi in BI]
    dtc = [_softplus(x + dtbr_ref[...]) for x in dt_raw]
    dtr = [_softplus(_rows_of_transpose(sel_ref[...], x) + dtbc_ref[...]) for x in dt_raw]
    if valid < L:
        dtc = [jnp.where(t_col < valid, x, 0.0) for x in dtc]
        dtr = [jnp.where(s_row < valid, x, 0.0) for x in dtr]
    cs_col = [_dot3(lower, x * (-jnp.exp(alr_ref[...]))) for x in dtc]
    cs_row = [_dot3_right(x * (-jnp.exp(alc_ref[...])), upper) for x in dtr]
    cd3 = [_split3(jnp.concatenate([cs_col[bi], dtc[bi]], axis=0)) for bi in BI]

    problems = [(bi, g) for bi in BI for g in range(G)]
    for w0 in range(0, len(problems), wave):
        keys = problems[w0:w0 + wave]
        hkeys = [(bi, g, jl) for bi, g in keys for jl in range(hpg)]
        if preact:
            xa = each(keys, lambda bi, g: prow(bi, lay.xbc(g * gw), gw))
            Bm = each(keys, lambda bi, g: prow(bi, lay.xbc(inner + g * N), N))
            Cm = each(keys, lambda bi, g: prow(bi, lay.xbc(inner + G * N + g * N), N))
            zg = each(keys, lambda bi, g: prow(bi, lay.z(g * gw), gw))
        else:
            xa = each(keys, lambda bi, g: conv_silu(catx, bi, 0, g * gw, gw))
            Bm = each(keys, lambda bi, g: conv_silu(catb, bi, inner, g * N, N))
            Cm = each(keys, lambda bi, g: conv_silu(catc, bi, inner + G * N, g * N, N))
            zg = each(keys, lambda bi, g: _silu(prow(bi, lay.z(g * gw), gw)))
        ce = each(keys, lambda bi, g: (
            jnp.dot(cd3[bi][0], exp_ref[:, gsl(g)], preferred_element_type=F32)
            + jnp.dot(cd3[bi][1], exp_ref[:, gsl(g)], preferred_element_type=F32)
            + jnp.dot(cd3[bi][2], exp_ref[:, gsl(g)], preferred_element_type=F32)))
        cs_exp = each(keys, lambda bi, g: ce[bi, g][0:L])
        dt_exp = each(keys, lambda bi, g: ce[bi, g][L:2 * L])
        Bmb = each(keys, lambda bi, g: Bm[bi, g].astype(BF16))
        Cmb = each(keys, lambda bi, g: Cm[bi, g].astype(BF16))
        CB = each(keys, lambda bi, g: _dot_nt(Cmb[bi, g], Bmb[bi, g]))
        hs0 = each(keys, lambda bi, g: hs[bi, gsl(g), :])
        yc = each(keys, lambda bi, g: _dot_nt(Cmb[bi, g], hs0[bi, g]) * jnp.exp(cs_exp[bi, g]))
        xab = each(keys, lambda bi, g: xa[bi, g].astype(BF16))
        seg = each(hkeys, lambda bi, g, jl: jnp.where(
            causal, cs_col[bi][:, g * hpg + jl:g * hpg + jl + 1] - cs_row[bi][g * hpg + jl:g * hpg + jl + 1, :],
            -jnp.inf))
        mj = each(hkeys, lambda bi, g, jl: CB[bi, g] * (
            jnp.exp(seg[bi, g, jl]) * dtr[bi][g * hpg + jl:g * hpg + jl + 1, :]))
        yh = each(hkeys, lambda bi, g, jl: _dot(
            mj[bi, g, jl], xab[bi, g][:, (jl // per) * LANES:(jl // per + 1) * LANES]))
        yi = each(keys, lambda bi, g: jnp.concatenate(
            [_select_chain(lane, [yh[bi, g, p * per + u] for u in range(per)], hd)
             for p in range(gw // LANES)], axis=1))
        y = each(keys, lambda bi, g: yc[bi, g] + yi[bi, g] + dsk_ref[:, gsl(g)] * xa[bi, g])
        y = each(keys, lambda bi, g: y[bi, g] * zg[bi, g])
        y = each(keys, lambda bi, g: y[bi, g] * lax.rsqrt(
            jnp.mean(y[bi, g] * y[bi, g], axis=1, keepdims=True) + LN_EPS) * nrm_ref[:, gsl(g)])
        if valid == L:
            for bi, g in keys:
                y_ref[bi, :, gsl(g)] = y[bi, g].astype(y_ref.dtype)
        else:
            assert len(keys) == bt * G
            for g in range(G):
                _store_seqs(y_ref, gsl(g), [y[bi, g] for bi in BI], L, valid)

        xw = each(keys, lambda bi, g: xa[bi, g] * (
            jnp.exp(cs_exp[bi, g][L - 1:L, :] - cs_exp[bi, g]) * dt_exp[bi, g]))
        upd = each(keys, lambda bi, g: _dot_tn(xw[bi, g], Bmb[bi, g]))
        dec = each(hkeys, lambda bi, g, jl: jnp.exp(cs_row[bi][g * hpg + jl:g * hpg + jl + 1, L - 1:L]))
        for bi, g, jl in hkeys:
            r0 = g * gw + jl * hd
            hs[bi, r0:r0 + hd, :] = (dec[bi, g, jl] * hs0[bi, g][jl * hd:(jl + 1) * hd, :]
                                     + upd[bi, g][jl * hd:(jl + 1) * hd, :])

    @pl.when(cidx == nc - 1)
    def _():
        h_ref[0] = hs[...].reshape(h_ref.shape[1:])


def _ssd_scan(P, B, nc, L, valid, bt, j, h0, conv0, prev, prm):
    nst, _, nh, hd, N = h0.shape
    G = prm["groups"]
    hpg = nh // G
    gw = hpg * hd
    inner = nh * hd
    GN = G * N
    W1, cd = conv0.shape[2], conv0.shape[3]
    lead, blk = _row_blocks(B, nc, L, valid, bt)
    assert P.shape[:2] == lead and N == LANES and gw % LANES == 0 and B % bt == 0
    assert nh <= LANES and inner % GN == 0 and cd == inner + 2 * GN
    tmat = _ssd_mats(L)
    sel = _sel_rows(nh)
    wave = bt * G if L <= 2 * SUBLANES else 2
    lay = _SsdLayout(inner, GN)
    assert P.shape[2] == 2 * lay.tile and lay.half % gw == 0 and lay.zhalf % gw == 0
    preact = valid == L
    crows = SUBLANES if preact else SUBLANES + L
    full = lambda a: pl.BlockSpec(a.shape, lambda b, c: (0,) * a.ndim)
    st_h = pl.BlockSpec((1, bt, nh, hd, N), lambda b, c: (j, b, 0, 0, 0))
    n_alias = 0 if prev is None else 1
    consts = [prm["conv_w"], prm["conv_b"], prm["dtb_row"], prm["dtb_col"], prm["alog_row"],
              prm["alog_col"], prm["dskip"], prm["norm"], tmat, prm["expand"], sel]
    in_specs = [
        blk(2 * lay.tile, 0),
        pl.BlockSpec((1, bt, W1, cd), lambda b, c: (j, b, 0, 0)),
        st_h,
    ] + [full(a) for a in consts] + [pl.BlockSpec(memory_space=pl.ANY)] * n_alias
    out_specs = [blk(inner, 0), st_h]
    out_shape = [jax.ShapeDtypeStruct(lead + (inner,), _mix_dtype(L, valid)),
                 jax.ShapeDtypeStruct(h0.shape, F32)]
    n_in = len(in_specs) - n_alias
    args = [P, conv0, h0] + consts + ([prev] if prev is not None else [])
    return pl.pallas_call(
        functools.partial(_ssd_kernel, L, valid, nc, bt, G, hpg, hd, wave, preact, n_alias),
        grid=(B // bt, nc),
        in_specs=in_specs,
        out_specs=out_specs,
        out_shape=out_shape,
        input_output_aliases={n_in: 1} if n_alias else {},
        scratch_shapes=[pltpu.VMEM((bt, crows, inner), F32), pltpu.VMEM((bt, crows, GN), F32),
                        pltpu.VMEM((bt, crows, GN), F32), pltpu.VMEM((bt, inner, N), F32)],
        compiler_params=pltpu.CompilerParams(
            dimension_semantics=("parallel", "arbitrary"), vmem_limit_bytes=VMEM_LIMIT),
        name="ssd_scan",
    )(*args)


def _prep_ab(j, H, dk, dv, w_in, ig_bias, fg_bias, ml_norm, wa2, ba, gla_norm, w_out):
    rank = wa2.shape[1]
    qk, vv = H * dk, H * dv
    o = np.cumsum([0, qk, qk, vv, vv, H, H, qk, qk, vv, vv, rank])
    assert int(o[-1]) == w_in.shape[2]
    wj = w_in[j]
    small = jnp.concatenate([wj[:, o[4]:o[6]], wj[:, o[10]:o[11]]], axis=1)
    small = jnp.pad(small, ((0, 0), (0, LANES - small.shape[1])))
    w = jnp.concatenate([wj[:, :o[4]], wj[:, o[6]:o[10]], small], axis=1).astype(BF16)
    wa2p = jnp.zeros((LANES, qk), F32).at[2 * H:2 * H + rank, :].set(wa2[j]).astype(BF16)
    return {
        "w_in": w,
        "gate_bias": jnp.concatenate([ig_bias[j], fg_bias[j]]).astype(F32).reshape(2 * H, 1),
        "ml_norm": ml_norm[j].reshape(1, vv),
        "gla_norm": gla_norm[j].reshape(1, vv),
        "wa2": wa2p,
        "ba": ba[j].reshape(1, qk),
        "w_out_ml": w_out[j][:vv].astype(BF16),
        "w_out_gla": w_out[j][vv:].astype(BF16),
    }


def _prep_ssd(j, G, nh, hd, N, w_in, conv_w, conv_b, dt_bias, a_log, d_skip, norm_g, w_out):
    inner = nh * hd
    hpg = nh // G
    cd = inner + 2 * G * N
    wz, wxbc, wdt = w_in[j][:, :inner], w_in[j][:, inner:inner + cd], w_in[j][:, inner + cd:]
    lay = _SsdLayout(inner, G * N)
    wdt = jnp.pad(wdt, ((0, 0), (0, LANES - nh)))
    w = jnp.concatenate([wxbc[:, :lay.half], wz[:, :lay.zhalf], wdt,
                         wxbc[:, lay.half:], wz[:, lay.zhalf:], jnp.zeros_like(wdt)], axis=1).astype(BF16)
    lane_form = lambda v: jnp.pad(v, (0, LANES - nh)).reshape(1, LANES)
    e = np.zeros((LANES, inner), np.float32)
    for r in range(nh):
        e[r, r * hd:(r + 1) * hd] = 1.0
    return {
        "groups": G,
        "w_in": w,
        "conv_w": conv_w[j], "conv_b": conv_b[j].reshape(1, cd),
        "dtb_row": lane_form(dt_bias[j]), "dtb_col": dt_bias[j].reshape(nh, 1),
        "alog_row": lane_form(a_log[j]), "alog_col": a_log[j].reshape(nh, 1),
        "dskip": jnp.repeat(d_skip[j], hd).reshape(1, inner),
        "norm": norm_g[j].reshape(1, inner),
        "expand": jnp.asarray(e, dtype=BF16),
        "w_out": w_out[j].astype(BF16),
    }


def _largest_divisor(n, cap, step=1):
    return max(d for d in range(step, cap + 1, step) if n % d == 0)


def _plan(B, T, chunk):
    if T % chunk == 0:
        return chunk, T // chunk, _largest_divisor(B, 4), 1
    L = max(SUBLANES, 1 << int(math.ceil(math.log2(T))))
    per = L // T
    assert L % T == 0 and B % per == 0
    return L, 1, _largest_divisor(B, 8, per), _largest_divisor(B, 4, per)


def _trunk(x, states, ab_prm, ssd_prm, mlp, lns, alpha, chunk):
    B, T, D = x.shape
    mC, mn, mm, gS, sh, sconv = states
    n_ab, _, H, dk, dv = mC.shape
    L, nc, bt_ab, bt_ssd = _plan(B, T, chunk)
    valid = T if nc == 1 else L
    M = B * T
    tm = 512 if M % 512 == 0 else M
    tmm = 1024 if M % 1024 == 0 else tm
    X = x.reshape(M, D)
    ab_states = (mC, mn.reshape(n_ab, B, 1, H * dk), mm.reshape(n_ab, B, 1, H), gS)
    ab_out, h_out, ncv = None, None, []
    depth = mlp[0].shape[0]
    for l in range(depth):
        j = l // 2
        if l % 2 == 0:
            p = ab_prm[j]
            P = _proj(X, p["w_in"], tm, p["w_in"].shape[1])
            lead = _row_blocks(B, nc, L, valid, bt_ab)[0]
            res = _ab_scan(P.reshape(lead + (-1,)), B, nc, L, valid, bt_ab, j, ab_states, ab_out, p)
            ab_out = res[2:]
            ys = [res[0].reshape(M, -1), res[1].reshape(M, -1)]
            ws = [p["w_out_ml"], p["w_out_gla"]]
        else:
            p = ssd_prm[j]
            inner = p["norm"].shape[1]
            cd = p["conv_b"].shape[1]
            lay = _SsdLayout(inner, (cd - inner) // 2)
            W1 = sconv.shape[2]
            assert T >= W1
            if valid == L:
                tmc = max(t for t in (128, 256, 512) if T % t == 0)
                P, tail = _proj_conv(X, p["w_in"], p["conv_w"], p["conv_b"], sconv, j, B, tmc, lay)
                ncv.append(tail[:, SUBLANES - W1:])
            else:
                P = _proj(X, p["w_in"], tm, lay.tile)
                last = P.reshape(B, T, -1)[:, T - W1:]
                ncv.append(jnp.concatenate([last[:, :, :lay.half], last[:, :, lay.tile:lay.tile + lay.half]], axis=2))
            lead = _row_blocks(B, nc, L, valid, bt_ssd)[0]
            y, h_out = _ssd_scan(P.reshape(lead + (-1,)), B, nc, L, valid, bt_ssd, j, sh, sconv, h_out, p)
            ys = [y.reshape(M, -1)]
            ws = [p["w_out"]]
        X = _outproj_ln(ys, ws, X, lns[0], lns[1], l, alpha, tm)
        X = _mlp_ln(X, mlp[0], mlp[1], lns[2], lns[3], l, alpha, tmm, 1024)
    nC, nn_, nm, nS = ab_out
    return (X.reshape(B, T, D), nC, nn_.reshape(mn.shape), nm.reshape(mm.shape), nS, h_out, jnp.stack(ncv))


def kernel(x_prompt, x_sample, state_mlstm_C, state_mlstm_n, state_mlstm_m, state_gla_S, state_ssd_h,
           state_ssd_conv, ab_w_in, ab_ig_bias, ab_fg_bias, ab_ml_norm, ab_gla_wa2, ab_gla_ba, ab_gla_norm,
           ab_w_out, ssd_w_in, ssd_conv_w, ssd_conv_b, ssd_dt_bias, ssd_a_log, ssd_d, ssd_norm, ssd_w_out,
           mlp_w1, mlp_w2, ln_mix_g, ln_mix_b, ln_mlp_g, ln_mlp_b):
    depth = mlp_w1.shape[0]
    D = x_prompt.shape[2]
    alpha = (2 * depth) ** 0.25
    n_ab, _, H, dk, dv = state_mlstm_C.shape
    n_ssd, _, nh, hd, N = state_ssd_h.shape
    cd = state_ssd_conv.shape[3]
    G = (cd - nh * hd) // (2 * N)
    ab_prm = [_prep_ab(j, H, dk, dv, ab_w_in, ab_ig_bias, ab_fg_bias, ab_ml_norm, ab_gla_wa2,
                       ab_gla_ba, ab_gla_norm, ab_w_out) for j in range(n_ab)]
    ssd_prm = [_prep_ssd(j, G, nh, hd, N, ssd_w_in, ssd_conv_w, ssd_conv_b, ssd_dt_bias, ssd_a_log,
                         ssd_d, ssd_norm, ssd_w_out) for j in range(n_ssd)]
    mlp = (mlp_w1.astype(BF16), mlp_w2.astype(BF16))
    lns = tuple(a.reshape(depth, 1, D) for a in (ln_mix_g, ln_mix_b, ln_mlp_g, ln_mlp_b))

    Bp = x_prompt.shape[0]
    zeros = (jnp.zeros((n_ab, Bp, H, dk, dv), F32), jnp.zeros((n_ab, Bp, H, dk), F32),
             jnp.zeros((n_ab, Bp, H), F32), jnp.zeros((n_ab, Bp, H, dk, dv), F32),
             jnp.zeros((n_ssd, Bp, nh, hd, N), F32), jnp.zeros((n_ssd, Bp) + state_ssd_conv.shape[2:], F32))
    carried = (state_mlstm_C, state_mlstm_n, state_mlstm_m, state_gla_S, state_ssd_h, state_ssd_conv)
    chunk = 128
    yp = _trunk(x_prompt, zeros, ab_prm, ssd_prm, mlp, lns, alpha, chunk)
    ys = _trunk(x_sample, carried, ab_prm, ssd_prm, mlp, lns, alpha, chunk)
    return (yp[0], ys[0]) + yp[1:] + ys[1:]
```

```python
import functools
import itertools
import math

import jax
import jax.numpy as jnp
import numpy as np
from jax import lax
from jax.experimental import pallas as pl
from jax.experimental.pallas import tpu as pltpu

F32 = jnp.float32
BF16 = jnp.bfloat16

LN_EPS = 1e-5
GLA_TAU = 16.0
LANES = 128
SUBLANES = 8
VMEM_LIMIT = 48 * 1024 * 1024


def _dot(a, b):
    return jnp.dot(a.astype(BF16), b.astype(BF16), preferred_element_type=F32)


def _dot_nt(a, b):
    return lax.dot_general(a.astype(BF16), b.astype(BF16), (((1,), (1,)), ((), ())),
                           preferred_element_type=F32)


def _dot_tn(a, b):
    return lax.dot_general(a.astype(BF16), b.astype(BF16), (((0,), (0,)), ((), ())),
                           preferred_element_type=F32)


def _split3(x):
    hi = x.astype(BF16)
    r = x - hi.astype(F32)
    mid = r.astype(BF16)
    lo = (r - mid.astype(F32)).astype(BF16)
    return hi, mid, lo


def _dot3(t, x):
    hi, mid, lo = _split3(x)
    f = lambda p: jnp.dot(t, p, preferred_element_type=F32)
    return f(hi) + f(mid) + f(lo)


def _dot3_right(x, t):
    hi, mid, lo = _split3(x)
    f = lambda p: jnp.dot(p, t, preferred_element_type=F32)
    return f(hi) + f(mid) + f(lo)


def _rows_of_transpose(sel, x):
    hi, mid, lo = _split3(x)
    f = lambda p: lax.dot_general(sel, p, (((1,), (1,)), ((), ())), preferred_element_type=F32)
    return f(hi) + f(mid) + f(lo)


def _softplus(x):
    e = jnp.exp(-jnp.abs(x))
    u = 1.0 + e
    d = u - 1.0
    return jnp.maximum(x, 0.0) + jnp.where(d == 0.0, e, jnp.log(u) * (e / d))


def _log_sigmoid(x):
    return -_softplus(-x)


def _sigmoid(x):
    return 0.5 + 0.5 * jnp.tanh(0.5 * x)


def _silu(x):
    h = 0.5 * x
    return h + h * jnp.tanh(h)


def _seq_tile(ref, bi, cols, L, T):
    if T == L:
        return ref[bi, :, cols]
    r0 = bi * T
    a = (r0 // L) * L
    tile = ref[0, a:a + L, cols]
    return tile if r0 == a else pltpu.roll(tile, L - (r0 - a), axis=0)


def _store_seqs(ref, cols, vals, L, T, first=0):
    if T == L:
        for i, v in enumerate(vals):
            ref[first + i, :, cols] = v.astype(ref.dtype)
        return
    per = L // T
    assert first % per == 0 and len(vals) % per == 0
    row = lax.broadcasted_iota(jnp.int32, vals[0].shape, 0)
    for a in range(len(vals) // per):
        out = vals[a * per]
        for k in range(1, per):
            out = jnp.where(row < k * T, out, pltpu.roll(vals[a * per + k], k * T, axis=0))
        r0 = (first // per + a) * L
        ref[0, r0:r0 + L, cols] = out


def _interleave(*gens):
    gens = list(gens)
    while gens:
        for g in list(gens):
            try:
                next(g)
            except StopIteration:
                gens.remove(g)


def _layernorm_rows(r, g, b):
    mu = jnp.mean(r, axis=1, keepdims=True)
    d = r - mu
    var = jnp.mean(d * d, axis=1, keepdims=True)
    return d * lax.rsqrt(var + LN_EPS) * g + b


def _headnorm(h, g):
    mu = jnp.mean(h, axis=1, keepdims=True)
    d = h - mu
    var = jnp.mean(d * d, axis=1, keepdims=True)
    return d * lax.rsqrt(var + LN_EPS) * g


def _select_chain(idx, pieces, width):
    out = pieces[0]
    for u in range(1, len(pieces)):
        out = jnp.where(idx < u * width, out, pieces[u])
    return out


def _cumsum_mats(L):
    t = np.arange(L)[:, None]
    j = np.arange(L)[None, :]
    upper = (t <= j)
    lower = (j <= t)
    return upper, lower


def _ab_mats(L):
    nlev = int(round(math.log2(L)))
    assert 1 << nlev == L
    upper, lower = _cumsum_mats(L)
    t = np.arange(L)[:, None]
    j = np.arange(L)[None, :]
    mats = [upper, lower]
    for i in range(nlev):
        n = L >> (i + 1)
        mid = (t // (2 * n)) * (2 * n) + n - 1
        second = (t % (2 * n)) >= n
        m = np.where(second, (j > mid) & (j <= t), (j > t) & (j <= mid))
        mats.append(m)
    return jnp.asarray(np.concatenate(mats, axis=0).astype(np.float32), dtype=BF16), nlev


def _ssd_mats(L):
    upper, lower = _cumsum_mats(L)
    return jnp.asarray(np.concatenate([upper, lower], axis=0).astype(np.float32), dtype=BF16)


def _proj_kernel(x_ref, w_ref, o_ref):
    o_ref[...] = jnp.dot(x_ref[...].astype(BF16), w_ref[...], preferred_element_type=F32)


def _proj(x, w, tm, tn):
    M, K = x.shape
    N = w.shape[1]
    assert M % tm == 0 and N % tn == 0
    return pl.pallas_call(
        _proj_kernel,
        grid=(N // tn, M // tm),
        in_specs=[pl.BlockSpec((tm, K), lambda j, i: (i, 0)),
                  pl.BlockSpec((K, tn), lambda j, i: (0, j))],
        out_specs=pl.BlockSpec((tm, tn), lambda j, i: (i, j)),
        out_shape=jax.ShapeDtypeStruct((M, N), F32),
        compiler_params=pltpu.CompilerParams(
            dimension_semantics=("parallel", "parallel"), vmem_limit_bytes=VMEM_LIMIT),
        name="proj",
    )(x, w)


class _SsdLayout:
    def __init__(self, inner, gn):
        self.cd = inner + 2 * gn
        self.half = self.cd // 2
        self.zhalf = inner // 2
        self.tile = self.half + self.zhalf + LANES
        self.dt = self.half + self.zhalf
        assert self.cd % (2 * LANES) == 0 and inner % (2 * LANES) == 0

    def xbc(self, c):
        return c if c < self.half else self.tile + c - self.half

    def z(self, c):
        return self.half + c if c < self.zhalf else self.tile + self.half + c - self.zhalf


def _proj_conv_kernel(tps, nrb, half, zhalf, x_ref, w_ref, cw_ref, cb_ref, cv_ref, o_ref, tail_ref, hist):
    i = pl.program_id(1)
    W = cw_ref.shape[0]
    P0 = SUBLANES - (W - 1)
    rb = x_ref.shape[0] // nrb

    @pl.when(i % tps == 0)
    def _():
        hist[...] = jnp.zeros_like(hist)
        hist[P0:SUBLANES, :] = cv_ref[0, 0]

    def raw_rows(r):
        return jnp.dot(x_ref[r * rb:(r + 1) * rb, :].astype(BF16), w_ref[...], preferred_element_type=F32)

    def finish(r, raw, prev):
        rows = slice(r * rb, (r + 1) * rb)
        xall = jnp.concatenate([prev, raw[:, :half]], axis=0)
        acc = cb_ref[...] + xall[SUBLANES:] * cw_ref[W - 1:W, :]
        for w in range(W - 1):
            acc = acc + pltpu.roll(xall, W - 1 - w, axis=0)[SUBLANES:] * cw_ref[w:w + 1, :]
        o_ref[rows, :half] = _silu(acc)
        o_ref[rows, half:half + zhalf] = _silu(raw[:, half:half + zhalf])
        o_ref[rows, half + zhalf:] = raw[:, half + zhalf:]
        return raw[rb - SUBLANES:, :half]

    prev = hist[...]
    raw = raw_rows(0)
    for r in range(nrb):
        nxt = raw_rows(r + 1) if r + 1 < nrb else None
        prev = finish(r, raw, prev)
        raw = nxt
    hist[...] = prev
    tail_ref[0] = prev


def _proj_conv(x, w, conv_w, conv_b, conv0, j, B, tm, lay):
    M, K = x.shape
    T = M // B
    assert M % tm == 0 and T % tm == 0 and w.shape[1] == 2 * lay.tile and (tm // 4) % SUBLANES == 0
    tps = T // tm
    W1 = conv0.shape[2]
    return pl.pallas_call(
        functools.partial(_proj_conv_kernel, tps, 4, lay.half, lay.zhalf),
        grid=(2, M // tm),
        in_specs=[pl.BlockSpec((tm, K), lambda c, i: (i, 0)),
                  pl.BlockSpec((K, lay.tile), lambda c, i: (0, c)),
                  pl.BlockSpec((W1 + 1, lay.half), lambda c, i: (0, c)),
                  pl.BlockSpec((1, lay.half), lambda c, i: (0, c)),
                  pl.BlockSpec((1, 1, W1, lay.half), lambda c, i: (j, i // tps, 0, c))],
        out_specs=[pl.BlockSpec((tm, lay.tile), lambda c, i: (i, c)),
                   pl.BlockSpec((1, SUBLANES, lay.half), lambda c, i: (i // tps, 0, c))],
        out_shape=[jax.ShapeDtypeStruct((M, 2 * lay.tile), F32),
                   jax.ShapeDtypeStruct((B, SUBLANES, lay.cd), F32)],
        scratch_shapes=[pltpu.VMEM((SUBLANES, lay.half), F32)],
        compiler_params=pltpu.CompilerParams(
            dimension_semantics=("parallel", "arbitrary"), vmem_limit_bytes=VMEM_LIMIT),
        name="proj_conv",
    )(x, w, conv_w, conv_b, conv0)


def _outproj_ln_kernel(alpha, n_in, *refs):
    ys = refs[:n_in]
    ws = refs[n_in:2 * n_in]
    x_ref, g_ref, b_ref, o_ref = refs[2 * n_in:]
    r = alpha * x_ref[...]
    for y_ref, w_ref in zip(ys, ws):
        r = r + jnp.dot(y_ref[...].astype(BF16), w_ref[...], preferred_element_type=F32)
    o_ref[...] = _layernorm_rows(r, g_ref[0], b_ref[0])


def _outproj_ln(ys, ws, x, g, b, l, alpha, tm):
    M, D = x.shape
    assert M % tm == 0
    n_in = len(ys)
    in_specs = ([pl.BlockSpec((tm, y.shape[1]), lambda i: (i, 0)) for y in ys]
                + [pl.BlockSpec(w.shape, lambda i: (0, 0)) for w in ws]
                + [pl.BlockSpec((tm, D), lambda i: (i, 0)),
                   pl.BlockSpec((1, 1, D), lambda i: (l, 0, 0)),
                   pl.BlockSpec((1, 1, D), lambda i: (l, 0, 0))])
    return pl.pallas_call(
        functools.partial(_outproj_ln_kernel, alpha, n_in),
        grid=(M // tm,),
        in_specs=in_specs,
        out_specs=pl.BlockSpec((tm, D), lambda i: (i, 0)),
        out_shape=jax.ShapeDtypeStruct((M, D), F32),
        compiler_params=pltpu.CompilerParams(
            dimension_semantics=("parallel",), vmem_limit_bytes=VMEM_LIMIT),
        name="outproj_ln",
    )(*ys, *ws, x, g, b)


def _mlp_kernel(alpha, nf, x_ref, w1_ref, w2_ref, g_ref, b_ref, o_ref, acc_ref):
    f = pl.program_id(1)

    @pl.when(f == 0)
    def _():
        acc_ref[...] = jnp.zeros_like(acc_ref)

    h = jnp.dot(x_ref[...].astype(BF16), w1_ref[0].astype(BF16), preferred_element_type=F32)
    h = jnp.square(jnp.maximum(h, 0.0))
    acc_ref[...] += jnp.dot(h.astype(BF16), w2_ref[0].astype(BF16), preferred_element_type=F32)

    @pl.when(f == nf - 1)
    def _():
        r = alpha * x_ref[...] + acc_ref[...]
        o_ref[...] = _layernorm_rows(r, g_ref[0], b_ref[0])


def _mlp_ln(x, w1, w2, g, b, l, alpha, tm, tf):
    M, D = x.shape
    Fdim = w1.shape[2]
    assert M % tm == 0 and Fdim % tf == 0
    nf = Fdim // tf
    return pl.pallas_call(
        functools.partial(_mlp_kernel, alpha, nf),
        grid=(M // tm, nf),
        in_specs=[pl.BlockSpec((tm, D), lambda i, f: (i, 0)),
                  pl.BlockSpec((1, D, tf), lambda i, f: (l, 0, f)),
                  pl.BlockSpec((1, tf, D), lambda i, f: (l, f, 0)),
                  pl.BlockSpec((1, 1, D), lambda i, f: (l, 0, 0)),
                  pl.BlockSpec((1, 1, D), lambda i, f: (l, 0, 0))],
        out_specs=pl.BlockSpec((tm, D), lambda i, f: (i, 0)),
        out_shape=jax.ShapeDtypeStruct((M, D), F32),
        scratch_shapes=[pltpu.VMEM((tm, D), F32)],
        compiler_params=pltpu.CompilerParams(
            dimension_semantics=("parallel", "arbitrary"), vmem_limit_bytes=VMEM_LIMIT),
        name="mlp_ln",
    )(x, w1, w2, g, b)


def _ab_kernel(L, valid, nlev, nc, bt, H, dk, dv, n_alias,
               qk_ref, v_ref, mo_ref, gqk_ref, gv_ref, gg_ref, sm_ref,
               c0_ref, n0_ref, m0_ref, s0_ref, gb_ref, mln_ref, wa2_ref, ba_ref, gln_ref, tmat_ref,
               sel_ref, *rest):
    ml_ref, gla_ref, c_ref, n_ref, m_ref, s_ref, cs, ns, ms, ss = rest[n_alias:]
    c = pl.program_id(1)
    per = LANES // dk
    HK = H * dk

    @pl.when(c == 0)
    def _():
        cs[...] = c0_ref[0].reshape(bt, HK, dv)
        ns[...] = n0_ref[0]
        ms[...] = m0_ref[0]
        ss[...] = s0_ref[0].reshape(bt, HK, dv)

    t_col = lax.broadcasted_iota(jnp.int32, (L, 1), 0)
    s_row = lax.broadcasted_iota(jnp.int32, (1, L), 1)
    tt = lax.broadcasted_iota(jnp.int32, (L, L), 0)
    sc = lax.broadcasted_iota(jnp.int32, (L, L), 1)
    causal = sc <= tt
    eye = sc == tt
    lane = lax.broadcasted_iota(jnp.int32, (L, LANES), 1)
    lane1 = lax.broadcasted_iota(jnp.int32, (1, LANES), 1)
    laneh = lax.broadcasted_iota(jnp.int32, (1, H), 1)
    rowp = lax.broadcasted_iota(jnp.int32, (LANES, dv), 0)
    ek = lax.broadcasted_iota(jnp.int32, (LANES, LANES), 0) == lax.broadcasted_iota(jnp.int32, (LANES, LANES), 1)
    inhead = [(lane >= u * dk) & (lane < (u + 1) * dk) for u in range(per)]
    upper = tmat_ref[0:L, :]
    scale = dk ** -0.5

    def to_col(row):
        return jnp.sum(jnp.where(eye, row, 0.0), axis=1, keepdims=True)

    psl = lambda p: slice(p * LANES, (p + 1) * LANES)
    hsl = lambda h: slice(h * dv, (h + 1) * dv)
    each = lambda keys, f: {k: f(*k) for k in keys}
    neg_inf = -jnp.inf
    allc = slice(None)
    tile = lambda ref, bi, cols: _seq_tile(ref, bi, cols, L, valid)

    def store(ref, vals, BI):
        for h in range(H):
            _store_seqs(ref, hsl(h), [vals[bi, h] for bi in BI], L, valid, BI[0])

    def mlstm(BI):
        pairs = [(bi, p) for bi in BI for p in range(H // per)]
        heads = [(bi, h) for bi in BI for h in range(H)]
        seqs = lambda f: {bi: f(bi) for bi in BI}
        ig8 = seqs(lambda bi: _rows_of_transpose(sel_ref[...], tile(sm_ref, bi, allc)) + gb_ref[...])
        yield
        lf8 = seqs(lambda bi: _log_sigmoid(ig8[bi]))
        yield
        if valid < L:
            ok = s_row < valid
            ig8 = seqs(lambda bi: jnp.where(ok, ig8[bi], neg_inf))
            lf8 = seqs(lambda bi: jnp.where(ok, lf8[bi], 0.0))
        b8 = seqs(lambda bi: _dot3_right(lf8[bi], upper))
        yield
        m_prev = seqs(lambda bi: ms[bi])
        qk = seqs(lambda bi: tile(qk_ref, bi, allc))
        yield
        Qp = each(pairs, lambda bi, p: qk[bi][:, psl(p)])
        Kp = each(pairs, lambda bi, p: qk[bi][:, HK + p * LANES:HK + (p + 1) * LANES] * scale)
        yield
        Kpb = each(pairs, lambda bi, p: Kp[bi, p].astype(BF16))
        C0p = each(pairs, lambda bi, p: cs[bi, psl(p), :])
        n0p = each(pairs, lambda bi, p: ns[bi][:, psl(p)])
        yield
        ig_row = each(heads, lambda bi, h: ig8[bi][h:h + 1, :])
        b_row = each(heads, lambda bi, h: b8[bi][H + h:H + h + 1, :])
        b_col = each(heads, lambda bi, h: to_col(b_row[bi, h]))
        yield
        ig_col = each(heads, lambda bi, h: to_col(ig_row[bi, h]))
        yield
        D = each(heads, lambda bi, h: jnp.where(causal, b_col[bi, h] - b_row[bi, h] + ig_row[bi, h], neg_inf))
        yield
        g_col = each(heads, lambda bi, h: b_col[bi, h] + m_prev[bi][:, h:h + 1])
        m_col = each(heads, lambda bi, h: jnp.maximum(g_col[bi, h], jnp.max(D[bi, h], axis=1, keepdims=True)))
        yield
        w_intra = each(heads, lambda bi, h: jnp.exp(D[bi, h] - m_col[bi, h]))
        yield
        w_inter = each(heads, lambda bi, h: jnp.exp(g_col[bi, h] - m_col[bi, h]))
        Qh = each(heads, lambda bi, h: jnp.where(inhead[h % per], Qp[bi, h // per], 0.0))
        yield
        Qhb = each(heads, lambda bi, h: Qh[bi, h].astype(BF16))
        vb = each(heads, lambda bi, h: tile(v_ref, bi, hsl(h)).astype(BF16))
        yield
        s = each(heads, lambda bi, h: _dot_nt(Qhb[bi, h], Kpb[bi, h // per]) * w_intra[bi, h])
        yield
        qc = each(heads, lambda bi, h: _dot(Qhb[bi, h], C0p[bi, h // per]))
        yield
        num = each(heads, lambda bi, h: _dot(s[bi, h], vb[bi, h]) + w_inter[bi, h] * qc[bi, h])
        yield
        den = each(heads, lambda bi, h: (jnp.sum(s[bi, h], axis=1, keepdims=True) + w_inter[bi, h]
                                         * jnp.sum(Qh[bi, h] * n0p[bi, h // per], axis=1, keepdims=True)))
        yield
        hh = each(heads, lambda bi, h: num[bi, h] / jnp.maximum(jnp.abs(den[bi, h]), jnp.exp(-m_col[bi, h])))
        yield
        ml = each(heads, lambda bi, h: (_headnorm(hh[bi, h], mln_ref[:, hsl(h)])
                                        * _sigmoid(tile(mo_ref, bi, hsl(h)))))
        yield
        store(ml_ref, ml, BI)
        yield
        mL = each(heads, lambda bi, h: m_col[bi, h][L - 1:L, :])
        wL_col = each(heads, lambda bi, h: jnp.exp(b_col[bi, h][L - 1:L, :] - b_col[bi, h] + ig_col[bi, h] - mL[bi, h]))
        yield
        wL0 = each(heads, lambda bi, h: jnp.exp(g_col[bi, h][L - 1:L, :] - mL[bi, h]))
        kw = each(heads, lambda bi, h: Kp[bi, h // per] * wL_col[bi, h])
        yield
        c_new = each(heads, lambda bi, h: wL0[bi, h] * C0p[bi, h // per] + _dot_tn(kw[bi, h], vb[bi, h]))
        yield
        n_new = each(heads, lambda bi, h: wL0[bi, h] * n0p[bi, h // per] + jnp.sum(kw[bi, h], axis=0, keepdims=True))
        yield
        for bi, p in pairs:
            cs[bi, psl(p), :] = _select_chain(rowp, [c_new[bi, p * per + u] for u in range(per)], dk)
            ns[bi, :, psl(p)] = _select_chain(lane1, [n_new[bi, p * per + u] for u in range(per)], dk)
        yield
        for bi in BI:
            m_new = m_prev[bi]
            for h in range(H):
                m_new = jnp.where(laneh == h, mL[bi, h], m_new)
            ms[bi] = m_new
        yield

    def gla(BI):
        pairs = [(bi, p) for bi in BI for p in range(H // per)]
        heads = [(bi, h) for bi in BI for h in range(H)]
        seqs = lambda f: {bi: f(bi) for bi in BI}
        gqk = seqs(lambda bi: tile(gqk_ref, bi, allc))
        yield
        Q2 = seqs(lambda bi: gqk[bi][:, :HK] * scale)
        K2 = seqs(lambda bi: gqk[bi][:, HK:])
        if valid < L:
            K2 = seqs(lambda bi: jnp.where(t_col < valid, K2[bi], 0.0))
        yield
        la = seqs(lambda bi: _log_sigmoid(_dot(tile(sm_ref, bi, allc), wa2_ref[...]) + ba_ref[...])
                  * (1.0 / GLA_TAU))
        if valid < L:
            la = seqs(lambda bi: jnp.where(t_col < valid, la[bi], 0.0))
        yield
        TL = seqs(lambda bi: _dot3(tmat_ref[L:(2 + nlev) * L, :], la[bi]))
        yield
        Q2b = seqs(lambda bi: Q2[bi].astype(BF16))
        K2b = seqs(lambda bi: K2[bi].astype(BF16))
        yield
        scores = each(heads, lambda bi, h: jnp.where(
            eye, _dot_nt(jnp.where(inhead[h % per], Q2b[bi][:, psl(h // per)], 0), K2b[bi][:, psl(h // per)]), 0.0))
        yield
        for i in range(nlev):
            n = L >> (i + 1)
            second = (t_col & n) != 0
            En = seqs(lambda bi: jnp.exp(TL[bi][(1 + i) * L:(2 + i) * L]))
            yield
            X = seqs(lambda bi: (jnp.where(second, Q2[bi], K2[bi]) * En[bi]).astype(BF16))
            yield
            Xk = seqs(lambda bi: jnp.where(second, 0, X[bi]))
            yield
            sn = each(heads, lambda bi, h: _dot_nt(
                jnp.where(second & inhead[h % per], X[bi][:, psl(h // per)], 0), Xk[bi][:, psl(h // per)]))
            yield
            if i > 0:
                sh = int(round(math.log2(2 * n)))
                same = (tt >> sh) == (sc >> sh)
                sn = each(heads, lambda bi, h: jnp.where(same, sn[bi, h], 0.0))
            scores = each(heads, lambda bi, h: scores[bi, h] + sn[bi, h])
            yield
        A = seqs(lambda bi: TL[bi][0:L])
        AL = seqs(lambda bi: A[bi][L - 1:L, :])
        QA = seqs(lambda bi: (Q2[bi] * jnp.exp(A[bi])).astype(BF16))
        yield
        kd = seqs(lambda bi: (K2[bi] * jnp.exp(AL[bi] - A[bi])).astype(BF16))
        eAL = seqs(lambda bi: jnp.exp(AL[bi]))
        yield
        S0p = each(pairs, lambda bi, p: ss[bi, psl(p), :])
        dec_col = each(pairs, lambda bi, p: jnp.sum(jnp.where(ek, eAL[bi][:, psl(p)], 0.0), axis=1, keepdims=True))
        yield
        v2b = each(heads, lambda bi, h: tile(gv_ref, bi, hsl(h)).astype(BF16))
        qs = each(heads, lambda bi, h: _dot(jnp.where(inhead[h % per], QA[bi][:, psl(h // per)], 0), S0p[bi, h // per]))
        yield
        o = each(heads, lambda bi, h: _dot(scores[bi, h], v2b[bi, h]) + qs[bi, h])
        yield
        gla_out = each(heads, lambda bi, h: (_headnorm(o[bi, h], gln_ref[:, hsl(h)])
                                             * _silu(tile(gg_ref, bi, hsl(h)))))
        yield
        store(gla_ref, gla_out, BI)
        yield
        upd = each(heads, lambda bi, h: _dot_tn(kd[bi][:, psl(h // per)], v2b[bi, h]))
        yield
        for bi, p in pairs:
            ss[bi, psl(p), :] = (dec_col[bi, p] * S0p[bi, p]
                                 + _select_chain(rowp, [upd[bi, p * per + u] for u in range(per)], dk))
        yield

    per_tile = L // valid
    if bt % 2 == 0 and (bt // 2) % per_tile == 0:
        ga, gb = range(bt // 2), range(bt // 2, bt)
        _interleave(itertools.chain(mlstm(ga), gla(ga)), itertools.chain(gla(gb), mlstm(gb)))
    else:
        _interleave(mlstm(range(bt)), gla(range(bt)))

    @pl.when(c == nc - 1)
    def _():
        c_ref[0] = cs[...].reshape(bt, H, dk, dv)
        n_ref[0] = ns[...]
        m_ref[0] = ms[...]
        s_ref[0] = ss[...].reshape(bt, H, dk, dv)


def _sel_rows(r):
    return jnp.asarray(np.eye(r, LANES, dtype=np.float32), dtype=BF16)


def _mix_dtype(L, valid):
    return BF16 if valid == L else F32


def _row_blocks(B, nc, L, valid, bt):
    if valid == L:
        return (B, nc * L), lambda w, k: pl.BlockSpec((bt, L, w), lambda b, c: (b, c, k))
    assert nc == 1 and L % valid == 0 and (bt * valid) % L == 0
    return (B // bt, bt * valid), lambda w, k: pl.BlockSpec((1, bt * valid, w), lambda b, c: (b, 0, k))


def _ab_scan(P, B, nc, L, valid, bt, j, states, prev, prm):
    C0, n0, m0, S0 = states
    nst, _, H, dk, dv = C0.shape
    HK = H * dk
    lead, blk = _row_blocks(B, nc, L, valid, bt)
    assert P.shape[:2] == lead and dv == LANES and LANES % dk == 0 and B % bt == 0
    tmat, nlev = _ab_mats(L)
    wv = H * dv
    assert 2 * HK == wv
    nsm = 6 * wv // LANES
    sel = _sel_rows(2 * H)
    sec = lambda k: blk(wv, k)
    st_c = pl.BlockSpec((1, bt, H, dk, dv), lambda b, c: (j, b, 0, 0, 0))
    st_n = pl.BlockSpec((1, bt, 1, HK), lambda b, c: (j, b, 0, 0))
    st_m = pl.BlockSpec((1, bt, 1, H), lambda b, c: (j, b, 0, 0))
    full = lambda a: pl.BlockSpec(a.shape, lambda b, c: (0,) * a.ndim)
    n_alias = 0 if prev is None else 4
    consts = [prm["gate_bias"], prm["ml_norm"], prm["wa2"], prm["ba"], prm["gla_norm"], tmat, sel]
    in_specs = ([sec(k) for k in range(6)]
                + [blk(LANES, nsm), st_c, st_n, st_m, st_c]
                + [full(a) for a in consts]
                + [pl.BlockSpec(memory_space=pl.ANY)] * n_alias)
    out_specs = [sec(0), sec(0), st_c, st_n, st_m, st_c]
    out_shape = [
        jax.ShapeDtypeStruct(lead + (wv,), _mix_dtype(L, valid)),
        jax.ShapeDtypeStruct(lead + (wv,), _mix_dtype(L, valid)),
        jax.ShapeDtypeStruct(C0.shape, F32), jax.ShapeDtypeStruct(n0.shape, F32),
        jax.ShapeDtypeStruct(m0.shape, F32), jax.ShapeDtypeStruct(S0.shape, F32),
    ]
    n_in = len(in_specs) - n_alias
    aliases = {n_in + k: 2 + k for k in range(n_alias)}
    args = [P] * 7 + [C0, n0, m0, S0] + consts + (list(prev) if prev is not None else [])
    return pl.pallas_call(
        functools.partial(_ab_kernel, L, valid, nlev, nc, bt, H, dk, dv, n_alias),
        grid=(B // bt, nc),
        in_specs=in_specs,
        out_specs=out_specs,
        out_shape=out_shape,
        input_output_aliases=aliases,
        scratch_shapes=[pltpu.VMEM((bt, HK, dv), F32), pltpu.VMEM((bt, 1, HK), F32),
                        pltpu.VMEM((bt, 1, H), F32), pltpu.VMEM((bt, HK, dv), F32)],
        compiler_params=pltpu.CompilerParams(
            dimension_semantics=("parallel", "arbitrary"), vmem_limit_bytes=VMEM_LIMIT),
        name="ab_scan",
    )(*args)


def _ssd_kernel(L, valid, nc, bt, G, hpg, hd, wave, preact, n_alias,
                p_ref, cv_ref, h0_ref,
                cw_ref, cb_ref, dtbr_ref, dtbc_ref, alr_ref, alc_ref, dsk_ref, nrm_ref,
                tmat_ref, exp_ref, sel_ref, *rest):
    y_ref, h_ref, catx, catb, catc, hs = rest[n_alias:]
    cidx = pl.program_id(1)
    gw = hpg * hd
    nh = G * hpg
    inner = nh * hd
    N = h0_ref.shape[4]
    W = cw_ref.shape[0]
    P0 = SUBLANES - (W - 1)
    per = LANES // hd
    cats = ((catx, 0, inner), (catb, inner, G * N), (catc, inner + G * N, G * N))

    lay = _SsdLayout(inner, G * N)

    @pl.when(cidx == 0)
    def _():
        if not preact:
            for cat, off, wid in cats:
                cat[:, P0:SUBLANES, :] = cv_ref[0, :, :, off:off + wid]
        hs[...] = h0_ref[0].reshape(bt, inner, N)

    t_col = lax.broadcasted_iota(jnp.int32, (L, 1), 0)
    s_row = lax.broadcasted_iota(jnp.int32, (1, L), 1)
    tt = lax.broadcasted_iota(jnp.int32, (L, L), 0)
    sc = lax.broadcasted_iota(jnp.int32, (L, L), 1)
    causal = sc <= tt
    lane = lax.broadcasted_iota(jnp.int32, (L, LANES), 1)
    upper = tmat_ref[0:L, :]
    lower = tmat_ref[L:2 * L, :]

    def conv_silu(cat, bi, off, lo, wid):
        xall = cat[bi, :, lo:lo + wid]
        cols = slice(off + lo, off + lo + wid)
        acc = cb_ref[:, cols] + xall[SUBLANES:SUBLANES + L] * cw_ref[W - 1:W, cols]
        for w in range(W - 1):
            tap = pltpu.roll(xall, W - 1 - w, axis=0)[SUBLANES:SUBLANES + L]
            acc = acc + tap * cw_ref[w:w + 1, cols]
        return _silu(acc)

    prow = lambda bi, off, wid: _seq_tile(p_ref, bi, slice(off, off + wid), L, valid)
    if not preact:
        assert nc == 1
        for bi in range(bt):
            for g in range(G):
                catx[bi, SUBLANES:SUBLANES + L, g * gw:(g + 1) * gw] = prow(bi, lay.xbc(g * gw), gw)
            catb[bi, SUBLANES:SUBLANES + L, :] = prow(bi, lay.xbc(inner), G * N)
            catc[bi, SUBLANES:SUBLANES + L, :] = prow(bi, lay.xbc(inner + G * N), G * N)

    BI = range(bt)
    each = lambda keys, f: {k: f(*k) for k in keys}
    gsl = lambda g: slice(g * gw, (g + 1) * gw)
    dt_raw = [prow(bi, lay.dt, LANES) for bi in BI]
    dtc = [_softplus(x + dtbr_ref[...]) for x in dt_raw]
    dtr = [_softplus(_rows_of_transpose(sel_ref[...], x) + dtbc_ref[...]) for x in dt_raw]
    if valid < L:
        dtc = [jnp.where(t_col < valid, x, 0.0) for x in dtc]
        dtr = [jnp.where(s_row < valid, x, 0.0) for x in dtr]
    cs_col = [_dot3(lower, x * (-jnp.exp(alr_ref[...]))) for x in dtc]
    cs_row = [_dot3_right(x * (-jnp.exp(alc_ref[...])), upper) for x in dtr]
    cd3 = [_split3(jnp.concatenate([cs_col[bi], dtc[bi]], axis=0)) for bi in BI]

    problems = [(bi, g) for bi in BI for g in range(G)]
    for w0 in range(0, len(problems), wave):
        keys = problems[w0:w0 + wave]
        hkeys = [(bi, g, jl) for bi, g in keys for jl in range(hpg)]
        if preact:
            xa = each(keys, lambda bi, g: prow(bi, lay.xbc(g * gw), gw))
            Bm = each(keys, lambda bi, g: prow(bi, lay.xbc(inner + g * N), N))
            Cm = each(keys, lambda bi, g: prow(bi, lay.xbc(inner + G * N + g * N), N))
            zg = each(keys, lambda bi, g: prow(bi, lay.z(g * gw), gw))
        else:
            xa = each(keys, lambda bi, g: conv_silu(catx, bi, 0, g * gw, gw))
            Bm = each(keys, lambda bi, g: conv_silu(catb, bi, inner, g * N, N))
            Cm = each(keys, lambda bi, g: conv_silu(catc, bi, inner + G * N, g * N, N))
            zg = each(keys, lambda bi, g: _silu(prow(bi, lay.z(g * gw), gw)))
        ce = each(keys, lambda bi, g: (
            jnp.dot(cd3[bi][0], exp_ref[:, gsl(g)], preferred_element_type=F32)
            + jnp.dot(cd3[bi][1], exp_ref[:, gsl(g)], preferred_element_type=F32)
            + jnp.dot(cd3[bi][2], exp_ref[:, gsl(g)], preferred_element_type=F32)))
        cs_exp = each(keys, lambda bi, g: ce[bi, g][0:L])
        dt_exp = each(keys, lambda bi, g: ce[bi, g][L:2 * L])
        Bmb = each(keys, lambda bi, g: Bm[bi, g].astype(BF16))
        Cmb = each(keys, lambda bi, g: Cm[bi, g].astype(BF16))
        CB = each(keys, lambda bi, g: _dot_nt(Cmb[bi, g], Bmb[bi, g]))
        hs0 = each(keys, lambda bi, g: hs[bi, gsl(g), :])
        yc = each(keys, lambda bi, g: _dot_nt(Cmb[bi, g], hs0[bi, g]) * jnp.exp(cs_exp[bi, g]))
        xab = each(keys, lambda bi, g: xa[bi, g].astype(BF16))
        seg = each(hkeys, lambda bi, g, jl: jnp.where(
            causal, cs_col[bi][:, g * hpg + jl:g * hpg + jl + 1] - cs_row[bi][g * hpg + jl:g * hpg + jl + 1, :],
            -jnp.inf))
        mj = each(hkeys, lambda bi, g, jl: CB[bi, g] * (
            jnp.exp(seg[bi, g, jl]) * dtr[bi][g * hpg + jl:g * hpg + jl + 1, :]))
        yh = each(hkeys, lambda bi, g, jl: _dot(
            mj[bi, g, jl], xab[bi, g][:, (jl // per) * LANES:(jl // per + 1) * LANES]))
        yi = each(keys, lambda bi, g: jnp.concatenate(
            [_select_chain(lane, [yh[bi, g, p * per + u] for u in range(per)], hd)
             for p in range(gw // LANES)], axis=1))
        y = each(keys, lambda bi, g: yc[bi, g] + yi[bi, g] + dsk_ref[:, gsl(g)] * xa[bi, g])
        y = each(keys, lambda bi, g: y[bi, g] * zg[bi, g])
        y = each(keys, lambda bi, g: y[bi, g] * lax.rsqrt(
            jnp.mean(y[bi, g] * y[bi, g], axis=1, keepdims=True) + LN_EPS) * nrm_ref[:, gsl(g)])
        if valid == L:
            for bi, g in keys:
                y_ref[bi, :, gsl(g)] = y[bi, g].astype(y_ref.dtype)
        else:
            assert len(keys) == bt * G
            for g in range(G):
                _store_seqs(y_ref, gsl(g), [y[bi, g] for bi in BI], L, valid)

        xw = each(keys, lambda bi, g: xa[bi, g] * (
            jnp.exp(cs_exp[bi, g][L - 1:L, :] - cs_exp[bi, g]) * dt_exp[bi, g]))
        upd = each(keys, lambda bi, g: _dot_tn(xw[bi, g], Bmb[bi, g]))
        dec = each(hkeys, lambda bi, g, jl: jnp.exp(cs_row[bi][g * hpg + jl:g * hpg + jl + 1, L - 1:L]))
        for bi, g, jl in hkeys:
            r0 = g * gw + jl * hd
            hs[bi, r0:r0 + hd, :] = (dec[bi, g, jl] * hs0[bi, g][jl * hd:(jl + 1) * hd, :]
                                     + upd[bi, g][jl * hd:(jl + 1) * hd, :])

    @pl.when(cidx == nc - 1)
    def _():
        h_ref[0] = hs[...].reshape(h_ref.shape[1:])


def _ssd_scan(P, B, nc, L, valid, bt, j, h0, conv0, prev, prm):
    nst, _, nh, hd, N = h0.shape
    G = prm["groups"]
    hpg = nh // G
    gw = hpg * hd
    inner = nh * hd
    GN = G * N
    W1, cd = conv0.shape[2], conv0.shape[3]
    lead, blk = _row_blocks(B, nc, L, valid, bt)
    assert P.shape[:2] == lead and N == LANES and gw % LANES == 0 and B % bt == 0
    assert nh <= LANES and inner % GN == 0 and cd == inner + 2 * GN
    tmat = _ssd_mats(L)
    sel = _sel_rows(nh)
    wave = bt * G if L <= 2 * SUBLANES else 2
    lay = _SsdLayout(inner, GN)
    assert P.shape[2] == 2 * lay.tile and lay.half % gw == 0 and lay.zhalf % gw == 0
    preact = valid == L
    crows = SUBLANES if preact else SUBLANES + L
    full = lambda a: pl.BlockSpec(a.shape, lambda b, c: (0,) * a.ndim)
    st_h = pl.BlockSpec((1, bt, nh, hd, N), lambda b, c: (j, b, 0, 0, 0))
    n_alias = 0 if prev is None else 1
    consts = [prm["conv_w"], prm["conv_b"], prm["dtb_row"], prm["dtb_col"], prm["alog_row"],
              prm["alog_col"], prm["dskip"], prm["norm"], tmat, prm["expand"], sel]
    in_specs = [
        blk(2 * lay.tile, 0),
        pl.BlockSpec((1, bt, W1, cd), lambda b, c: (j, b, 0, 0)),
        st_h,
    ] + [full(a) for a in consts] + [pl.BlockSpec(memory_space=pl.ANY)] * n_alias
    out_specs = [blk(inner, 0), st_h]
    out_shape = [jax.ShapeDtypeStruct(lead + (inner,), _mix_dtype(L, valid)),
                 jax.ShapeDtypeStruct(h0.shape, F32)]
    n_in = len(in_specs) - n_alias
    args = [P, conv0, h0] + consts + ([prev] if prev is not None else [])
    return pl.pallas_call(
        functools.partial(_ssd_kernel, L, valid, nc, bt, G, hpg, hd, wave, preact, n_alias),
        grid=(B // bt, nc),
        in_specs=in_specs,
        out_specs=out_specs,
        out_shape=out_shape,
        input_output_aliases={n_in: 1} if n_alias else {},
        scratch_shapes=[pltpu.VMEM((bt, crows, inner), F32), pltpu.VMEM((bt, crows, GN), F32),
                        pltpu.VMEM((bt, crows, GN), F32), pltpu.VMEM((bt, inner, N), F32)],
        compiler_params=pltpu.CompilerParams(
            dimension_semantics=("parallel", "arbitrary"), vmem_limit_bytes=VMEM_LIMIT),
        name="ssd_scan",
    )(*args)


def _prep_ab(j, H, dk, dv, w_in, ig_bias, fg_bias, ml_norm, wa2, ba, gla_norm, w_out):
    rank = wa2.shape[1]
    qk, vv = H * dk, H * dv
    o = np.cumsum([0, qk, qk, vv, vv, H, H, qk, qk, vv, vv, rank])
    assert int(o[-1]) == w_in.shape[2]
    wj = w_in[j]
    small = jnp.concatenate([wj[:, o[4]:o[6]], wj[:, o[10]:o[11]]], axis=1)
    small = jnp.pad(small, ((0, 0), (0, LANES - small.shape[1])))
    w = jnp.concatenate([wj[:, :o[4]], wj[:, o[6]:o[10]], small], axis=1).astype(BF16)
    wa2p = jnp.zeros((LANES, qk), F32).at[2 * H:2 * H + rank, :].set(wa2[j]).astype(BF16)
    return {
        "w_in": w,
        "gate_bias": jnp.concatenate([ig_bias[j], fg_bias[j]]).astype(F32).reshape(2 * H, 1),
        "ml_norm": ml_norm[j].reshape(1, vv),
        "gla_norm": gla_norm[j].reshape(1, vv),
        "wa2": wa2p,
        "ba": ba[j].reshape(1, qk),
        "w_out_ml": w_out[j][:vv].astype(BF16),
        "w_out_gla": w_out[j][vv:].astype(BF16),
    }


def _prep_ssd(j, G, nh, hd, N, w_in, conv_w, conv_b, dt_bias, a_log, d_skip, norm_g, w_out):
    inner = nh * hd
    hpg = nh // G
    cd = inner + 2 * G * N
    wz, wxbc, wdt = w_in[j][:, :inner], w_in[j][:, inner:inner + cd], w_in[j][:, inner + cd:]
    lay = _SsdLayout(inner, G * N)
    wdt = jnp.pad(wdt, ((0, 0), (0, LANES - nh)))
    w = jnp.concatenate([wxbc[:, :lay.half], wz[:, :lay.zhalf], wdt,
                         wxbc[:, lay.half:], wz[:, lay.zhalf:], jnp.zeros_like(wdt)], axis=1).astype(BF16)
    lane_form = lambda v: jnp.pad(v, (0, LANES - nh)).reshape(1, LANES)
    e = np.zeros((LANES, inner), np.float32)
    for r in range(nh):
        e[r, r * hd:(r + 1) * hd] = 1.0
    return {
        "groups": G,
        "w_in": w,
        "conv_w": conv_w[j], "conv_b": conv_b[j].reshape(1, cd),
        "dtb_row": lane_form(dt_bias[j]), "dtb_col": dt_bias[j].reshape(nh, 1),
        "alog_row": lane_form(a_log[j]), "alog_col": a_log[j].reshape(nh, 1),
        "dskip": jnp.repeat(d_skip[j], hd).reshape(1, inner),
        "norm": norm_g[j].reshape(1, inner),
        "expand": jnp.asarray(e, dtype=BF16),
        "w_out": w_out[j].astype(BF16),
    }


def _largest_divisor(n, cap, step=1):
    return max(d for d in range(step, cap + 1, step) if n % d == 0)


def _plan(B, T, chunk):
    if T % chunk == 0:
        return chunk, T // chunk, _largest_divisor(B, 4), 1
    L = max(SUBLANES, 1 << int(math.ceil(math.log2(T))))
    per = L // T
    assert L % T == 0 and B % per == 0
    return L, 1, _largest_divisor(B, 8, per), _largest_divisor(B, 4, per)


def _trunk(x, states, ab_prm, ssd_prm, mlp, lns, alpha, chunk):
    B, T, D = x.shape
    mC, mn, mm, gS, sh, sconv = states
    n_ab, _, H, dk, dv = mC.shape
    L, nc, bt_ab, bt_ssd = _plan(B, T, chunk)
    valid = T if nc == 1 else L
    M = B * T
    tm = 512 if M % 512 == 0 else M
    tmm = 1024 if M % 1024 == 0 else tm
    X = x.reshape(M, D)
    ab_states = (mC, mn.reshape(n_ab, B, 1, H * dk), mm.reshape(n_ab, B, 1, H), gS)
    ab_out, h_out, ncv = None, None, []
    depth = mlp[0].shape[0]
    for l in range(depth):
        j = l // 2
        if l % 2 == 0:
            p = ab_prm[j]
            P = _proj(X, p["w_in"], tm, p["w_in"].shape[1])
            lead = _row_blocks(B, nc, L, valid, bt_ab)[0]
            res = _ab_scan(P.reshape(lead + (-1,)), B, nc, L, valid, bt_ab, j, ab_states, ab_out, p)
            ab_out = res[2:]
            ys = [res[0].reshape(M, -1), res[1].reshape(M, -1)]
            ws = [p["w_out_ml"], p["w_out_gla"]]
        else:
            p = ssd_prm[j]
            inner = p["norm"].shape[1]
            cd = p["conv_b"].shape[1]
            lay = _SsdLayout(inner, (cd - inner) // 2)
            W1 = sconv.shape[2]
            assert T >= W1
            if valid == L:
                tmc = max(t for t in (128, 256, 512) if T % t == 0)
                P, tail = _proj_conv(X, p["w_in"], p["conv_w"], p["conv_b"], sconv, j, B, tmc, lay)
                ncv.append(tail[:, SUBLANES - W1:])
            else:
                P = _proj(X, p["w_in"], tm, lay.tile)
                last = P.reshape(B, T, -1)[:, T - W1:]
                ncv.append(jnp.concatenate([last[:, :, :lay.half], last[:, :, lay.tile:lay.tile + lay.half]], axis=2))
            lead = _row_blocks(B, nc, L, valid, bt_ssd)[0]
            y, h_out = _ssd_scan(P.reshape(lead + (-1,)), B, nc, L, valid, bt_ssd, j, sh, sconv, h_out, p)
            ys = [y.reshape(M, -1)]
            ws = [p["w_out"]]
        X = _outproj_ln(ys, ws, X, lns[0], lns[1], l, alpha, tm)
        X = _mlp_ln(X, mlp[0], mlp[1], lns[2], lns[3], l, alpha, tmm, 1024)
    nC, nn_, nm, nS = ab_out
    return (X.reshape(B, T, D), nC, nn_.reshape(mn.shape), nm.reshape(mm.shape), nS, h_out, jnp.stack(ncv))


def kernel(x_prompt, x_sample, state_mlstm_C, state_mlstm_n, state_mlstm_m, state_gla_S, state_ssd_h,
           state_ssd_conv, ab_w_in, ab_ig_bias, ab_fg_bias, ab_ml_norm, ab_gla_wa2, ab_gla_ba, ab_gla_norm,
           ab_w_out, ssd_w_in, ssd_conv_w, ssd_conv_b, ssd_dt_bias, ssd_a_log, ssd_d, ssd_norm, ssd_w_out,
           mlp_w1, mlp_w2, ln_mix_g, ln_mix_b, ln_mlp_g, ln_mlp_b):
    depth = mlp_w1.shape[0]
    D = x_prompt.shape[2]
    alpha = (2 * depth) ** 0.25
    n_ab, _, H, dk, dv = state_mlstm_C.shape
    n_ssd, _, nh, hd, N = state_ssd_h.shape
    cd = state_ssd_conv.shape[3]
    G = (cd - nh * hd) // (2 * N)
    ab_prm = [_prep_ab(j, H, dk, dv, ab_w_in, ab_ig_bias, ab_fg_bias, ab_ml_norm, ab_gla_wa2,
                       ab_gla_ba, ab_gla_norm, ab_w_out) for j in range(n_ab)]
    ssd_prm = [_prep_ssd(j, G, nh, hd, N, ssd_w_in, ssd_conv_w, ssd_conv_b, ssd_dt_bias, ssd_a_log,
                         ssd_d, ssd_norm, ssd_w_out) for j in range(n_ssd)]
    mlp = (mlp_w1, mlp_w2)
    lns = tuple(a.reshape(depth, 1, D) for a in (ln_mix_g, ln_mix_b, ln_mlp_g, ln_mlp_b))

    Bp = x_prompt.shape[0]
    zeros = (jnp.zeros((n_ab, Bp, H, dk, dv), F32), jnp.zeros((n_ab, Bp, H, dk), F32),
             jnp.zeros((n_ab, Bp, H), F32), jnp.zeros((n_ab, Bp, H, dk, dv), F32),
             jnp.zeros((n_ssd, Bp, nh, hd, N), F32), jnp.zeros((n_ssd, Bp) + state_ssd_conv.shape[2:], F32))
    carried = (state_mlstm_C, state_mlstm_n, state_mlstm_m, state_gla_S, state_ssd_h, state_ssd_conv)
    chunk = 128
    yp = _trunk(x_prompt, zeros, ab_prm, ssd_prm, mlp, lns, alpha, chunk)
    ys = _trunk(x_sample, carried, ab_prm, ssd_prm, mlp, lns, alpha, chunk)
    return (yp[0], ys[0]) + yp[1:] + ys[1:]
```

```python
import functools
import itertools
import math

import jax
import jax.numpy as jnp
import numpy as np
from jax import lax
from jax.experimental import pallas as pl
from jax.experimental.pallas import tpu as pltpu

F32 = jnp.float32
BF16 = jnp.bfloat16

LN_EPS = 1e-5
GLA_TAU = 16.0
LANES = 128
SUBLANES = 8
VMEM_LIMIT = 48 * 1024 * 1024


def _dot(a, b):
    return jnp.dot(a.astype(BF16), b.astype(BF16), preferred_element_type=F32)


def _dot_nt(a, b):
    return lax.dot_general(a.astype(BF16), b.astype(BF16), (((1,), (1,)), ((), ())),
                           preferred_element_type=F32)


def _dot_tn(a, b):
    return lax.dot_general(a.astype(BF16), b.astype(BF16), (((0,), (0,)), ((), ())),
                           preferred_element_type=F32)


def _split3(x):
    hi = x.astype(BF16)
    r = x - hi.astype(F32)
    mid = r.astype(BF16)
    lo = (r - mid.astype(F32)).astype(BF16)
    return hi, mid, lo


def _dot3(t, x):
    hi, mid, lo = _split3(x)
    f = lambda p: jnp.dot(t, p, preferred_element_type=F32)
    return f(hi) + f(mid) + f(lo)


def _dot3_right(x, t):
    hi, mid, lo = _split3(x)
    f = lambda p: jnp.dot(p, t, preferred_element_type=F32)
    return f(hi) + f(mid) + f(lo)


def _rows_of_transpose(sel, x):
    hi, mid, lo = _split3(x)
    f = lambda p: lax.dot_general(sel, p, (((1,), (1,)), ((), ())), preferred_element_type=F32)
    return f(hi) + f(mid) + f(lo)


def _softplus(x):
    e = jnp.exp(-jnp.abs(x))
    u = 1.0 + e
    d = u - 1.0
    return jnp.maximum(x, 0.0) + jnp.where(d == 0.0, e, jnp.log(u) * (e / d))


def _log_sigmoid(x):
    return -_softplus(-x)


def _sigmoid(x):
    return 0.5 + 0.5 * jnp.tanh(0.5 * x)


def _silu(x):
    h = 0.5 * x
    return h + h * jnp.tanh(h)


def _seq_tile(ref, bi, cols, L, T):
    if T == L:
        return ref[bi, :, cols]
    r0 = bi * T
    a = (r0 // L) * L
    tile = ref[0, a:a + L, cols]
    return tile if r0 == a else pltpu.roll(tile, L - (r0 - a), axis=0)


def _store_seqs(ref, cols, vals, L, T, first=0):
    if T == L:
        for i, v in enumerate(vals):
            ref[first + i, :, cols] = v.astype(ref.dtype)
        return
    per = L // T
    assert first % per == 0 and len(vals) % per == 0
    row = lax.broadcasted_iota(jnp.int32, vals[0].shape, 0)
    for a in range(len(vals) // per):
        out = vals[a * per]
        for k in range(1, per):
            out = jnp.where(row < k * T, out, pltpu.roll(vals[a * per + k], k * T, axis=0))
        r0 = (first // per + a) * L
        ref[0, r0:r0 + L, cols] = out


def _interleave(*gens):
    gens = list(gens)
    while gens:
        for g in list(gens):
            try:
                next(g)
            except StopIteration:
                gens.remove(g)


def _layernorm_rows(r, g, b):
    mu = jnp.mean(r, axis=1, keepdims=True)
    d = r - mu
    var = jnp.mean(d * d, axis=1, keepdims=True)
    return d * lax.rsqrt(var + LN_EPS) * g + b


def _headnorm(h, g):
    mu = jnp.mean(h, axis=1, keepdims=True)
    d = h - mu
    var = jnp.mean(d * d, axis=1, keepdims=True)
    return d * lax.rsqrt(var + LN_EPS) * g


def _select_chain(idx, pieces, width):
    out = pieces[0]
    for u in range(1, len(pieces)):
        out = jnp.where(idx < u * width, out, pieces[u])
    return out


def _cumsum_mats(L):
    t = np.arange(L)[:, None]
    j = np.arange(L)[None, :]
    upper = (t <= j)
    lower = (j <= t)
    return upper, lower


def _ab_mats(L):
    nlev = int(round(math.log2(L)))
    assert 1 << nlev == L
    upper, lower = _cumsum_mats(L)
    t = np.arange(L)[:, None]
    j = np.arange(L)[None, :]
    mats = [upper, lower]
    for i in range(nlev):
        n = L >> (i + 1)
        mid = (t // (2 * n)) * (2 * n) + n - 1
        second = (t % (2 * n)) >= n
        m = np.where(second, (j > mid) & (j <= t), (j > t) & (j <= mid))
        mats.append(m)
    return jnp.asarray(np.concatenate(mats, axis=0).astype(np.float32), dtype=BF16), nlev


def _ssd_mats(L):
    upper, lower = _cumsum_mats(L)
    return jnp.asarray(np.concatenate([upper, lower], axis=0).astype(np.float32), dtype=BF16)


def _proj_kernel(x_ref, w_ref, o_ref):
    o_ref[...] = jnp.dot(x_ref[...].astype(BF16), w_ref[...], preferred_element_type=F32)


def _proj(x, w, tm, tn):
    M, K = x.shape
    N = w.shape[1]
    assert M % tm == 0 and N % tn == 0
    return pl.pallas_call(
        _proj_kernel,
        grid=(N // tn, M // tm),
        in_specs=[pl.BlockSpec((tm, K), lambda j, i: (i, 0)),
                  pl.BlockSpec((K, tn), lambda j, i: (0, j))],
        out_specs=pl.BlockSpec((tm, tn), lambda j, i: (i, j)),
        out_shape=jax.ShapeDtypeStruct((M, N), F32),
        compiler_params=pltpu.CompilerParams(
            dimension_semantics=("parallel", "parallel"), vmem_limit_bytes=VMEM_LIMIT),
        name="proj",
    )(x, w)


class _SsdLayout:
    def __init__(self, inner, gn):
        self.cd = inner + 2 * gn
        self.half = self.cd // 2
        self.zhalf = inner // 2
        self.tile = self.half + self.zhalf + LANES
        self.dt = self.half + self.zhalf
        assert self.cd % (2 * LANES) == 0 and inner % (2 * LANES) == 0

    def xbc(self, c):
        return c if c < self.half else self.tile + c - self.half

    def z(self, c):
        return self.half + c if c < self.zhalf else self.tile + self.half + c - self.zhalf


def _proj_conv_kernel(tps, nrb, half, zhalf, x_ref, w_ref, cw_ref, cb_ref, cv_ref, o_ref, tail_ref, hist):
    i = pl.program_id(1)
    W = cw_ref.shape[0]
    P0 = SUBLANES - (W - 1)
    rb = x_ref.shape[0] // nrb

    @pl.when(i % tps == 0)
    def _():
        hist[...] = jnp.zeros_like(hist)
        hist[P0:SUBLANES, :] = cv_ref[0, 0]

    def raw_rows(r):
        return jnp.dot(x_ref[r * rb:(r + 1) * rb, :].astype(BF16), w_ref[...], preferred_element_type=F32)

    def finish(r, raw, prev):
        rows = slice(r * rb, (r + 1) * rb)
        xall = jnp.concatenate([prev, raw[:, :half]], axis=0)
        acc = cb_ref[...] + xall[SUBLANES:] * cw_ref[W - 1:W, :]
        for w in range(W - 1):
            acc = acc + pltpu.roll(xall, W - 1 - w, axis=0)[SUBLANES:] * cw_ref[w:w + 1, :]
        o_ref[rows, :half] = _silu(acc)
        o_ref[rows, half:half + zhalf] = _silu(raw[:, half:half + zhalf])
        o_ref[rows, half + zhalf:] = raw[:, half + zhalf:]
        return raw[rb - SUBLANES:, :half]

    prev = hist[...]
    raw = raw_rows(0)
    for r in range(nrb):
        nxt = raw_rows(r + 1) if r + 1 < nrb else None
        prev = finish(r, raw, prev)
        raw = nxt
    hist[...] = prev
    tail_ref[0] = prev


def _proj_conv(x, w, conv_w, conv_b, conv0, j, B, tm, lay):
    M, K = x.shape
    T = M // B
    assert M % tm == 0 and T % tm == 0 and w.shape[1] == 2 * lay.tile and (tm // 4) % SUBLANES == 0
    tps = T // tm
    W1 = conv0.shape[2]
    return pl.pallas_call(
        functools.partial(_proj_conv_kernel, tps, 4, lay.half, lay.zhalf),
        grid=(2, M // tm),
        in_specs=[pl.BlockSpec((tm, K), lambda c, i: (i, 0)),
                  pl.BlockSpec((K, lay.tile), lambda c, i: (0, c)),
                  pl.BlockSpec((W1 + 1, lay.half), lambda c, i: (0, c)),
                  pl.BlockSpec((1, lay.half), lambda c, i: (0, c)),
                  pl.BlockSpec((1, 1, W1, lay.half), lambda c, i: (j, i // tps, 0, c))],
        out_specs=[pl.BlockSpec((tm, lay.tile), lambda c, i: (i, c)),
                   pl.BlockSpec((1, SUBLANES, lay.half), lambda c, i: (i // tps, 0, c))],
        out_shape=[jax.ShapeDtypeStruct((M, 2 * lay.tile), F32),
                   jax.ShapeDtypeStruct((B, SUBLANES, lay.cd), F32)],
        scratch_shapes=[pltpu.VMEM((SUBLANES, lay.half), F32)],
        compiler_params=pltpu.CompilerParams(
            dimension_semantics=("parallel", "arbitrary"), vmem_limit_bytes=VMEM_LIMIT),
        name="proj_conv",
    )(x, w, conv_w, conv_b, conv0)


def _outproj_ln_kernel(alpha, n_in, *refs):
    ys = refs[:n_in]
    ws = refs[n_in:2 * n_in]
    x_ref, g_ref, b_ref, o_ref = refs[2 * n_in:]
    tm = x_ref.shape[0]
    nrb = 4 if tm % (4 * SUBLANES) == 0 else 1
    rb = tm // nrb

    def mix(k):
        rows = slice(k * rb, (k + 1) * rb)
        r = alpha * x_ref[rows, :]
        for y_ref, w_ref in zip(ys, ws):
            r = r + jnp.dot(y_ref[rows, :].astype(BF16), w_ref[...], preferred_element_type=F32)
        return r

    r = mix(0)
    for k in range(nrb):
        nxt = mix(k + 1) if k + 1 < nrb else None
        o_ref[k * rb:(k + 1) * rb, :] = _layernorm_rows(r, g_ref[0], b_ref[0])
        r = nxt


def _outproj_ln(ys, ws, x, g, b, l, alpha, tm):
    M, D = x.shape
    assert M % tm == 0
    n_in = len(ys)
    in_specs = ([pl.BlockSpec((tm, y.shape[1]), lambda i: (i, 0)) for y in ys]
                + [pl.BlockSpec(w.shape, lambda i: (0, 0)) for w in ws]
                + [pl.BlockSpec((tm, D), lambda i: (i, 0)),
                   pl.BlockSpec((1, 1, D), lambda i: (l, 0, 0)),
                   pl.BlockSpec((1, 1, D), lambda i: (l, 0, 0))])
    return pl.pallas_call(
        functools.partial(_outproj_ln_kernel, alpha, n_in),
        grid=(M // tm,),
        in_specs=in_specs,
        out_specs=pl.BlockSpec((tm, D), lambda i: (i, 0)),
        out_shape=jax.ShapeDtypeStruct((M, D), F32),
        compiler_params=pltpu.CompilerParams(
            dimension_semantics=("parallel",), vmem_limit_bytes=VMEM_LIMIT),
        name="outproj_ln",
    )(*ys, *ws, x, g, b)


def _mlp_kernel(alpha, nf, x_ref, w1_ref, w2_ref, g_ref, b_ref, o_ref, acc_ref):
    f = pl.program_id(1)

    @pl.when(f == 0)
    def _():
        acc_ref[...] = jnp.zeros_like(acc_ref)

    h = jnp.dot(x_ref[...].astype(BF16), w1_ref[0].astype(BF16), preferred_element_type=F32)
    h = jnp.square(jnp.maximum(h, 0.0))
    acc_ref[...] += jnp.dot(h.astype(BF16), w2_ref[0].astype(BF16), preferred_element_type=F32)

    @pl.when(f == nf - 1)
    def _():
        r = alpha * x_ref[...] + acc_ref[...]
        o_ref[...] = _layernorm_rows(r, g_ref[0], b_ref[0])


def _mlp_ln(x, w1, w2, g, b, l, alpha, tm, tf):
    M, D = x.shape
    Fdim = w1.shape[2]
    assert M % tm == 0 and Fdim % tf == 0
    nf = Fdim // tf
    return pl.pallas_call(
        functools.partial(_mlp_kernel, alpha, nf),
        grid=(M // tm, nf),
        in_specs=[pl.BlockSpec((tm, D), lambda i, f: (i, 0)),
                  pl.BlockSpec((1, D, tf), lambda i, f: (l, 0, f)),
                  pl.BlockSpec((1, tf, D), lambda i, f: (l, f, 0)),
                  pl.BlockSpec((1, 1, D), lambda i, f: (l, 0, 0)),
                  pl.BlockSpec((1, 1, D), lambda i, f: (l, 0, 0))],
        out_specs=pl.BlockSpec((tm, D), lambda i, f: (i, 0)),
        out_shape=jax.ShapeDtypeStruct((M, D), F32),
        scratch_shapes=[pltpu.VMEM((tm, D), F32)],
        compiler_params=pltpu.CompilerParams(
            dimension_semantics=("parallel", "arbitrary"), vmem_limit_bytes=VMEM_LIMIT),
        name="mlp_ln",
    )(x, w1, w2, g, b)


def _ab_kernel(L, valid, nlev, nc, bt, H, dk, dv, n_alias,
               qk_ref, v_ref, mo_ref, gqk_ref, gv_ref, gg_ref, sm_ref,
               c0_ref, n0_ref, m0_ref, s0_ref, gb_ref, mln_ref, wa2_ref, ba_ref, gln_ref, tmat_ref,
               sel_ref, *rest):
    ml_ref, gla_ref, c_ref, n_ref, m_ref, s_ref, cs, ns, ms, ss = rest[n_alias:]
    c = pl.program_id(1)
    per = LANES // dk
    HK = H * dk

    @pl.when(c == 0)
    def _():
        cs[...] = c0_ref[0].reshape(bt, HK, dv)
        ns[...] = n0_ref[0]
        ms[...] = m0_ref[0]
        ss[...] = s0_ref[0].reshape(bt, HK, dv)

    t_col = lax.broadcasted_iota(jnp.int32, (L, 1), 0)
    s_row = lax.broadcasted_iota(jnp.int32, (1, L), 1)
    tt = lax.broadcasted_iota(jnp.int32, (L, L), 0)
    sc = lax.broadcasted_iota(jnp.int32, (L, L), 1)
    causal = sc <= tt
    eye = sc == tt
    lane = lax.broadcasted_iota(jnp.int32, (L, LANES), 1)
    lane1 = lax.broadcasted_iota(jnp.int32, (1, LANES), 1)
    laneh = lax.broadcasted_iota(jnp.int32, (1, H), 1)
    rowp = lax.broadcasted_iota(jnp.int32, (LANES, dv), 0)
    ek = lax.broadcasted_iota(jnp.int32, (LANES, LANES), 0) == lax.broadcasted_iota(jnp.int32, (LANES, LANES), 1)
    inhead = [(lane >= u * dk) & (lane < (u + 1) * dk) for u in range(per)]
    upper = tmat_ref[0:L, :]
    scale = dk ** -0.5

    def to_col(row):
        return jnp.sum(jnp.where(eye, row, 0.0), axis=1, keepdims=True)

    psl = lambda p: slice(p * LANES, (p + 1) * LANES)
    hsl = lambda h: slice(h * dv, (h + 1) * dv)
    each = lambda keys, f: {k: f(*k) for k in keys}
    neg_inf = -jnp.inf
    allc = slice(None)
    tile = lambda ref, bi, cols: _seq_tile(ref, bi, cols, L, valid)

    def store(ref, vals, BI):
        for h in range(H):
            _store_seqs(ref, hsl(h), [vals[bi, h] for bi in BI], L, valid, BI[0])

    def mlstm(BI):
        pairs = [(bi, p) for bi in BI for p in range(H // per)]
        heads = [(bi, h) for bi in BI for h in range(H)]
        seqs = lambda f: {bi: f(bi) for bi in BI}
        ig8 = seqs(lambda bi: _rows_of_transpose(sel_ref[...], tile(sm_ref, bi, allc)) + gb_ref[...])
        yield
        lf8 = seqs(lambda bi: _log_sigmoid(ig8[bi]))
        yield
        if valid < L:
            ok = s_row < valid
            ig8 = seqs(lambda bi: jnp.where(ok, ig8[bi], neg_inf))
            lf8 = seqs(lambda bi: jnp.where(ok, lf8[bi], 0.0))
        b8 = seqs(lambda bi: _dot3_right(lf8[bi], upper))
        yield
        m_prev = seqs(lambda bi: ms[bi])
        qk = seqs(lambda bi: tile(qk_ref, bi, allc))
        yield
        Qp = each(pairs, lambda bi, p: qk[bi][:, psl(p)])
        Kp = each(pairs, lambda bi, p: qk[bi][:, HK + p * LANES:HK + (p + 1) * LANES] * scale)
        yield
        Kpb = each(pairs, lambda bi, p: Kp[bi, p].astype(BF16))
        C0p = each(pairs, lambda bi, p: cs[bi, psl(p), :])
        n0p = each(pairs, lambda bi, p: ns[bi][:, psl(p)])
        yield
        ig_row = each(heads, lambda bi, h: ig8[bi][h:h + 1, :])
        b_row = each(heads, lambda bi, h: b8[bi][H + h:H + h + 1, :])
        b_col = each(heads, lambda bi, h: to_col(b_row[bi, h]))
        yield
        ig_col = each(heads, lambda bi, h: to_col(ig_row[bi, h]))
        yield
        D = each(heads, lambda bi, h: jnp.where(causal, b_col[bi, h] - b_row[bi, h] + ig_row[bi, h], neg_inf))
        yield
        g_col = each(heads, lambda bi, h: b_col[bi, h] + m_prev[bi][:, h:h + 1])
        m_col = each(heads, lambda bi, h: jnp.maximum(g_col[bi, h], jnp.max(D[bi, h], axis=1, keepdims=True)))
        yield
        w_intra = each(heads, lambda bi, h: jnp.exp(D[bi, h] - m_col[bi, h]))
        yield
        w_inter = each(heads, lambda bi, h: jnp.exp(g_col[bi, h] - m_col[bi, h]))
        Qh = each(heads, lambda bi, h: jnp.where(inhead[h % per], Qp[bi, h // per], 0.0))
        yield
        Qhb = each(heads, lambda bi, h: Qh[bi, h].astype(BF16))
        vb = each(heads, lambda bi, h: tile(v_ref, bi, hsl(h)).astype(BF16))
        yield
        s = each(heads, lambda bi, h: _dot_nt(Qhb[bi, h], Kpb[bi, h // per]) * w_intra[bi, h])
        yield
        qc = each(heads, lambda bi, h: _dot(Qhb[bi, h], C0p[bi, h // per]))
        yield
        num = each(heads, lambda bi, h: _dot(s[bi, h], vb[bi, h]) + w_inter[bi, h] * qc[bi, h])
        yield
        den = each(heads, lambda bi, h: (jnp.sum(s[bi, h], axis=1, keepdims=True) + w_inter[bi, h]
                                         * jnp.sum(Qh[bi, h] * n0p[bi, h // per], axis=1, keepdims=True)))
        yield
        hh = each(heads, lambda bi, h: num[bi, h] / jnp.maximum(jnp.abs(den[bi, h]), jnp.exp(-m_col[bi, h])))
        yield
        ml = each(heads, lambda bi, h: (_headnorm(hh[bi, h], mln_ref[:, hsl(h)])
                                        * _sigmoid(tile(mo_ref, bi, hsl(h)))))
        yield
        store(ml_ref, ml, BI)
        yield
        mL = each(heads, lambda bi, h: m_col[bi, h][L - 1:L, :])
        wL_col = each(heads, lambda bi, h: jnp.exp(b_col[bi, h][L - 1:L, :] - b_col[bi, h] + ig_col[bi, h] - mL[bi, h]))
        yield
        wL0 = each(heads, lambda bi, h: jnp.exp(g_col[bi, h][L - 1:L, :] - mL[bi, h]))
        kw = each(heads, lambda bi, h: Kp[bi, h // per] * wL_col[bi, h])
        yield
        c_new = each(heads, lambda bi, h: wL0[bi, h] * C0p[bi, h // per] + _dot_tn(kw[bi, h], vb[bi, h]))
        yield
        n_new = each(heads, lambda bi, h: wL0[bi, h] * n0p[bi, h // per] + jnp.sum(kw[bi, h], axis=0, keepdims=True))
        yield
        for bi, p in pairs:
            cs[bi, psl(p), :] = _select_chain(rowp, [c_new[bi, p * per + u] for u in range(per)], dk)
            ns[bi, :, psl(p)] = _select_chain(lane1, [n_new[bi, p * per + u] for u in range(per)], dk)
        yield
        for bi in BI:
            m_new = m_prev[bi]
            for h in range(H):
                m_new = jnp.where(laneh == h, mL[bi, h], m_new)
            ms[bi] = m_new
        yield

    def gla(BI):
        pairs = [(bi, p) for bi in BI for p in range(H // per)]
        heads = [(bi, h) for bi in BI for h in range(H)]
        seqs = lambda f: {bi: f(bi) for bi in BI}
        gqk = seqs(lambda bi: tile(gqk_ref, bi, allc))
        yield
        Q2 = seqs(lambda bi: gqk[bi][:, :HK] * scale)
        K2 = seqs(lambda bi: gqk[bi][:, HK:])
        if valid < L:
            K2 = seqs(lambda bi: jnp.where(t_col < valid, K2[bi], 0.0))
        yield
        la = seqs(lambda bi: _log_sigmoid(_dot(tile(sm_ref, bi, allc), wa2_ref[...]) + ba_ref[...])
                  * (1.0 / GLA_TAU))
        if valid < L:
            la = seqs(lambda bi: jnp.where(t_col < valid, la[bi], 0.0))
        yield
        TL = seqs(lambda bi: _dot3(tmat_ref[L:(2 + nlev) * L, :], la[bi]))
        yield
        Q2b = seqs(lambda bi: Q2[bi].astype(BF16))
        K2b = seqs(lambda bi: K2[bi].astype(BF16))
        yield
        scores = each(heads, lambda bi, h: jnp.where(
            eye, _dot_nt(jnp.where(inhead[h % per], Q2b[bi][:, psl(h // per)], 0), K2b[bi][:, psl(h // per)]), 0.0))
        yield
        for i in range(nlev):
            n = L >> (i + 1)
            second = (t_col & n) != 0
            En = seqs(lambda bi: jnp.exp(TL[bi][(1 + i) * L:(2 + i) * L]))
            yield
            X = seqs(lambda bi: (jnp.where(second, Q2[bi], K2[bi]) * En[bi]).astype(BF16))
            yield
            Xk = seqs(lambda bi: jnp.where(second, 0, X[bi]))
            yield
            sn = each(heads, lambda bi, h: _dot_nt(
                jnp.where(second & inhead[h % per], X[bi][:, psl(h // per)], 0), Xk[bi][:, psl(h // per)]))
            yield
            if i > 0:
                sh = int(round(math.log2(2 * n)))
                same = (tt >> sh) == (sc >> sh)
                sn = each(heads, lambda bi, h: jnp.where(same, sn[bi, h], 0.0))
            scores = each(heads, lambda bi, h: scores[bi, h] + sn[bi, h])
            yield
        A = seqs(lambda bi: TL[bi][0:L])
        AL = seqs(lambda bi: A[bi][L - 1:L, :])
        QA = seqs(lambda bi: (Q2[bi] * jnp.exp(A[bi])).astype(BF16))
        yield
        kd = seqs(lambda bi: (K2[bi] * jnp.exp(AL[bi] - A[bi])).astype(BF16))
        eAL = seqs(lambda bi: jnp.exp(AL[bi]))
        yield
        S0p = each(pairs, lambda bi, p: ss[bi, psl(p), :])
        dec_col = each(pairs, lambda bi, p: jnp.sum(jnp.where(ek, eAL[bi][:, psl(p)], 0.0), axis=1, keepdims=True))
        yield
        v2b = each(heads, lambda bi, h: tile(gv_ref, bi, hsl(h)).astype(BF16))
        qs = each(heads, lambda bi, h: _dot(jnp.where(inhead[h % per], QA[bi][:, psl(h // per)], 0), S0p[bi, h // per]))
        yield
        o = each(heads, lambda bi, h: _dot(scores[bi, h], v2b[bi, h]) + qs[bi, h])
        yield
        gla_out = each(heads, lambda bi, h: (_headnorm(o[bi, h], gln_ref[:, hsl(h)])
                                             * _silu(tile(gg_ref, bi, hsl(h)))))
        yield
        store(gla_ref, gla_out, BI)
        yield
        upd = each(heads, lambda bi, h: _dot_tn(kd[bi][:, psl(h // per)], v2b[bi, h]))
        yield
        for bi, p in pairs:
            ss[bi, psl(p), :] = (dec_col[bi, p] * S0p[bi, p]
                                 + _select_chain(rowp, [upd[bi, p * per + u] for u in range(per)], dk))
        yield

    per_tile = L // valid
    if bt % 2 == 0 and (bt // 2) % per_tile == 0:
        ga, gb = range(bt // 2), range(bt // 2, bt)
        _interleave(itertools.chain(mlstm(ga), gla(ga)), itertools.chain(gla(gb), mlstm(gb)))
    else:
        _interleave(mlstm(range(bt)), gla(range(bt)))

    @pl.when(c == nc - 1)
    def _():
        c_ref[0] = cs[...].reshape(bt, H, dk, dv)
        n_ref[0] = ns[...]
        m_ref[0] = ms[...]
        s_ref[0] = ss[...].reshape(bt, H, dk, dv)


def _sel_rows(r):
    return jnp.asarray(np.eye(r, LANES, dtype=np.float32), dtype=BF16)


def _mix_dtype(L, valid):
    return BF16 if valid == L else F32


def _row_blocks(B, nc, L, valid, bt):
    if valid == L:
        return (B, nc * L), lambda w, k: pl.BlockSpec((bt, L, w), lambda b, c: (b, c, k))
    assert nc == 1 and L % valid == 0 and (bt * valid) % L == 0
    return (B // bt, bt * valid), lambda w, k: pl.BlockSpec((1, bt * valid, w), lambda b, c: (b, 0, k))


def _ab_scan(P, B, nc, L, valid, bt, j, states, prev, prm):
    C0, n0, m0, S0 = states
    nst, _, H, dk, dv = C0.shape
    HK = H * dk
    lead, blk = _row_blocks(B, nc, L, valid, bt)
    assert P.shape[:2] == lead and dv == LANES and LANES % dk == 0 and B % bt == 0
    tmat, nlev = _ab_mats(L)
    wv = H * dv
    assert 2 * HK == wv
    nsm = 6 * wv // LANES
    sel = _sel_rows(2 * H)
    sec = lambda k: blk(wv, k)
    st_c = pl.BlockSpec((1, bt, H, dk, dv), lambda b, c: (j, b, 0, 0, 0))
    st_n = pl.BlockSpec((1, bt, 1, HK), lambda b, c: (j, b, 0, 0))
    st_m = pl.BlockSpec((1, bt, 1, H), lambda b, c: (j, b, 0, 0))
    full = lambda a: pl.BlockSpec(a.shape, lambda b, c: (0,) * a.ndim)
    n_alias = 0 if prev is None else 4
    consts = [prm["gate_bias"], prm["ml_norm"], prm["wa2"], prm["ba"], prm["gla_norm"], tmat, sel]
    in_specs = ([sec(k) for k in range(6)]
                + [blk(LANES, nsm), st_c, st_n, st_m, st_c]
                + [full(a) for a in consts]
                + [pl.BlockSpec(memory_space=pl.ANY)] * n_alias)
    out_specs = [sec(0), sec(0), st_c, st_n, st_m, st_c]
    out_shape = [
        jax.ShapeDtypeStruct(lead + (wv,), _mix_dtype(L, valid)),
        jax.ShapeDtypeStruct(lead + (wv,), _mix_dtype(L, valid)),
        jax.ShapeDtypeStruct(C0.shape, F32), jax.ShapeDtypeStruct(n0.shape, F32),
        jax.ShapeDtypeStruct(m0.shape, F32), jax.ShapeDtypeStruct(S0.shape, F32),
    ]
    n_in = len(in_specs) - n_alias
    aliases = {n_in + k: 2 + k for k in range(n_alias)}
    args = [P] * 7 + [C0, n0, m0, S0] + consts + (list(prev) if prev is not None else [])
    return pl.pallas_call(
        functools.partial(_ab_kernel, L, valid, nlev, nc, bt, H, dk, dv, n_alias),
        grid=(B // bt, nc),
        in_specs=in_specs,
        out_specs=out_specs,
        out_shape=out_shape,
        input_output_aliases=aliases,
        scratch_shapes=[pltpu.VMEM((bt, HK, dv), F32), pltpu.VMEM((bt, 1, HK), F32),
                        pltpu.VMEM((bt, 1, H), F32), pltpu.VMEM((bt, HK, dv), F32)],
        compiler_params=pltpu.CompilerParams(
            dimension_semantics=("parallel", "arbitrary"), vmem_limit_bytes=VMEM_LIMIT),
        name="ab_scan",
    )(*args)


def _ssd_kernel(L, valid, nc, bt, G, hpg, hd, wave, preact, n_alias,
                p_ref, cv_ref, h0_ref,
                cw_ref, cb_ref, dtbr_ref, dtbc_ref, alr_ref, alc_ref, dsk_ref, nrm_ref,
                tmat_ref, exp_ref, sel_ref, *rest):
    y_ref, h_ref, catx, catb, catc, hs = rest[n_alias:]
    cidx = pl.program_id(1)
    gw = hpg * hd
    nh = G * hpg
    inner = nh * hd
    N = h0_ref.shape[4]
    W = cw_ref.shape[0]
    P0 = SUBLANES - (W - 1)
    per = LANES // hd
    cats = ((catx, 0, inner), (catb, inner, G * N), (catc, inner + G * N, G * N))

    lay = _SsdLayout(inner, G * N)

    @pl.when(cidx == 0)
    def _():
        if not preact:
            for cat, off, wid in cats:
                cat[:, P0:SUBLANES, :] = cv_ref[0, :, :, off:off + wid]
        hs[...] = h0_ref[0].reshape(bt, inner, N)

    t_col = lax.broadcasted_iota(jnp.int32, (L, 1), 0)
    s_row = lax.broadcasted_iota(jnp.int32, (1, L), 1)
    tt = lax.broadcasted_iota(jnp.int32, (L, L), 0)
    sc = lax.broadcasted_iota(jnp.int32, (L, L), 1)
    causal = sc <= tt
    lane = lax.broadcasted_iota(jnp.int32, (L, LANES), 1)
    upper = tmat_ref[0:L, :]
    lower = tmat_ref[L:2 * L, :]

    def conv_silu(cat, bi, off, lo, wid):
        xall = cat[bi, :, lo:lo + wid]
        cols = slice(off + lo, off + lo + wid)
        acc = cb_ref[:, cols] + xall[SUBLANES:SUBLANES + L] * cw_ref[W - 1:W, cols]
        for w in range(W - 1):
            tap = pltpu.roll(xall, W - 1 - w, axis=0)[SUBLANES:SUBLANES + L]
            acc = acc + tap * cw_ref[w:w + 1, cols]
        return _silu(acc)

    prow = lambda bi, off, wid: _seq_tile(p_ref, bi, slice(off, off + wid), L, valid)
    if not preact:
        assert nc == 1
        for bi in range(bt):
            for g in range(G):
                catx[bi, SUBLANES:SUBLANES + L, g * gw:(g + 1) * gw] = prow(bi, lay.xbc(g * gw), gw)
            catb[bi, SUBLANES:SUBLANES + L, :] = prow(bi, lay.xbc(inner), G * N)
            catc[bi, SUBLANES:SUBLANES + L, :] = prow(bi, lay.xbc(inner + G * N), G * N)

    BI = range(bt)
    each = lambda keys, f: {k: f(*k) for k in keys}
    gsl = lambda g: slice(g * gw, (g + 1) * gw)
    dt_raw = [prow(bi, lay.dt, LANES) for bi in BI]
    dtc = [_softplus(x + dtbr_ref[...]) for x in dt_raw]
    dtr = [_softplus(_rows_of_transpose(sel_ref[...], x) + dtbc_ref[...]) for x in dt_raw]
    if valid < L:
        dtc = [jnp.where(t_col < valid, x, 0.0) for x in dtc]
        dtr = [jnp.where(s_row < valid, x, 0.0) for x in dtr]
    cs_col = [_dot3(lower, x * (-jnp.exp(alr_ref[...]))) for x in dtc]
    cs_row = [_dot3_right(x * (-jnp.exp(alc_ref[...])), upper) for x in dtr]
    cd3 = [_split3(jnp.concatenate([cs_col[bi], dtc[bi]], axis=0)) for bi in BI]

    problems = [(bi, g) for bi in BI for g in range(G)]
    for w0 in range(0, len(problems), wave):
        keys = problems[w0:w0 + wave]
        hkeys = [(bi, g, jl) for bi, g in keys for jl in range(hpg)]
        if preact:
            xa = each(keys, lambda bi, g: prow(bi, lay.xbc(g * gw), gw))
            Bm = each(keys, lambda bi, g: prow(bi, lay.xbc(inner + g * N), N))
            Cm = each(keys, lambda bi, g: prow(bi, lay.xbc(inner + G * N + g * N), N))
            zg = each(keys, lambda bi, g: prow(bi, lay.z(g * gw), gw))
        else:
            xa = each(keys, lambda bi, g: conv_silu(catx, bi, 0, g * gw, gw))
            Bm = each(keys, lambda bi, g: conv_silu(catb, bi, inner, g * N, N))
            Cm = each(keys, lambda bi, g: conv_silu(catc, bi, inner + G * N, g * N, N))
            zg = each(keys, lambda bi, g: _silu(prow(bi, lay.z(g * gw), gw)))
        ce = each(keys, lambda bi, g: (
            jnp.dot(cd3[bi][0], exp_ref[:, gsl(g)], preferred_element_type=F32)
            + jnp.dot(cd3[bi][1], exp_ref[:, gsl(g)], preferred_element_type=F32)
            + jnp.dot(cd3[bi][2], exp_ref[:, gsl(g)], preferred_element_type=F32)))
        cs_exp = each(keys, lambda bi, g: ce[bi, g][0:L])
        dt_exp = each(keys, lambda bi, g: ce[bi, g][L:2 * L])
        Bmb = each(keys, lambda bi, g: Bm[bi, g].astype(BF16))
        Cmb = each(keys, lambda bi, g: Cm[bi, g].astype(BF16))
        CB = each(keys, lambda bi, g: _dot_nt(Cmb[bi, g], Bmb[bi, g]))
        hs0 = each(keys, lambda bi, g: hs[bi, gsl(g), :])
        yc = each(keys, lambda bi, g: _dot_nt(Cmb[bi, g], hs0[bi, g]) * jnp.exp(cs_exp[bi, g]))
        xab = each(keys, lambda bi, g: xa[bi, g].astype(BF16))
        seg = each(hkeys, lambda bi, g, jl: jnp.where(
            causal, cs_col[bi][:, g * hpg + jl:g * hpg + jl + 1] - cs_row[bi][g * hpg + jl:g * hpg + jl + 1, :],
            -jnp.inf))
        mj = each(hkeys, lambda bi, g, jl: CB[bi, g] * (
            jnp.exp(seg[bi, g, jl]) * dtr[bi][g * hpg + jl:g * hpg + jl + 1, :]))
        yh = each(hkeys, lambda bi, g, jl: _dot(
            mj[bi, g, jl], xab[bi, g][:, (jl // per) * LANES:(jl // per + 1) * LANES]))
        yi = each(keys, lambda bi, g: jnp.concatenate(
            [_select_chain(lane, [yh[bi, g, p * per + u] for u in range(per)], hd)
             for p in range(gw // LANES)], axis=1))
        y = each(keys, lambda bi, g: yc[bi, g] + yi[bi, g] + dsk_ref[:, gsl(g)] * xa[bi, g])
        y = each(keys, lambda bi, g: y[bi, g] * zg[bi, g])
        y = each(keys, lambda bi, g: y[bi, g] * lax.rsqrt(
            jnp.mean(y[bi, g] * y[bi, g], axis=1, keepdims=True) + LN_EPS) * nrm_ref[:, gsl(g)])
        if valid == L:
            for bi, g in keys:
                y_ref[bi, :, gsl(g)] = y[bi, g].astype(y_ref.dtype)
        else:
            assert len(keys) == bt * G
            for g in range(G):
                _store_seqs(y_ref, gsl(g), [y[bi, g] for bi in BI], L, valid)

        xw = each(keys, lambda bi, g: xa[bi, g] * (
            jnp.exp(cs_exp[bi, g][L - 1:L, :] - cs_exp[bi, g]) * dt_exp[bi, g]))
        upd = each(keys, lambda bi, g: _dot_tn(xw[bi, g], Bmb[bi, g]))
        dec = each(hkeys, lambda bi, g, jl: jnp.exp(cs_row[bi][g * hpg + jl:g * hpg + jl + 1, L - 1:L]))
        for bi, g, jl in hkeys:
            r0 = g * gw + jl * hd
            hs[bi, r0:r0 + hd, :] = (dec[bi, g, jl] * hs0[bi, g][jl * hd:(jl + 1) * hd, :]
                                     + upd[bi, g][jl * hd:(jl + 1) * hd, :])

    @pl.when(cidx == nc - 1)
    def _():
        h_ref[0] = hs[...].reshape(h_ref.shape[1:])


def _ssd_scan(P, B, nc, L, valid, bt, j, h0, conv0, prev, prm):
    nst, _, nh, hd, N = h0.shape
    G = prm["groups"]
    hpg = nh // G
    gw = hpg * hd
    inner = nh * hd
    GN = G * N
    W1, cd = conv0.shape[2], conv0.shape[3]
    lead, blk = _row_blocks(B, nc, L, valid, bt)
    assert P.shape[:2] == lead and N == LANES and gw % LANES == 0 and B % bt == 0
    assert nh <= LANES and inner % GN == 0 and cd == inner + 2 * GN
    tmat = _ssd_mats(L)
    sel = _sel_rows(nh)
    wave = bt * G if L <= 2 * SUBLANES else 2
    lay = _SsdLayout(inner, GN)
    assert P.shape[2] == 2 * lay.tile and lay.half % gw == 0 and lay.zhalf % gw == 0
    preact = valid == L
    crows = SUBLANES if preact else SUBLANES + L
    full = lambda a: pl.BlockSpec(a.shape, lambda b, c: (0,) * a.ndim)
    st_h = pl.BlockSpec((1, bt, nh, hd, N), lambda b, c: (j, b, 0, 0, 0))
    n_alias = 0 if prev is None else 1
    consts = [prm["conv_w"], prm["conv_b"], prm["dtb_row"], prm["dtb_col"], prm["alog_row"],
              prm["alog_col"], prm["dskip"], prm["norm"], tmat, prm["expand"], sel]
    in_specs = [
        blk(2 * lay.tile, 0),
        pl.BlockSpec((1, bt, W1, cd), lambda b, c: (j, b, 0, 0)),
        st_h,
    ] + [full(a) for a in consts] + [pl.BlockSpec(memory_space=pl.ANY)] * n_alias
    out_specs = [blk(inner, 0), st_h]
    out_shape = [jax.ShapeDtypeStruct(lead + (inner,), _mix_dtype(L, valid)),
                 jax.ShapeDtypeStruct(h0.shape, F32)]
    n_in = len(in_specs) - n_alias
    args = [P, conv0, h0] + consts + ([prev] if prev is not None else [])
    return pl.pallas_call(
        functools.partial(_ssd_kernel, L, valid, nc, bt, G, hpg, hd, wave, preact, n_alias),
        grid=(B // bt, nc),
        in_specs=in_specs,
        out_specs=out_specs,
        out_shape=out_shape,
        input_output_aliases={n_in: 1} if n_alias else {},
        scratch_shapes=[pltpu.VMEM((bt, crows, inner), F32), pltpu.VMEM((bt, crows, GN), F32),
                        pltpu.VMEM((bt, crows, GN), F32), pltpu.VMEM((bt, inner, N), F32)],
        compiler_params=pltpu.CompilerParams(
            dimension_semantics=("parallel", "arbitrary"), vmem_limit_bytes=VMEM_LIMIT),
        name="ssd_scan",
    )(*args)


def _prep_ab(j, H, dk, dv, w_in, ig_bias, fg_bias, ml_norm, wa2, ba, gla_norm, w_out):
    rank = wa2.shape[1]
    qk, vv = H * dk, H * dv
    o = np.cumsum([0, qk, qk, vv, vv, H, H, qk, qk, vv, vv, rank])
    assert int(o[-1]) == w_in.shape[2]
    wj = w_in[j]
    small = jnp.concatenate([wj[:, o[4]:o[6]], wj[:, o[10]:o[11]]], axis=1)
    small = jnp.pad(small, ((0, 0), (0, LANES - small.shape[1])))
    w = jnp.concatenate([wj[:, :o[4]], wj[:, o[6]:o[10]], small], axis=1).astype(BF16)
    wa2p = jnp.zeros((LANES, qk), F32).at[2 * H:2 * H + rank, :].set(wa2[j]).astype(BF16)
    return {
        "w_in": w,
        "gate_bias": jnp.concatenate([ig_bias[j], fg_bias[j]]).astype(F32).reshape(2 * H, 1),
        "ml_norm": ml_norm[j].reshape(1, vv),
        "gla_norm": gla_norm[j].reshape(1, vv),
        "wa2": wa2p,
        "ba": ba[j].reshape(1, qk),
        "w_out_ml": w_out[j][:vv].astype(BF16),
        "w_out_gla": w_out[j][vv:].astype(BF16),
    }


def _prep_ssd(j, G, nh, hd, N, w_in, conv_w, conv_b, dt_bias, a_log, d_skip, norm_g, w_out):
    inner = nh * hd
    hpg = nh // G
    cd = inner + 2 * G * N
    wz, wxbc, wdt = w_in[j][:, :inner], w_in[j][:, inner:inner + cd], w_in[j][:, inner + cd:]
    lay = _SsdLayout(inner, G * N)
    wdt = jnp.pad(wdt, ((0, 0), (0, LANES - nh)))
    w = jnp.concatenate([wxbc[:, :lay.half], wz[:, :lay.zhalf], wdt,
                         wxbc[:, lay.half:], wz[:, lay.zhalf:], jnp.zeros_like(wdt)], axis=1).astype(BF16)
    lane_form = lambda v: jnp.pad(v, (0, LANES - nh)).reshape(1, LANES)
    e = np.zeros((LANES, inner), np.float32)
    for r in range(nh):
        e[r, r * hd:(r + 1) * hd] = 1.0
    return {
        "groups": G,
        "w_in": w,
        "conv_w": conv_w[j], "conv_b": conv_b[j].reshape(1, cd),
        "dtb_row": lane_form(dt_bias[j]), "dtb_col": dt_bias[j].reshape(nh, 1),
        "alog_row": lane_form(a_log[j]), "alog_col": a_log[j].reshape(nh, 1),
        "dskip": jnp.repeat(d_skip[j], hd).reshape(1, inner),
        "norm": norm_g[j].reshape(1, inner),
        "expand": jnp.asarray(e, dtype=BF16),
        "w_out": w_out[j].astype(BF16),
    }


def _largest_divisor(n, cap, step=1):
    return max(d for d in range(step, cap + 1, step) if n % d == 0)


def _plan(B, T, chunk):
    if T % chunk == 0:
        return chunk, T // chunk, _largest_divisor(B, 4), _largest_divisor(B, 2)
    L = max(SUBLANES, 1 << int(math.ceil(math.log2(T))))
    per = L // T
    assert L % T == 0 and B % per == 0
    return L, 1, _largest_divisor(B, 8, per), _largest_divisor(B, 4, per)


def _trunk(x, states, ab_prm, ssd_prm, mlp, lns, alpha, chunk):
    B, T, D = x.shape
    mC, mn, mm, gS, sh, sconv = states
    n_ab, _, H, dk, dv = mC.shape
    L, nc, bt_ab, bt_ssd = _plan(B, T, chunk)
    valid = T if nc == 1 else L
    M = B * T
    tm = 512 if M % 512 == 0 else M
    tmm = 1024 if M % 1024 == 0 else tm
    X = x.reshape(M, D)
    ab_states = (mC, mn.reshape(n_ab, B, 1, H * dk), mm.reshape(n_ab, B, 1, H), gS)
    ab_out, h_out, ncv = None, None, []
    depth = mlp[0].shape[0]
    for l in range(depth):
        j = l // 2
        if l % 2 == 0:
            p = ab_prm[j]
            P = _proj(X, p["w_in"], tm, p["w_in"].shape[1])
            lead = _row_blocks(B, nc, L, valid, bt_ab)[0]
            res = _ab_scan(P.reshape(lead + (-1,)), B, nc, L, valid, bt_ab, j, ab_states, ab_out, p)
            ab_out = res[2:]
            ys = [res[0].reshape(M, -1), res[1].reshape(M, -1)]
            ws = [p["w_out_ml"], p["w_out_gla"]]
        else:
            p = ssd_prm[j]
            inner = p["norm"].shape[1]
            cd = p["conv_b"].shape[1]
            lay = _SsdLayout(inner, (cd - inner) // 2)
            W1 = sconv.shape[2]
            assert T >= W1
            if valid == L:
                tmc = max(t for t in (128, 256, 512) if T % t == 0)
                P, tail = _proj_conv(X, p["w_in"], p["conv_w"], p["conv_b"], sconv, j, B, tmc, lay)
                ncv.append(tail[:, SUBLANES - W1:])
            else:
                P = _proj(X, p["w_in"], tm, lay.tile)
                last = P.reshape(B, T, -1)[:, T - W1:]
                ncv.append(jnp.concatenate([last[:, :, :lay.half], last[:, :, lay.tile:lay.tile + lay.half]], axis=2))
            lead = _row_blocks(B, nc, L, valid, bt_ssd)[0]
            y, h_out = _ssd_scan(P.reshape(lead + (-1,)), B, nc, L, valid, bt_ssd, j, sh, sconv, h_out, p)
            ys = [y.reshape(M, -1)]
            ws = [p["w_out"]]
        X = _outproj_ln(ys, ws, X, lns[0], lns[1], l, alpha, tmm)
        X = _mlp_ln(X, mlp[0], mlp[1], lns[2], lns[3], l, alpha, tmm, 1024)
    nC, nn_, nm, nS = ab_out
    return (X.reshape(B, T, D), nC, nn_.reshape(mn.shape), nm.reshape(mm.shape), nS, h_out, jnp.stack(ncv))


def kernel(x_prompt, x_sample, state_mlstm_C, state_mlstm_n, state_mlstm_m, state_gla_S, state_ssd_h,
           state_ssd_conv, ab_w_in, ab_ig_bias, ab_fg_bias, ab_ml_norm, ab_gla_wa2, ab_gla_ba, ab_gla_norm,
           ab_w_out, ssd_w_in, ssd_conv_w, ssd_conv_b, ssd_dt_bias, ssd_a_log, ssd_d, ssd_norm, ssd_w_out,
           mlp_w1, mlp_w2, ln_mix_g, ln_mix_b, ln_mlp_g, ln_mlp_b):
    depth = mlp_w1.shape[0]
    D = x_prompt.shape[2]
    alpha = (2 * depth) ** 0.25
    n_ab, _, H, dk, dv = state_mlstm_C.shape
    n_ssd, _, nh, hd, N = state_ssd_h.shape
    cd = state_ssd_conv.shape[3]
    G = (cd - nh * hd) // (2 * N)
    ab_prm = [_prep_ab(j, H, dk, dv, ab_w_in, ab_ig_bias, ab_fg_bias, ab_ml_norm, ab_gla_wa2,
                       ab_gla_ba, ab_gla_norm, ab_w_out) for j in range(n_ab)]
    ssd_prm = [_prep_ssd(j, G, nh, hd, N, ssd_w_in, ssd_conv_w, ssd_conv_b, ssd_dt_bias, ssd_a_log,
                         ssd_d, ssd_norm, ssd_w_out) for j in range(n_ssd)]
    mlp = (mlp_w1, mlp_w2)
    lns = tuple(a.reshape(depth, 1, D) for a in (ln_mix_g, ln_mix_b, ln_mlp_g, ln_mlp_b))

    Bp = x_prompt.shape[0]
    zeros = (jnp.zeros((n_ab, Bp, H, dk, dv), F32), jnp.zeros((n_ab, Bp, H, dk), F32),
             jnp.zeros((n_ab, Bp, H), F32), jnp.zeros((n_ab, Bp, H, dk, dv), F32),
             jnp.zeros((n_ssd, Bp, nh, hd, N), F32), jnp.zeros((n_ssd, Bp) + state_ssd_conv.shape[2:], F32))
    carried = (state_mlstm_C, state_mlstm_n, state_mlstm_m, state_gla_S, state_ssd_h, state_ssd_conv)
    chunk = 128
    yp = _trunk(x_prompt, zeros, ab_prm, ssd_prm, mlp, lns, alpha, chunk)
    ys = _trunk(x_sample, carried, ab_prm, ssd_prm, mlp, lns, alpha, chunk)
    return (yp[0], ys[0]) + yp[1:] + ys[1:]
```

```python
import functools
import itertools
import math

import jax
import jax.numpy as jnp
import numpy as np
from jax import lax
from jax.experimental import pallas as pl
from jax.experimental.pallas import tpu as pltpu

F32 = jnp.float32
BF16 = jnp.bfloat16

LN_EPS = 1e-5
GLA_TAU = 16.0
LANES = 128
SUBLANES = 8
VMEM_LIMIT = 48 * 1024 * 1024


def _dot(a, b):
    return jnp.dot(a.astype(BF16), b.astype(BF16), preferred_element_type=F32)


def _dot_nt(a, b):
    return lax.dot_general(a.astype(BF16), b.astype(BF16), (((1,), (1,)), ((), ())),
                           preferred_element_type=F32)


def _dot_tn(a, b):
    return lax.dot_general(a.astype(BF16), b.astype(BF16), (((0,), (0,)), ((), ())),
                           preferred_element_type=F32)


def _split3(x):
    hi = x.astype(BF16)
    r = x - hi.astype(F32)
    mid = r.astype(BF16)
    lo = (r - mid.astype(F32)).astype(BF16)
    return hi, mid, lo


def _dot3(t, x):
    hi, mid, lo = _split3(x)
    f = lambda p: jnp.dot(t, p, preferred_element_type=F32)
    return f(hi) + f(mid) + f(lo)


def _dot3_right(x, t):
    hi, mid, lo = _split3(x)
    f = lambda p: jnp.dot(p, t, preferred_element_type=F32)
    return f(hi) + f(mid) + f(lo)


def _rows_of_transpose(sel, x):
    hi, mid, lo = _split3(x)
    f = lambda p: lax.dot_general(sel, p, (((1,), (1,)), ((), ())), preferred_element_type=F32)
    return f(hi) + f(mid) + f(lo)


def _softplus(x):
    e = jnp.exp(-jnp.abs(x))
    u = 1.0 + e
    d = u - 1.0
    return jnp.maximum(x, 0.0) + jnp.where(d == 0.0, e, jnp.log(u) * (e / d))


def _log_sigmoid(x):
    return -_softplus(-x)


def _sigmoid(x):
    return 0.5 + 0.5 * jnp.tanh(0.5 * x)


def _silu(x):
    h = 0.5 * x
    return h + h * jnp.tanh(h)


def _seq_tile(ref, bi, cols, L, T):
    if T == L:
        return ref[bi, :, cols]
    r0 = bi * T
    a = (r0 // L) * L
    tile = ref[0, a:a + L, cols]
    return tile if r0 == a else pltpu.roll(tile, L - (r0 - a), axis=0)


def _store_seqs(ref, cols, vals, L, T, first=0):
    if T == L:
        for i, v in enumerate(vals):
            ref[first + i, :, cols] = v.astype(ref.dtype)
        return
    per = L // T
    assert first % per == 0 and len(vals) % per == 0
    row = lax.broadcasted_iota(jnp.int32, vals[0].shape, 0)
    for a in range(len(vals) // per):
        out = vals[a * per]
        for k in range(1, per):
            out = jnp.where(row < k * T, out, pltpu.roll(vals[a * per + k], k * T, axis=0))
        r0 = (first // per + a) * L
        ref[0, r0:r0 + L, cols] = out


def _interleave(*gens):
    gens = list(gens)
    while gens:
        for g in list(gens):
            try:
                next(g)
            except StopIteration:
                gens.remove(g)


def _layernorm_rows(r, g, b):
    mu = jnp.mean(r, axis=1, keepdims=True)
    d = r - mu
    var = jnp.mean(d * d, axis=1, keepdims=True)
    return d * lax.rsqrt(var + LN_EPS) * g + b


def _headnorm(h, g):
    mu = jnp.mean(h, axis=1, keepdims=True)
    d = h - mu
    var = jnp.mean(d * d, axis=1, keepdims=True)
    return d * lax.rsqrt(var + LN_EPS) * g


def _select_chain(idx, pieces, width):
    out = pieces[0]
    for u in range(1, len(pieces)):
        out = jnp.where(idx < u * width, out, pieces[u])
    return out


def _cumsum_mats(L):
    t = np.arange(L)[:, None]
    j = np.arange(L)[None, :]
    upper = (t <= j)
    lower = (j <= t)
    return upper, lower


def _ab_mats(L):
    nlev = int(round(math.log2(L)))
    assert 1 << nlev == L
    upper, lower = _cumsum_mats(L)
    t = np.arange(L)[:, None]
    j = np.arange(L)[None, :]
    mats = [upper, lower]
    for i in range(nlev):
        n = L >> (i + 1)
        mid = (t // (2 * n)) * (2 * n) + n - 1
        second = (t % (2 * n)) >= n
        m = np.where(second, (j > mid) & (j <= t), (j > t) & (j <= mid))
        mats.append(m)
    return jnp.asarray(np.concatenate(mats, axis=0).astype(np.float32), dtype=BF16), nlev


def _ssd_mats(L):
    upper, lower = _cumsum_mats(L)
    return jnp.asarray(np.concatenate([upper, lower], axis=0).astype(np.float32), dtype=BF16)


def _proj_kernel(x_ref, w_ref, o_ref):
    o_ref[...] = jnp.dot(x_ref[...].astype(BF16), w_ref[...], preferred_element_type=F32)


def _proj(x, w, tm, tn):
    M, K = x.shape
    N = w.shape[1]
    assert M % tm == 0 and N % tn == 0
    return pl.pallas_call(
        _proj_kernel,
        grid=(N // tn, M // tm),
        in_specs=[pl.BlockSpec((tm, K), lambda j, i: (i, 0)),
                  pl.BlockSpec((K, tn), lambda j, i: (0, j))],
        out_specs=pl.BlockSpec((tm, tn), lambda j, i: (i, j)),
        out_shape=jax.ShapeDtypeStruct((M, N), F32),
        compiler_params=pltpu.CompilerParams(
            dimension_semantics=("parallel", "parallel"), vmem_limit_bytes=VMEM_LIMIT),
        name="proj",
    )(x, w)


class _SsdLayout:
    def __init__(self, inner, gn):
        self.cd = inner + 2 * gn
        self.half = self.cd // 2
        self.zhalf = inner // 2
        self.tile = self.half + self.zhalf + LANES
        self.dt = self.half + self.zhalf
        assert self.cd % (2 * LANES) == 0 and inner % (2 * LANES) == 0

    def xbc(self, c):
        return c if c < self.half else self.tile + c - self.half

    def z(self, c):
        return self.half + c if c < self.zhalf else self.tile + self.half + c - self.zhalf


def _proj_conv_kernel(tps, nrb, half, zhalf, x_ref, w_ref, cw_ref, cb_ref, cv_ref, o_ref, tail_ref, hist):
    i = pl.program_id(1)
    W = cw_ref.shape[0]
    P0 = SUBLANES - (W - 1)
    rb = x_ref.shape[0] // nrb

    @pl.when(i % tps == 0)
    def _():
        hist[...] = jnp.zeros_like(hist)
        hist[P0:SUBLANES, :] = cv_ref[0, 0]

    def raw_rows(r):
        return jnp.dot(x_ref[r * rb:(r + 1) * rb, :].astype(BF16), w_ref[...], preferred_element_type=F32)

    def finish(r, raw, prev):
        rows = slice(r * rb, (r + 1) * rb)
        xall = jnp.concatenate([prev, raw[:, :half]], axis=0)
        acc = cb_ref[...] + xall[SUBLANES:] * cw_ref[W - 1:W, :]
        for w in range(W - 1):
            acc = acc + pltpu.roll(xall, W - 1 - w, axis=0)[SUBLANES:] * cw_ref[w:w + 1, :]
        o_ref[rows, :half] = _silu(acc)
        o_ref[rows, half:half + zhalf] = _silu(raw[:, half:half + zhalf])
        o_ref[rows, half + zhalf:] = raw[:, half + zhalf:]
        return raw[rb - SUBLANES:, :half]

    prev = hist[...]
    raw = raw_rows(0)
    for r in range(nrb):
        nxt = raw_rows(r + 1) if r + 1 < nrb else None
        prev = finish(r, raw, prev)
        raw = nxt
    hist[...] = prev
    tail_ref[0] = prev


def _proj_conv(x, w, conv_w, conv_b, conv0, j, B, tm, lay):
    M, K = x.shape
    T = M // B
    assert M % tm == 0 and T % tm == 0 and w.shape[1] == 2 * lay.tile and (tm // 4) % SUBLANES == 0
    tps = T // tm
    W1 = conv0.shape[2]
    return pl.pallas_call(
        functools.partial(_proj_conv_kernel, tps, 4, lay.half, lay.zhalf),
        grid=(2, M // tm),
        in_specs=[pl.BlockSpec((tm, K), lambda c, i: (i, 0)),
                  pl.BlockSpec((K, lay.tile), lambda c, i: (0, c)),
                  pl.BlockSpec((W1 + 1, lay.half), lambda c, i: (0, c)),
                  pl.BlockSpec((1, lay.half), lambda c, i: (0, c)),
                  pl.BlockSpec((1, 1, W1, lay.half), lambda c, i: (j, i // tps, 0, c))],
        out_specs=[pl.BlockSpec((tm, lay.tile), lambda c, i: (i, c)),
                   pl.BlockSpec((1, SUBLANES, lay.half), lambda c, i: (i // tps, 0, c))],
        out_shape=[jax.ShapeDtypeStruct((M, 2 * lay.tile), F32),
                   jax.ShapeDtypeStruct((B, SUBLANES, lay.cd), F32)],
        scratch_shapes=[pltpu.VMEM((SUBLANES, lay.half), F32)],
        compiler_params=pltpu.CompilerParams(
            dimension_semantics=("parallel", "arbitrary"), vmem_limit_bytes=VMEM_LIMIT),
        name="proj_conv",
    )(x, w, conv_w, conv_b, conv0)


def _outproj_ln_kernel(alpha, n_in, *refs):
    ys = refs[:n_in]
    ws = refs[n_in:2 * n_in]
    x_ref, g_ref, b_ref, o_ref = refs[2 * n_in:]
    tm = x_ref.shape[0]
    nrb = 4 if tm % (4 * SUBLANES) == 0 else 1
    rb = tm // nrb

    def mix(k):
        rows = slice(k * rb, (k + 1) * rb)
        r = alpha * x_ref[rows, :]
        for y_ref, w_ref in zip(ys, ws):
            r = r + jnp.dot(y_ref[rows, :].astype(BF16), w_ref[...], preferred_element_type=F32)
        return r

    r = mix(0)
    for k in range(nrb):
        nxt = mix(k + 1) if k + 1 < nrb else None
        o_ref[k * rb:(k + 1) * rb, :] = _layernorm_rows(r, g_ref[0], b_ref[0])
        r = nxt


def _outproj_ln(ys, ws, x, g, b, l, alpha, tm):
    M, D = x.shape
    assert M % tm == 0
    n_in = len(ys)
    in_specs = ([pl.BlockSpec((tm, y.shape[1]), lambda i: (i, 0)) for y in ys]
                + [pl.BlockSpec(w.shape, lambda i: (0, 0)) for w in ws]
                + [pl.BlockSpec((tm, D), lambda i: (i, 0)),
                   pl.BlockSpec((1, 1, D), lambda i: (l, 0, 0)),
                   pl.BlockSpec((1, 1, D), lambda i: (l, 0, 0))])
    return pl.pallas_call(
        functools.partial(_outproj_ln_kernel, alpha, n_in),
        grid=(M // tm,),
        in_specs=in_specs,
        out_specs=pl.BlockSpec((tm, D), lambda i: (i, 0)),
        out_shape=jax.ShapeDtypeStruct((M, D), F32),
        compiler_params=pltpu.CompilerParams(
            dimension_semantics=("parallel",), vmem_limit_bytes=VMEM_LIMIT),
        name="outproj_ln",
    )(*ys, *ws, x, g, b)


def _mlp_kernel(alpha, nf, x_ref, w1_ref, w2_ref, g_ref, b_ref, o_ref, acc_ref):
    f = pl.program_id(1)

    @pl.when(f == 0)
    def _():
        acc_ref[...] = jnp.zeros_like(acc_ref)

    h = jnp.dot(x_ref[...].astype(BF16), w1_ref[0].astype(BF16), preferred_element_type=F32)
    h = jnp.square(jnp.maximum(h, 0.0))
    acc_ref[...] += jnp.dot(h.astype(BF16), w2_ref[0].astype(BF16), preferred_element_type=F32)

    @pl.when(f == nf - 1)
    def _():
        r = alpha * x_ref[...] + acc_ref[...]
        o_ref[...] = _layernorm_rows(r, g_ref[0], b_ref[0])


def _mlp_ln(x, w1, w2, g, b, l, alpha, tm, tf):
    M, D = x.shape
    Fdim = w1.shape[2]
    assert M % tm == 0 and Fdim % tf == 0
    nf = Fdim // tf
    return pl.pallas_call(
        functools.partial(_mlp_kernel, alpha, nf),
        grid=(M // tm, nf),
        in_specs=[pl.BlockSpec((tm, D), lambda i, f: (i, 0)),
                  pl.BlockSpec((1, D, tf), lambda i, f: (l, 0, f)),
                  pl.BlockSpec((1, tf, D), lambda i, f: (l, f, 0)),
                  pl.BlockSpec((1, 1, D), lambda i, f: (l, 0, 0)),
                  pl.BlockSpec((1, 1, D), lambda i, f: (l, 0, 0))],
        out_specs=pl.BlockSpec((tm, D), lambda i, f: (i, 0)),
        out_shape=jax.ShapeDtypeStruct((M, D), F32),
        scratch_shapes=[pltpu.VMEM((tm, D), F32)],
        compiler_params=pltpu.CompilerParams(
            dimension_semantics=("parallel", "arbitrary"), vmem_limit_bytes=VMEM_LIMIT),
        name="mlp_ln",
    )(x, w1, w2, g, b)


def _ab_kernel(L, valid, nlev, nc, bt, H, dk, dv, n_alias,
               qk_ref, v_ref, mo_ref, gqk_ref, gv_ref, gg_ref, sm_ref,
               c0_ref, n0_ref, m0_ref, s0_ref, gb_ref, mln_ref, wa2_ref, ba_ref, gln_ref, tmat_ref,
               sel_ref, *rest):
    ml_ref, gla_ref, c_ref, n_ref, m_ref, s_ref, cs, ns, ms, ss = rest[n_alias:]
    c = pl.program_id(1)
    per = LANES // dk
    HK = H * dk

    @pl.when(c == 0)
    def _():
        cs[...] = c0_ref[0].reshape(bt, HK, dv)
        ns[...] = n0_ref[0]
        ms[...] = m0_ref[0]
        ss[...] = s0_ref[0].reshape(bt, HK, dv)

    t_col = lax.broadcasted_iota(jnp.int32, (L, 1), 0)
    s_row = lax.broadcasted_iota(jnp.int32, (1, L), 1)
    tt = lax.broadcasted_iota(jnp.int32, (L, L), 0)
    sc = lax.broadcasted_iota(jnp.int32, (L, L), 1)
    causal = sc <= tt
    eye = sc == tt
    lane = lax.broadcasted_iota(jnp.int32, (L, LANES), 1)
    lane1 = lax.broadcasted_iota(jnp.int32, (1, LANES), 1)
    laneh = lax.broadcasted_iota(jnp.int32, (1, H), 1)
    rowp = lax.broadcasted_iota(jnp.int32, (LANES, dv), 0)
    ek = lax.broadcasted_iota(jnp.int32, (LANES, LANES), 0) == lax.broadcasted_iota(jnp.int32, (LANES, LANES), 1)
    inhead = [(lane >= u * dk) & (lane < (u + 1) * dk) for u in range(per)]
    upper = tmat_ref[0:L, :]
    scale = dk ** -0.5

    def to_col(row):
        return jnp.sum(jnp.where(eye, row, 0.0), axis=1, keepdims=True)

    psl = lambda p: slice(p * LANES, (p + 1) * LANES)
    hsl = lambda h: slice(h * dv, (h + 1) * dv)
    each = lambda keys, f: {k: f(*k) for k in keys}
    neg_inf = -jnp.inf
    allc = slice(None)
    tile = lambda ref, bi, cols: _seq_tile(ref, bi, cols, L, valid)

    def store(ref, vals, BI):
        for h in range(H):
            _store_seqs(ref, hsl(h), [vals[bi, h] for bi in BI], L, valid, BI[0])

    def mlstm(BI):
        pairs = [(bi, p) for bi in BI for p in range(H // per)]
        heads = [(bi, h) for bi in BI for h in range(H)]
        seqs = lambda f: {bi: f(bi) for bi in BI}
        ig8 = seqs(lambda bi: _rows_of_transpose(sel_ref[...], tile(sm_ref, bi, allc)) + gb_ref[...])
        yield
        lf8 = seqs(lambda bi: _log_sigmoid(ig8[bi]))
        yield
        if valid < L:
            ok = s_row < valid
            ig8 = seqs(lambda bi: jnp.where(ok, ig8[bi], neg_inf))
            lf8 = seqs(lambda bi: jnp.where(ok, lf8[bi], 0.0))
        b8 = seqs(lambda bi: _dot3_right(lf8[bi], upper))
        yield
        m_prev = seqs(lambda bi: ms[bi])
        qk = seqs(lambda bi: tile(qk_ref, bi, allc))
        yield
        Qp = each(pairs, lambda bi, p: qk[bi][:, psl(p)])
        Kp = each(pairs, lambda bi, p: qk[bi][:, HK + p * LANES:HK + (p + 1) * LANES] * scale)
        yield
        Kpb = each(pairs, lambda bi, p: Kp[bi, p].astype(BF16))
        C0p = each(pairs, lambda bi, p: cs[bi, psl(p), :])
        n0p = each(pairs, lambda bi, p: ns[bi][:, psl(p)])
        yield
        ig_row = each(heads, lambda bi, h: ig8[bi][h:h + 1, :])
        b_row = each(heads, lambda bi, h: b8[bi][H + h:H + h + 1, :])
        b_col = each(heads, lambda bi, h: to_col(b_row[bi, h]))
        yield
        ig_col = each(heads, lambda bi, h: to_col(ig_row[bi, h]))
        yield
        D = each(heads, lambda bi, h: jnp.where(causal, b_col[bi, h] - b_row[bi, h] + ig_row[bi, h], neg_inf))
        yield
        g_col = each(heads, lambda bi, h: b_col[bi, h] + m_prev[bi][:, h:h + 1])
        m_col = each(heads, lambda bi, h: jnp.maximum(g_col[bi, h], jnp.max(D[bi, h], axis=1, keepdims=True)))
        yield
        w_intra = each(heads, lambda bi, h: jnp.exp(D[bi, h] - m_col[bi, h]))
        yield
        w_inter = each(heads, lambda bi, h: jnp.exp(g_col[bi, h] - m_col[bi, h]))
        Qh = each(heads, lambda bi, h: jnp.where(inhead[h % per], Qp[bi, h // per], 0.0))
        yield
        Qhb = each(heads, lambda bi, h: Qh[bi, h].astype(BF16))
        vb = each(heads, lambda bi, h: tile(v_ref, bi, hsl(h)).astype(BF16))
        yield
        s = each(heads, lambda bi, h: _dot_nt(Qhb[bi, h], Kpb[bi, h // per]) * w_intra[bi, h])
        yield
        qc = each(heads, lambda bi, h: _dot(Qhb[bi, h], C0p[bi, h // per]))
        yield
        num = each(heads, lambda bi, h: _dot(s[bi, h], vb[bi, h]) + w_inter[bi, h] * qc[bi, h])
        yield
        den = each(heads, lambda bi, h: (jnp.sum(s[bi, h], axis=1, keepdims=True) + w_inter[bi, h]
                                         * jnp.sum(Qh[bi, h] * n0p[bi, h // per], axis=1, keepdims=True)))
        yield
        hh = each(heads, lambda bi, h: num[bi, h] / jnp.maximum(jnp.abs(den[bi, h]), jnp.exp(-m_col[bi, h])))
        yield
        ml = each(heads, lambda bi, h: (_headnorm(hh[bi, h], mln_ref[:, hsl(h)])
                                        * _sigmoid(tile(mo_ref, bi, hsl(h)))))
        yield
        store(ml_ref, ml, BI)
        yield
        mL = each(heads, lambda bi, h: m_col[bi, h][L - 1:L, :])
        wL_col = each(heads, lambda bi, h: jnp.exp(b_col[bi, h][L - 1:L, :] - b_col[bi, h] + ig_col[bi, h] - mL[bi, h]))
        yield
        wL0 = each(heads, lambda bi, h: jnp.exp(g_col[bi, h][L - 1:L, :] - mL[bi, h]))
        kw = each(heads, lambda bi, h: Kp[bi, h // per] * wL_col[bi, h])
        yield
        c_new = each(heads, lambda bi, h: wL0[bi, h] * C0p[bi, h // per] + _dot_tn(kw[bi, h], vb[bi, h]))
        yield
        n_new = each(heads, lambda bi, h: wL0[bi, h] * n0p[bi, h // per] + jnp.sum(kw[bi, h], axis=0, keepdims=True))
        yield
        for bi, p in pairs:
            cs[bi, psl(p), :] = _select_chain(rowp, [c_new[bi, p * per + u] for u in range(per)], dk)
            ns[bi, :, psl(p)] = _select_chain(lane1, [n_new[bi, p * per + u] for u in range(per)], dk)
        yield
        for bi in BI:
            m_new = m_prev[bi]
            for h in range(H):
                m_new = jnp.where(laneh == h, mL[bi, h], m_new)
            ms[bi] = m_new
        yield

    def gla(BI):
        pairs = [(bi, p) for bi in BI for p in range(H // per)]
        heads = [(bi, h) for bi in BI for h in range(H)]
        seqs = lambda f: {bi: f(bi) for bi in BI}
        gqk = seqs(lambda bi: tile(gqk_ref, bi, allc))
        yield
        Q2 = seqs(lambda bi: gqk[bi][:, :HK] * scale)
        K2 = seqs(lambda bi: gqk[bi][:, HK:])
        if valid < L:
            K2 = seqs(lambda bi: jnp.where(t_col < valid, K2[bi], 0.0))
        yield
        la = seqs(lambda bi: _log_sigmoid(_dot(tile(sm_ref, bi, allc), wa2_ref[...]) + ba_ref[...])
                  * (1.0 / GLA_TAU))
        if valid < L:
            la = seqs(lambda bi: jnp.where(t_col < valid, la[bi], 0.0))
        yield
        TL = seqs(lambda bi: _dot3(tmat_ref[L:(2 + nlev) * L, :], la[bi]))
        yield
        Q2b = seqs(lambda bi: Q2[bi].astype(BF16))
        K2b = seqs(lambda bi: K2[bi].astype(BF16))
        yield
        scores = each(heads, lambda bi, h: jnp.where(
            eye, _dot_nt(jnp.where(inhead[h % per], Q2b[bi][:, psl(h // per)], 0), K2b[bi][:, psl(h // per)]), 0.0))
        yield
        for i in range(nlev):
            n = L >> (i + 1)
            second = (t_col & n) != 0
            En = seqs(lambda bi: jnp.exp(TL[bi][(1 + i) * L:(2 + i) * L]))
            yield
            X = seqs(lambda bi: (jnp.where(second, Q2[bi], K2[bi]) * En[bi]).astype(BF16))
            yield
            Xk = seqs(lambda bi: jnp.where(second, 0, X[bi]))
            yield
            sn = each(heads, lambda bi, h: _dot_nt(
                jnp.where(second & inhead[h % per], X[bi][:, psl(h // per)], 0), Xk[bi][:, psl(h // per)]))
            yield
            if i > 0:
                sh = int(round(math.log2(2 * n)))
                same = (tt >> sh) == (sc >> sh)
                sn = each(heads, lambda bi, h: jnp.where(same, sn[bi, h], 0.0))
            scores = each(heads, lambda bi, h: scores[bi, h] + sn[bi, h])
            yield
        A = seqs(lambda bi: TL[bi][0:L])
        AL = seqs(lambda bi: A[bi][L - 1:L, :])
        QA = seqs(lambda bi: (Q2[bi] * jnp.exp(A[bi])).astype(BF16))
        yield
        kd = seqs(lambda bi: (K2[bi] * jnp.exp(AL[bi] - A[bi])).astype(BF16))
        eAL = seqs(lambda bi: jnp.exp(AL[bi]))
        yield
        S0p = each(pairs, lambda bi, p: ss[bi, psl(p), :])
        dec_col = each(pairs, lambda bi, p: jnp.sum(jnp.where(ek, eAL[bi][:, psl(p)], 0.0), axis=1, keepdims=True))
        yield
        v2b = each(heads, lambda bi, h: tile(gv_ref, bi, hsl(h)).astype(BF16))
        qs = each(heads, lambda bi, h: _dot(jnp.where(inhead[h % per], QA[bi][:, psl(h // per)], 0), S0p[bi, h // per]))
        yield
        o = each(heads, lambda bi, h: _dot(scores[bi, h], v2b[bi, h]) + qs[bi, h])
        yield
        gla_out = each(heads, lambda bi, h: (_headnorm(o[bi, h], gln_ref[:, hsl(h)])
                                             * _silu(tile(gg_ref, bi, hsl(h)))))
        yield
        store(gla_ref, gla_out, BI)
        yield
        upd = each(heads, lambda bi, h: _dot_tn(kd[bi][:, psl(h // per)], v2b[bi, h]))
        yield
        for bi, p in pairs:
            ss[bi, psl(p), :] = (dec_col[bi, p] * S0p[bi, p]
                                 + _select_chain(rowp, [upd[bi, p * per + u] for u in range(per)], dk))
        yield

    per_tile = L // valid
    if valid == L and bt >= 4:
        gens = []
        for k in range(bt):
            gk = range(k, k + 1)
            gens.append(itertools.chain(mlstm(gk), gla(gk)) if k % 2 == 0 else itertools.chain(gla(gk), mlstm(gk)))
        _interleave(*gens)
    elif bt % 2 == 0 and (bt // 2) % per_tile == 0:
        ga, gb = range(bt // 2), range(bt // 2, bt)
        _interleave(itertools.chain(mlstm(ga), gla(ga)), itertools.chain(gla(gb), mlstm(gb)))
    else:
        _interleave(mlstm(range(bt)), gla(range(bt)))

    @pl.when(c == nc - 1)
    def _():
        c_ref[0] = cs[...].reshape(bt, H, dk, dv)
        n_ref[0] = ns[...]
        m_ref[0] = ms[...]
        s_ref[0] = ss[...].reshape(bt, H, dk, dv)


def _sel_rows(r):
    return jnp.asarray(np.eye(r, LANES, dtype=np.float32), dtype=BF16)


def _mix_dtype(L, valid):
    return BF16 if valid == L else F32


def _row_blocks(B, nc, L, valid, bt):
    if valid == L:
        return (B, nc * L), lambda w, k: pl.BlockSpec((bt, L, w), lambda b, c: (b, c, k))
    assert nc == 1 and L % valid == 0 and (bt * valid) % L == 0
    return (B // bt, bt * valid), lambda w, k: pl.BlockSpec((1, bt * valid, w), lambda b, c: (b, 0, k))


def _ab_scan(P, B, nc, L, valid, bt, j, states, prev, prm):
    C0, n0, m0, S0 = states
    nst, _, H, dk, dv = C0.shape
    HK = H * dk
    lead, blk = _row_blocks(B, nc, L, valid, bt)
    assert P.shape[:2] == lead and dv == LANES and LANES % dk == 0 and B % bt == 0
    tmat, nlev = _ab_mats(L)
    wv = H * dv
    assert 2 * HK == wv
    nsm = 6 * wv // LANES
    sel = _sel_rows(2 * H)
    sec = lambda k: blk(wv, k)
    st_c = pl.BlockSpec((1, bt, H, dk, dv), lambda b, c: (j, b, 0, 0, 0))
    st_n = pl.BlockSpec((1, bt, 1, HK), lambda b, c: (j, b, 0, 0))
    st_m = pl.BlockSpec((1, bt, 1, H), lambda b, c: (j, b, 0, 0))
    full = lambda a: pl.BlockSpec(a.shape, lambda b, c: (0,) * a.ndim)
    n_alias = 0 if prev is None else 4
    consts = [prm["gate_bias"], prm["ml_norm"], prm["wa2"], prm["ba"], prm["gla_norm"], tmat, sel]
    in_specs = ([sec(k) for k in range(6)]
                + [blk(LANES, nsm), st_c, st_n, st_m, st_c]
                + [full(a) for a in consts]
                + [pl.BlockSpec(memory_space=pl.ANY)] * n_alias)
    out_specs = [sec(0), sec(0), st_c, st_n, st_m, st_c]
    out_shape = [
        jax.ShapeDtypeStruct(lead + (wv,), _mix_dtype(L, valid)),
        jax.ShapeDtypeStruct(lead + (wv,), _mix_dtype(L, valid)),
        jax.ShapeDtypeStruct(C0.shape, F32), jax.ShapeDtypeStruct(n0.shape, F32),
        jax.ShapeDtypeStruct(m0.shape, F32), jax.ShapeDtypeStruct(S0.shape, F32),
    ]
    n_in = len(in_specs) - n_alias
    aliases = {n_in + k: 2 + k for k in range(n_alias)}
    args = [P] * 7 + [C0, n0, m0, S0] + consts + (list(prev) if prev is not None else [])
    return pl.pallas_call(
        functools.partial(_ab_kernel, L, valid, nlev, nc, bt, H, dk, dv, n_alias),
        grid=(B // bt, nc),
        in_specs=in_specs,
        out_specs=out_specs,
        out_shape=out_shape,
        input_output_aliases=aliases,
        scratch_shapes=[pltpu.VMEM((bt, HK, dv), F32), pltpu.VMEM((bt, 1, HK), F32),
                        pltpu.VMEM((bt, 1, H), F32), pltpu.VMEM((bt, HK, dv), F32)],
        compiler_params=pltpu.CompilerParams(
            dimension_semantics=("parallel", "arbitrary"), vmem_limit_bytes=VMEM_LIMIT),
        name="ab_scan",
    )(*args)


def _ssd_kernel(L, valid, nc, bt, G, hpg, hd, wave, preact, n_alias,
                p_ref, cv_ref, h0_ref,
                cw_ref, cb_ref, dtbr_ref, dtbc_ref, alr_ref, alc_ref, dsk_ref, nrm_ref,
                tmat_ref, exp_ref, sel_ref, *rest):
    y_ref, h_ref, catx, catb, catc, hs = rest[n_alias:]
    cidx = pl.program_id(1)
    gw = hpg * hd
    nh = G * hpg
    inner = nh * hd
    N = h0_ref.shape[4]
    W = cw_ref.shape[0]
    P0 = SUBLANES - (W - 1)
    per = LANES // hd
    cats = ((catx, 0, inner), (catb, inner, G * N), (catc, inner + G * N, G * N))

    lay = _SsdLayout(inner, G * N)

    @pl.when(cidx == 0)
    def _():
        if not preact:
            for cat, off, wid in cats:
                cat[:, P0:SUBLANES, :] = cv_ref[0, :, :, off:off + wid]
        hs[...] = h0_ref[0].reshape(bt, inner, N)

    t_col = lax.broadcasted_iota(jnp.int32, (L, 1), 0)
    s_row = lax.broadcasted_iota(jnp.int32, (1, L), 1)
    tt = lax.broadcasted_iota(jnp.int32, (L, L), 0)
    sc = lax.broadcasted_iota(jnp.int32, (L, L), 1)
    causal = sc <= tt
    lane = lax.broadcasted_iota(jnp.int32, (L, LANES), 1)
    upper = tmat_ref[0:L, :]
    lower = tmat_ref[L:2 * L, :]

    def conv_silu(cat, bi, off, lo, wid):
        xall = cat[bi, :, lo:lo + wid]
        cols = slice(off + lo, off + lo + wid)
        acc = cb_ref[:, cols] + xall[SUBLANES:SUBLANES + L] * cw_ref[W - 1:W, cols]
        for w in range(W - 1):
            tap = pltpu.roll(xall, W - 1 - w, axis=0)[SUBLANES:SUBLANES + L]
            acc = acc + tap * cw_ref[w:w + 1, cols]
        return _silu(acc)

    prow = lambda bi, off, wid: _seq_tile(p_ref, bi, slice(off, off + wid), L, valid)
    if not preact:
        assert nc == 1
        for bi in range(bt):
            for g in range(G):
                catx[bi, SUBLANES:SUBLANES + L, g * gw:(g + 1) * gw] = prow(bi, lay.xbc(g * gw), gw)
            catb[bi, SUBLANES:SUBLANES + L, :] = prow(bi, lay.xbc(inner), G * N)
            catc[bi, SUBLANES:SUBLANES + L, :] = prow(bi, lay.xbc(inner + G * N), G * N)

    BI = range(bt)
    each = lambda keys, f: {k: f(*k) for k in keys}
    gsl = lambda g: slice(g * gw, (g + 1) * gw)
    dt_raw = [prow(bi, lay.dt, LANES) for bi in BI]
    dtc = [_softplus(x + dtbr_ref[...]) for x in dt_raw]
    dtr = [_softplus(_rows_of_transpose(sel_ref[...], x) + dtbc_ref[...]) for x in dt_raw]
    if valid < L:
        dtc = [jnp.where(t_col < valid, x, 0.0) for x in dtc]
        dtr = [jnp.where(s_row < valid, x, 0.0) for x in dtr]
    cs_col = [_dot3(lower, x * (-jnp.exp(alr_ref[...]))) for x in dtc]
    cs_row = [_dot3_right(x * (-jnp.exp(alc_ref[...])), upper) for x in dtr]
    cd3 = [_split3(jnp.concatenate([cs_col[bi], dtc[bi]], axis=0)) for bi in BI]

    problems = [(bi, g) for bi in BI for g in range(G)]
    for w0 in range(0, len(problems), wave):
        keys = problems[w0:w0 + wave]
        hkeys = [(bi, g, jl) for bi, g in keys for jl in range(hpg)]
        if preact:
            xa = each(keys, lambda bi, g: prow(bi, lay.xbc(g * gw), gw))
            Bm = each(keys, lambda bi, g: prow(bi, lay.xbc(inner + g * N), N))
            Cm = each(keys, lambda bi, g: prow(bi, lay.xbc(inner + G * N + g * N), N))
            zg = each(keys, lambda bi, g: prow(bi, lay.z(g * gw), gw))
        else:
            xa = each(keys, lambda bi, g: conv_silu(catx, bi, 0, g * gw, gw))
            Bm = each(keys, lambda bi, g: conv_silu(catb, bi, inner, g * N, N))
            Cm = each(keys, lambda bi, g: conv_silu(catc, bi, inner + G * N, g * N, N))
            zg = each(keys, lambda bi, g: _silu(prow(bi, lay.z(g * gw), gw)))
        ce = each(keys, lambda bi, g: (
            jnp.dot(cd3[bi][0], exp_ref[:, gsl(g)], preferred_element_type=F32)
            + jnp.dot(cd3[bi][1], exp_ref[:, gsl(g)], preferred_element_type=F32)
            + jnp.dot(cd3[bi][2], exp_ref[:, gsl(g)], preferred_element_type=F32)))
        cs_exp = each(keys, lambda bi, g: ce[bi, g][0:L])
        dt_exp = each(keys, lambda bi, g: ce[bi, g][L:2 * L])
        Bmb = each(keys, lambda bi, g: Bm[bi, g].astype(BF16))
        Cmb = each(keys, lambda bi, g: Cm[bi, g].astype(BF16))
        CB = each(keys, lambda bi, g: _dot_nt(Cmb[bi, g], Bmb[bi, g]))
        hs0 = each(keys, lambda bi, g: hs[bi, gsl(g), :])
        yc = each(keys, lambda bi, g: _dot_nt(Cmb[bi, g], hs0[bi, g]) * jnp.exp(cs_exp[bi, g]))
        xab = each(keys, lambda bi, g: xa[bi, g].astype(BF16))
        seg = each(hkeys, lambda bi, g, jl: jnp.where(
            causal, cs_col[bi][:, g * hpg + jl:g * hpg + jl + 1] - cs_row[bi][g * hpg + jl:g * hpg + jl + 1, :],
            -jnp.inf))
        mj = each(hkeys, lambda bi, g, jl: CB[bi, g] * (
            jnp.exp(seg[bi, g, jl]) * dtr[bi][g * hpg + jl:g * hpg + jl + 1, :]))
        yh = each(hkeys, lambda bi, g, jl: _dot(
            mj[bi, g, jl], xab[bi, g][:, (jl // per) * LANES:(jl // per + 1) * LANES]))
        yi = each(keys, lambda bi, g: jnp.concatenate(
            [_select_chain(lane, [yh[bi, g, p * per + u] for u in range(per)], hd)
             for p in range(gw // LANES)], axis=1))
        y = each(keys, lambda bi, g: yc[bi, g] + yi[bi, g] + dsk_ref[:, gsl(g)] * xa[bi, g])
        y = each(keys, lambda bi, g: y[bi, g] * zg[bi, g])
        y = each(keys, lambda bi, g: y[bi, g] * lax.rsqrt(
            jnp.mean(y[bi, g] * y[bi, g], axis=1, keepdims=True) + LN_EPS) * nrm_ref[:, gsl(g)])
        if valid == L:
            for bi, g in keys:
                y_ref[bi, :, gsl(g)] = y[bi, g].astype(y_ref.dtype)
        else:
            assert len(keys) == bt * G
            for g in range(G):
                _store_seqs(y_ref, gsl(g), [y[bi, g] for bi in BI], L, valid)

        xw = each(keys, lambda bi, g: xa[bi, g] * (
            jnp.exp(cs_exp[bi, g][L - 1:L, :] - cs_exp[bi, g]) * dt_exp[bi, g]))
        upd = each(keys, lambda bi, g: _dot_tn(xw[bi, g], Bmb[bi, g]))
        dec = each(hkeys, lambda bi, g, jl: jnp.exp(cs_row[bi][g * hpg + jl:g * hpg + jl + 1, L - 1:L]))
        for bi, g, jl in hkeys:
            r0 = g * gw + jl * hd
            hs[bi, r0:r0 + hd, :] = (dec[bi, g, jl] * hs0[bi, g][jl * hd:(jl + 1) * hd, :]
                                     + upd[bi, g][jl * hd:(jl + 1) * hd, :])

    @pl.when(cidx == nc - 1)
    def _():
        h_ref[0] = hs[...].reshape(h_ref.shape[1:])


def _ssd_scan(P, B, nc, L, valid, bt, j, h0, conv0, prev, prm):
    nst, _, nh, hd, N = h0.shape
    G = prm["groups"]
    hpg = nh // G
    gw = hpg * hd
    inner = nh * hd
    GN = G * N
    W1, cd = conv0.shape[2], conv0.shape[3]
    lead, blk = _row_blocks(B, nc, L, valid, bt)
    assert P.shape[:2] == lead and N == LANES and gw % LANES == 0 and B % bt == 0
    assert nh <= LANES and inner % GN == 0 and cd == inner + 2 * GN
    tmat = _ssd_mats(L)
    sel = _sel_rows(nh)
    wave = bt * G if L <= 2 * SUBLANES else 2
    lay = _SsdLayout(inner, GN)
    assert P.shape[2] == 2 * lay.tile and lay.half % gw == 0 and lay.zhalf % gw == 0
    preact = valid == L
    crows = SUBLANES if preact else SUBLANES + L
    full = lambda a: pl.BlockSpec(a.shape, lambda b, c: (0,) * a.ndim)
    st_h = pl.BlockSpec((1, bt, nh, hd, N), lambda b, c: (j, b, 0, 0, 0))
    n_alias = 0 if prev is None else 1
    consts = [prm["conv_w"], prm["conv_b"], prm["dtb_row"], prm["dtb_col"], prm["alog_row"],
              prm["alog_col"], prm["dskip"], prm["norm"], tmat, prm["expand"], sel]
    in_specs = [
        blk(2 * lay.tile, 0),
        pl.BlockSpec((1, bt, W1, cd), lambda b, c: (j, b, 0, 0)),
        st_h,
    ] + [full(a) for a in consts] + [pl.BlockSpec(memory_space=pl.ANY)] * n_alias
    out_specs = [blk(inner, 0), st_h]
    out_shape = [jax.ShapeDtypeStruct(lead + (inner,), _mix_dtype(L, valid)),
                 jax.ShapeDtypeStruct(h0.shape, F32)]
    n_in = len(in_specs) - n_alias
    args = [P, conv0, h0] + consts + ([prev] if prev is not None else [])
    return pl.pallas_call(
        functools.partial(_ssd_kernel, L, valid, nc, bt, G, hpg, hd, wave, preact, n_alias),
        grid=(B // bt, nc),
        in_specs=in_specs,
        out_specs=out_specs,
        out_shape=out_shape,
        input_output_aliases={n_in: 1} if n_alias else {},
        scratch_shapes=[pltpu.VMEM((bt, crows, inner), F32), pltpu.VMEM((bt, crows, GN), F32),
                        pltpu.VMEM((bt, crows, GN), F32), pltpu.VMEM((bt, inner, N), F32)],
        compiler_params=pltpu.CompilerParams(
            dimension_semantics=("parallel", "arbitrary"), vmem_limit_bytes=VMEM_LIMIT),
        name="ssd_scan",
    )(*args)


def _prep_ab(j, H, dk, dv, w_in, ig_bias, fg_bias, ml_norm, wa2, ba, gla_norm, w_out):
    rank = wa2.shape[1]
    qk, vv = H * dk, H * dv
    o = np.cumsum([0, qk, qk, vv, vv, H, H, qk, qk, vv, vv, rank])
    assert int(o[-1]) == w_in.shape[2]
    wj = w_in[j]
    small = jnp.concatenate([wj[:, o[4]:o[6]], wj[:, o[10]:o[11]]], axis=1)
    small = jnp.pad(small, ((0, 0), (0, LANES - small.shape[1])))
    w = jnp.concatenate([wj[:, :o[4]], wj[:, o[6]:o[10]], small], axis=1).astype(BF16)
    wa2p = jnp.zeros((LANES, qk), F32).at[2 * H:2 * H + rank, :].set(wa2[j]).astype(BF16)
    return {
        "w_in": w,
        "gate_bias": jnp.concatenate([ig_bias[j], fg_bias[j]]).astype(F32).reshape(2 * H, 1),
        "ml_norm": ml_norm[j].reshape(1, vv),
        "gla_norm": gla_norm[j].reshape(1, vv),
        "wa2": wa2p,
        "ba": ba[j].reshape(1, qk),
        "w_out_ml": w_out[j][:vv].astype(BF16),
        "w_out_gla": w_out[j][vv:].astype(BF16),
    }


def _prep_ssd(j, G, nh, hd, N, w_in, conv_w, conv_b, dt_bias, a_log, d_skip, norm_g, w_out):
    inner = nh * hd
    hpg = nh // G
    cd = inner + 2 * G * N
    wz, wxbc, wdt = w_in[j][:, :inner], w_in[j][:, inner:inner + cd], w_in[j][:, inner + cd:]
    lay = _SsdLayout(inner, G * N)
    wdt = jnp.pad(wdt, ((0, 0), (0, LANES - nh)))
    w = jnp.concatenate([wxbc[:, :lay.half], wz[:, :lay.zhalf], wdt,
                         wxbc[:, lay.half:], wz[:, lay.zhalf:], jnp.zeros_like(wdt)], axis=1).astype(BF16)
    lane_form = lambda v: jnp.pad(v, (0, LANES - nh)).reshape(1, LANES)
    e = np.zeros((LANES, inner), np.float32)
    for r in range(nh):
        e[r, r * hd:(r + 1) * hd] = 1.0
    return {
        "groups": G,
        "w_in": w,
        "conv_w": conv_w[j], "conv_b": conv_b[j].reshape(1, cd),
        "dtb_row": lane_form(dt_bias[j]), "dtb_col": dt_bias[j].reshape(nh, 1),
        "alog_row": lane_form(a_log[j]), "alog_col": a_log[j].reshape(nh, 1),
        "dskip": jnp.repeat(d_skip[j], hd).reshape(1, inner),
        "norm": norm_g[j].reshape(1, inner),
        "expand": jnp.asarray(e, dtype=BF16),
        "w_out": w_out[j].astype(BF16),
    }


def _largest_divisor(n, cap, step=1):
    return max(d for d in range(step, cap + 1, step) if n % d == 0)


def _plan(B, T, chunk):
    if T % chunk == 0:
        return chunk, T // chunk, _largest_divisor(B, 8), _largest_divisor(B, 4)
    L = max(SUBLANES, 1 << int(math.ceil(math.log2(T))))
    per = L // T
    assert L % T == 0 and B % per == 0
    return L, 1, _largest_divisor(B, 8, per), _largest_divisor(B, 4, per)


def _trunk(x, states, ab_prm, ssd_prm, mlp, lns, alpha, chunk):
    B, T, D = x.shape
    mC, mn, mm, gS, sh, sconv = states
    n_ab, _, H, dk, dv = mC.shape
    L, nc, bt_ab, bt_ssd = _plan(B, T, chunk)
    valid = T if nc == 1 else L
    M = B * T
    tm = 512 if M % 512 == 0 else M
    tmm = 1024 if M % 1024 == 0 else tm
    X = x.reshape(M, D)
    ab_states = (mC, mn.reshape(n_ab, B, 1, H * dk), mm.reshape(n_ab, B, 1, H), gS)
    ab_out, h_out, ncv = None, None, []
    depth = mlp[0].shape[0]
    for l in range(depth):
        j = l // 2
        if l % 2 == 0:
            p = ab_prm[j]
            P = _proj(X, p["w_in"], tm, p["w_in"].shape[1])
            lead = _row_blocks(B, nc, L, valid, bt_ab)[0]
            res = _ab_scan(P.reshape(lead + (-1,)), B, nc, L, valid, bt_ab, j, ab_states, ab_out, p)
            ab_out = res[2:]
            ys = [res[0].reshape(M, -1), res[1].reshape(M, -1)]
            ws = [p["w_out_ml"], p["w_out_gla"]]
        else:
            p = ssd_prm[j]
            inner = p["norm"].shape[1]
            cd = p["conv_b"].shape[1]
            lay = _SsdLayout(inner, (cd - inner) // 2)
            W1 = sconv.shape[2]
            assert T >= W1
            if valid == L:
                tmc = max(t for t in (128, 256, 512) if T % t == 0)
                P, tail = _proj_conv(X, p["w_in"], p["conv_w"], p["conv_b"], sconv, j, B, tmc, lay)
                ncv.append(tail[:, SUBLANES - W1:])
            else:
                P = _proj(X, p["w_in"], tm, lay.tile)
                last = P.reshape(B, T, -1)[:, T - W1:]
                ncv.append(jnp.concatenate([last[:, :, :lay.half], last[:, :, lay.tile:lay.tile + lay.half]], axis=2))
            lead = _row_blocks(B, nc, L, valid, bt_ssd)[0]
            y, h_out = _ssd_scan(P.reshape(lead + (-1,)), B, nc, L, valid, bt_ssd, j, sh, sconv, h_out, p)
            ys = [y.reshape(M, -1)]
            ws = [p["w_out"]]
        X = _outproj_ln(ys, ws, X, lns[0], lns[1], l, alpha, tmm)
        X = _mlp_ln(X, mlp[0], mlp[1], lns[2], lns[3], l, alpha, tmm, 1024)
    nC, nn_, nm, nS = ab_out
    return (X.reshape(B, T, D), nC, nn_.reshape(mn.shape), nm.reshape(mm.shape), nS, h_out, jnp.stack(ncv))


def kernel(x_prompt, x_sample, state_mlstm_C, state_mlstm_n, state_mlstm_m, state_gla_S, state_ssd_h,
           state_ssd_conv, ab_w_in, ab_ig_bias, ab_fg_bias, ab_ml_norm, ab_gla_wa2, ab_gla_ba, ab_gla_norm,
           ab_w_out, ssd_w_in, ssd_conv_w, ssd_conv_b, ssd_dt_bias, ssd_a_log, ssd_d, ssd_norm, ssd_w_out,
           mlp_w1, mlp_w2, ln_mix_g, ln_mix_b, ln_mlp_g, ln_mlp_b):
    depth = mlp_w1.shape[0]
    D = x_prompt.shape[2]
    alpha = (2 * depth) ** 0.25
    n_ab, _, H, dk, dv = state_mlstm_C.shape
    n_ssd, _, nh, hd, N = state_ssd_h.shape
    cd = state_ssd_conv.shape[3]
    G = (cd - nh * hd) // (2 * N)
    ab_prm = [_prep_ab(j, H, dk, dv, ab_w_in, ab_ig_bias, ab_fg_bias, ab_ml_norm, ab_gla_wa2,
                       ab_gla_ba, ab_gla_norm, ab_w_out) for j in range(n_ab)]
    ssd_prm = [_prep_ssd(j, G, nh, hd, N, ssd_w_in, ssd_conv_w, ssd_conv_b, ssd_dt_bias, ssd_a_log,
                         ssd_d, ssd_norm, ssd_w_out) for j in range(n_ssd)]
    mlp = (mlp_w1, mlp_w2)
    lns = tuple(a.reshape(depth, 1, D) for a in (ln_mix_g, ln_mix_b, ln_mlp_g, ln_mlp_b))

    Bp = x_prompt.shape[0]
    zeros = (jnp.zeros((n_ab, Bp, H, dk, dv), F32), jnp.zeros((n_ab, Bp, H, dk), F32),
             jnp.zeros((n_ab, Bp, H), F32), jnp.zeros((n_ab, Bp, H, dk, dv), F32),
             jnp.zeros((n_ssd, Bp, nh, hd, N), F32), jnp.zeros((n_ssd, Bp) + state_ssd_conv.shape[2:], F32))
    carried = (state_mlstm_C, state_mlstm_n, state_mlstm_m, state_gla_S, state_ssd_h, state_ssd_conv)
    chunk = 128
    yp = _trunk(x_prompt, zeros, ab_prm, ssd_prm, mlp, lns, alpha, chunk)
    ys = _trunk(x_sample, carried, ab_prm, ssd_prm, mlp, lns, alpha, chunk)
    return (yp[0], ys[0]) + yp[1:] + ys[1:]
```

```python
import functools
import itertools
import math

import jax
import jax.numpy as jnp
import numpy as np
from jax import lax
from jax.experimental import pallas as pl
from jax.experimental.pallas import tpu as pltpu

F32 = jnp.float32
BF16 = jnp.bfloat16

LN_EPS = 1e-5
GLA_TAU = 16.0
LANES = 128
SUBLANES = 8
VMEM_LIMIT = 48 * 1024 * 1024


def _dot(a, b):
    return jnp.dot(a.astype(BF16), b.astype(BF16), preferred_element_type=F32)


def _dot_nt(a, b):
    return lax.dot_general(a.astype(BF16), b.astype(BF16), (((1,), (1,)), ((), ())),
                           preferred_element_type=F32)


def _dot_tn(a, b):
    return lax.dot_general(a.astype(BF16), b.astype(BF16), (((0,), (0,)), ((), ())),
                           preferred_element_type=F32)


def _split3(x):
    hi = x.astype(BF16)
    r = x - hi.astype(F32)
    mid = r.astype(BF16)
    lo = (r - mid.astype(F32)).astype(BF16)
    return hi, mid, lo


def _dot3(t, x):
    hi, mid, lo = _split3(x)
    f = lambda p: jnp.dot(t, p, preferred_element_type=F32)
    return f(hi) + f(mid) + f(lo)


def _dot3_right(x, t):
    hi, mid, lo = _split3(x)
    f = lambda p: jnp.dot(p, t, preferred_element_type=F32)
    return f(hi) + f(mid) + f(lo)


def _rows_of_transpose(sel, x):
    hi, mid, lo = _split3(x)
    f = lambda p: lax.dot_general(sel, p, (((1,), (1,)), ((), ())), preferred_element_type=F32)
    return f(hi) + f(mid) + f(lo)


def _softplus(x):
    e = jnp.exp(-jnp.abs(x))
    u = 1.0 + e
    d = u - 1.0
    return jnp.maximum(x, 0.0) + jnp.where(d == 0.0, e, jnp.log(u) * (e / d))


def _log_sigmoid(x):
    return -_softplus(-x)


def _sigmoid(x):
    return 0.5 + 0.5 * jnp.tanh(0.5 * x)


def _silu(x):
    h = 0.5 * x
    return h + h * jnp.tanh(h)


def _seq_tile(ref, bi, cols, L, T):
    if T == L:
        return ref[bi, :, cols]
    r0 = bi * T
    a = (r0 // L) * L
    tile = ref[0, a:a + L, cols]
    return tile if r0 == a else pltpu.roll(tile, L - (r0 - a), axis=0)


def _store_seqs(ref, cols, vals, L, T, first=0):
    if T == L:
        for i, v in enumerate(vals):
            ref[first + i, :, cols] = v.astype(ref.dtype)
        return
    per = L // T
    assert first % per == 0 and len(vals) % per == 0
    row = lax.broadcasted_iota(jnp.int32, vals[0].shape, 0)
    for a in range(len(vals) // per):
        out = vals[a * per]
        for k in range(1, per):
            out = jnp.where(row < k * T, out, pltpu.roll(vals[a * per + k], k * T, axis=0))
        r0 = (first // per + a) * L
        ref[0, r0:r0 + L, cols] = out


def _interleave(*gens):
    gens = list(gens)
    while gens:
        for g in list(gens):
            try:
                next(g)
            except StopIteration:
                gens.remove(g)


def _layernorm_rows(r, g, b):
    mu = jnp.mean(r, axis=1, keepdims=True)
    d = r - mu
    var = jnp.mean(d * d, axis=1, keepdims=True)
    return d * lax.rsqrt(var + LN_EPS) * g + b


def _headnorm(h, g):
    mu = jnp.mean(h, axis=1, keepdims=True)
    d = h - mu
    var = jnp.mean(d * d, axis=1, keepdims=True)
    return d * lax.rsqrt(var + LN_EPS) * g


def _select_chain(idx, pieces, width):
    out = pieces[0]
    for u in range(1, len(pieces)):
        out = jnp.where(idx < u * width, out, pieces[u])
    return out


def _cumsum_mats(L):
    t = np.arange(L)[:, None]
    j = np.arange(L)[None, :]
    upper = (t <= j)
    lower = (j <= t)
    return upper, lower


def _ab_mats(L):
    nlev = int(round(math.log2(L)))
    assert 1 << nlev == L
    upper, lower = _cumsum_mats(L)
    t = np.arange(L)[:, None]
    j = np.arange(L)[None, :]
    mats = [upper, lower]
    for i in range(nlev):
        n = L >> (i + 1)
        mid = (t // (2 * n)) * (2 * n) + n - 1
        second = (t % (2 * n)) >= n
        m = np.where(second, (j > mid) & (j <= t), (j > t) & (j <= mid))
        mats.append(m)
    return jnp.asarray(np.concatenate(mats, axis=0).astype(np.float32), dtype=BF16), nlev


def _ssd_mats(L):
    upper, lower = _cumsum_mats(L)
    return jnp.asarray(np.concatenate([upper, lower], axis=0).astype(np.float32), dtype=BF16)


def _proj_kernel(x_ref, w_ref, o_ref):
    o_ref[...] = jnp.dot(x_ref[...].astype(BF16), w_ref[...], preferred_element_type=F32)


def _proj(x, w, tm, tn):
    M, K = x.shape
    N = w.shape[1]
    assert M % tm == 0 and N % tn == 0
    return pl.pallas_call(
        _proj_kernel,
        grid=(N // tn, M // tm),
        in_specs=[pl.BlockSpec((tm, K), lambda j, i: (i, 0)),
                  pl.BlockSpec((K, tn), lambda j, i: (0, j))],
        out_specs=pl.BlockSpec((tm, tn), lambda j, i: (i, j)),
        out_shape=jax.ShapeDtypeStruct((M, N), F32),
        compiler_params=pltpu.CompilerParams(
            dimension_semantics=("parallel", "parallel"), vmem_limit_bytes=VMEM_LIMIT),
        name="proj",
    )(x, w)


class _SsdLayout:
    def __init__(self, inner, gn):
        self.cd = inner + 2 * gn
        self.half = self.cd // 2
        self.zhalf = inner // 2
        self.tile = self.half + self.zhalf + LANES
        self.dt = self.half + self.zhalf
        assert self.cd % (2 * LANES) == 0 and inner % (2 * LANES) == 0

    def xbc(self, c):
        return c if c < self.half else self.tile + c - self.half

    def z(self, c):
        return self.half + c if c < self.zhalf else self.tile + self.half + c - self.zhalf


def _proj_conv_kernel(tps, nrb, half, zhalf, x_ref, w_ref, cw_ref, cb_ref, cv_ref, o_ref, tail_ref, hist):
    i = pl.program_id(1)
    W = cw_ref.shape[0]
    P0 = SUBLANES - (W - 1)
    rb = x_ref.shape[0] // nrb

    @pl.when(i % tps == 0)
    def _():
        hist[...] = jnp.zeros_like(hist)
        hist[P0:SUBLANES, :] = cv_ref[0, 0]

    def raw_rows(r):
        return jnp.dot(x_ref[r * rb:(r + 1) * rb, :].astype(BF16), w_ref[...], preferred_element_type=F32)

    def finish(r, raw, prev):
        rows = slice(r * rb, (r + 1) * rb)
        xall = jnp.concatenate([prev, raw[:, :half]], axis=0)
        acc = cb_ref[...] + xall[SUBLANES:] * cw_ref[W - 1:W, :]
        for w in range(W - 1):
            acc = acc + pltpu.roll(xall, W - 1 - w, axis=0)[SUBLANES:] * cw_ref[w:w + 1, :]
        o_ref[rows, :half] = _silu(acc)
        o_ref[rows, half:half + zhalf] = _silu(raw[:, half:half + zhalf])
        o_ref[rows, half + zhalf:] = raw[:, half + zhalf:]
        return raw[rb - SUBLANES:, :half]

    prev = hist[...]
    raw = raw_rows(0)
    for r in range(nrb):
        nxt = raw_rows(r + 1) if r + 1 < nrb else None
        prev = finish(r, raw, prev)
        raw = nxt
    hist[...] = prev
    tail_ref[0] = prev


def _proj_conv(x, w, conv_w, conv_b, conv0, j, B, tm, lay):
    M, K = x.shape
    T = M // B
    assert M % tm == 0 and T % tm == 0 and w.shape[1] == 2 * lay.tile and (tm // 4) % SUBLANES == 0
    tps = T // tm
    W1 = conv0.shape[2]
    return pl.pallas_call(
        functools.partial(_proj_conv_kernel, tps, 4, lay.half, lay.zhalf),
        grid=(2, M // tm),
        in_specs=[pl.BlockSpec((tm, K), lambda c, i: (i, 0)),
                  pl.BlockSpec((K, lay.tile), lambda c, i: (0, c)),
                  pl.BlockSpec((W1 + 1, lay.half), lambda c, i: (0, c)),
                  pl.BlockSpec((1, lay.half), lambda c, i: (0, c)),
                  pl.BlockSpec((1, 1, W1, lay.half), lambda c, i: (j, i // tps, 0, c))],
        out_specs=[pl.BlockSpec((tm, lay.tile), lambda c, i: (i, c)),
                   pl.BlockSpec((1, SUBLANES, lay.half), lambda c, i: (i // tps, 0, c))],
        out_shape=[jax.ShapeDtypeStruct((M, 2 * lay.tile), F32),
                   jax.ShapeDtypeStruct((B, SUBLANES, lay.cd), F32)],
        scratch_shapes=[pltpu.VMEM((SUBLANES, lay.half), F32)],
        compiler_params=pltpu.CompilerParams(
            dimension_semantics=("parallel", "arbitrary"), vmem_limit_bytes=VMEM_LIMIT),
        name="proj_conv",
    )(x, w, conv_w, conv_b, conv0)


def _outproj_ln_kernel(alpha, n_in, *refs):
    ys = refs[:n_in]
    ws = refs[n_in:2 * n_in]
    x_ref, g_ref, b_ref, o_ref = refs[2 * n_in:]
    tm = x_ref.shape[0]
    nrb = 4 if tm % (4 * SUBLANES) == 0 else 1
    rb = tm // nrb

    def mix(k):
        rows = slice(k * rb, (k + 1) * rb)
        r = alpha * x_ref[rows, :]
        for y_ref, w_ref in zip(ys, ws):
            r = r + jnp.dot(y_ref[rows, :].astype(BF16), w_ref[...], preferred_element_type=F32)
        return r

    r = mix(0)
    for k in range(nrb):
        nxt = mix(k + 1) if k + 1 < nrb else None
        o_ref[k * rb:(k + 1) * rb, :] = _layernorm_rows(r, g_ref[0], b_ref[0])
        r = nxt


def _outproj_ln(ys, ws, x, g, b, l, alpha, tm):
    M, D = x.shape
    assert M % tm == 0
    n_in = len(ys)
    in_specs = ([pl.BlockSpec((tm, y.shape[1]), lambda i: (i, 0)) for y in ys]
                + [pl.BlockSpec(w.shape, lambda i: (0, 0)) for w in ws]
                + [pl.BlockSpec((tm, D), lambda i: (i, 0)),
                   pl.BlockSpec((1, 1, D), lambda i: (l, 0, 0)),
                   pl.BlockSpec((1, 1, D), lambda i: (l, 0, 0))])
    return pl.pallas_call(
        functools.partial(_outproj_ln_kernel, alpha, n_in),
        grid=(M // tm,),
        in_specs=in_specs,
        out_specs=pl.BlockSpec((tm, D), lambda i: (i, 0)),
        out_shape=jax.ShapeDtypeStruct((M, D), F32),
        compiler_params=pltpu.CompilerParams(
            dimension_semantics=("parallel",), vmem_limit_bytes=VMEM_LIMIT),
        name="outproj_ln",
    )(*ys, *ws, x, g, b)


def _mlp_kernel(alpha, nf, x_ref, w1_ref, w2_ref, g_ref, b_ref, o_ref, acc_ref):
    f = pl.program_id(1)

    h = jnp.dot(x_ref[...].astype(BF16), w1_ref[0].astype(BF16), preferred_element_type=F32)
    h = jnp.square(jnp.maximum(h, 0.0))
    d = jnp.dot(h.astype(BF16), w2_ref[0].astype(BF16), preferred_element_type=F32)

    @pl.when(f == 0)
    def _():
        acc_ref[...] = d

    @pl.when(f > 0)
    def _():
        acc_ref[...] += d

    @pl.when(f == nf - 1)
    def _():
        r = alpha * x_ref[...] + acc_ref[...]
        o_ref[...] = _layernorm_rows(r, g_ref[0], b_ref[0])


def _mlp_ln(x, w1, w2, g, b, l, alpha, tm, tf):
    M, D = x.shape
    Fdim = w1.shape[2]
    assert M % tm == 0 and Fdim % tf == 0
    nf = Fdim // tf
    return pl.pallas_call(
        functools.partial(_mlp_kernel, alpha, nf),
        grid=(M // tm, nf),
        in_specs=[pl.BlockSpec((tm, D), lambda i, f: (i, 0)),
                  pl.BlockSpec((1, D, tf), lambda i, f: (l, 0, f)),
                  pl.BlockSpec((1, tf, D), lambda i, f: (l, f, 0)),
                  pl.BlockSpec((1, 1, D), lambda i, f: (l, 0, 0)),
                  pl.BlockSpec((1, 1, D), lambda i, f: (l, 0, 0))],
        out_specs=pl.BlockSpec((tm, D), lambda i, f: (i, 0)),
        out_shape=jax.ShapeDtypeStruct((M, D), F32),
        scratch_shapes=[pltpu.VMEM((tm, D), F32)],
        compiler_params=pltpu.CompilerParams(
            dimension_semantics=("parallel", "arbitrary"), vmem_limit_bytes=VMEM_LIMIT),
        name="mlp_ln",
    )(x, w1, w2, g, b)


def _ab_kernel(L, valid, nlev, nc, bt, H, dk, dv, n_alias,
               qk_ref, v_ref, mo_ref, gqk_ref, gv_ref, gg_ref, sm_ref,
               c0_ref, n0_ref, m0_ref, s0_ref, gb_ref, mln_ref, wa2_ref, ba_ref, gln_ref, tmat_ref,
               sel_ref, *rest):
    ml_ref, gla_ref, c_ref, n_ref, m_ref, s_ref, cs, ns, ms, ss = rest[n_alias:]
    c = pl.program_id(1)
    per = LANES // dk
    HK = H * dk

    @pl.when(c == 0)
    def _():
        cs[...] = c0_ref[0].reshape(bt, HK, dv)
        ns[...] = n0_ref[0]
        ms[...] = m0_ref[0]
        ss[...] = s0_ref[0].reshape(bt, HK, dv)

    t_col = lax.broadcasted_iota(jnp.int32, (L, 1), 0)
    s_row = lax.broadcasted_iota(jnp.int32, (1, L), 1)
    tt = lax.broadcasted_iota(jnp.int32, (L, L), 0)
    sc = lax.broadcasted_iota(jnp.int32, (L, L), 1)
    causal = sc <= tt
    eye = sc == tt
    lane = lax.broadcasted_iota(jnp.int32, (L, LANES), 1)
    lane1 = lax.broadcasted_iota(jnp.int32, (1, LANES), 1)
    laneh = lax.broadcasted_iota(jnp.int32, (1, H), 1)
    rowp = lax.broadcasted_iota(jnp.int32, (LANES, dv), 0)
    ek = lax.broadcasted_iota(jnp.int32, (LANES, LANES), 0) == lax.broadcasted_iota(jnp.int32, (LANES, LANES), 1)
    inhead = [(lane >= u * dk) & (lane < (u + 1) * dk) for u in range(per)]
    upper = tmat_ref[0:L, :]
    scale = dk ** -0.5

    def to_col(row):
        return jnp.sum(jnp.where(eye, row, 0.0), axis=1, keepdims=True)

    psl = lambda p: slice(p * LANES, (p + 1) * LANES)
    hsl = lambda h: slice(h * dv, (h + 1) * dv)
    each = lambda keys, f: {k: f(*k) for k in keys}
    neg_inf = -jnp.inf
    allc = slice(None)
    tile = lambda ref, bi, cols: _seq_tile(ref, bi, cols, L, valid)

    def store(ref, vals, BI):
        for h in range(H):
            _store_seqs(ref, hsl(h), [vals[bi, h] for bi in BI], L, valid, BI[0])

    def mlstm(BI):
        pairs = [(bi, p) for bi in BI for p in range(H // per)]
        heads = [(bi, h) for bi in BI for h in range(H)]
        seqs = lambda f: {bi: f(bi) for bi in BI}
        ig8 = seqs(lambda bi: _rows_of_transpose(sel_ref[...], tile(sm_ref, bi, allc)) + gb_ref[...])
        yield
        lf8 = seqs(lambda bi: _log_sigmoid(ig8[bi]))
        yield
        if valid < L:
            ok = s_row < valid
            ig8 = seqs(lambda bi: jnp.where(ok, ig8[bi], neg_inf))
            lf8 = seqs(lambda bi: jnp.where(ok, lf8[bi], 0.0))
        b8 = seqs(lambda bi: _dot3_right(lf8[bi], upper))
        yield
        m_prev = seqs(lambda bi: ms[bi])
        qk = seqs(lambda bi: tile(qk_ref, bi, allc))
        yield
        Qp = each(pairs, lambda bi, p: qk[bi][:, psl(p)])
        Kp = each(pairs, lambda bi, p: qk[bi][:, HK + p * LANES:HK + (p + 1) * LANES] * scale)
        yield
        Kpb = each(pairs, lambda bi, p: Kp[bi, p].astype(BF16))
        C0p = each(pairs, lambda bi, p: cs[bi, psl(p), :])
        n0p = each(pairs, lambda bi, p: ns[bi][:, psl(p)])
        yield
        ig_row = each(heads, lambda bi, h: ig8[bi][h:h + 1, :])
        b_row = each(heads, lambda bi, h: b8[bi][H + h:H + h + 1, :])
        b_col = each(heads, lambda bi, h: to_col(b_row[bi, h]))
        yield
        ig_col = each(heads, lambda bi, h: to_col(ig_row[bi, h]))
        yield
        D = each(heads, lambda bi, h: jnp.where(causal, b_col[bi, h] - b_row[bi, h] + ig_row[bi, h], neg_inf))
        yield
        g_col = each(heads, lambda bi, h: b_col[bi, h] + m_prev[bi][:, h:h + 1])
        m_col = each(heads, lambda bi, h: jnp.maximum(g_col[bi, h], jnp.max(D[bi, h], axis=1, keepdims=True)))
        yield
        w_intra = each(heads, lambda bi, h: jnp.exp(D[bi, h] - m_col[bi, h]))
        yield
        w_inter = each(heads, lambda bi, h: jnp.exp(g_col[bi, h] - m_col[bi, h]))
        Qh = each(heads, lambda bi, h: jnp.where(inhead[h % per], Qp[bi, h // per], 0.0))
        yield
        Qhb = each(heads, lambda bi, h: Qh[bi, h].astype(BF16))
        vb = each(heads, lambda bi, h: tile(v_ref, bi, hsl(h)).astype(BF16))
        yield
        s = each(heads, lambda bi, h: _dot_nt(Qhb[bi, h], Kpb[bi, h // per]) * w_intra[bi, h])
        yield
        qc = each(heads, lambda bi, h: _dot(Qhb[bi, h], C0p[bi, h // per]))
        yield
        num = each(heads, lambda bi, h: _dot(s[bi, h], vb[bi, h]) + w_inter[bi, h] * qc[bi, h])
        yield
        den = each(heads, lambda bi, h: (jnp.sum(s[bi, h], axis=1, keepdims=True) + w_inter[bi, h]
                                         * jnp.sum(Qh[bi, h] * n0p[bi, h // per], axis=1, keepdims=True)))
        yield
        hh = each(heads, lambda bi, h: num[bi, h] / jnp.maximum(jnp.abs(den[bi, h]), jnp.exp(-m_col[bi, h])))
        yield
        ml = each(heads, lambda bi, h: (_headnorm(hh[bi, h], mln_ref[:, hsl(h)])
                                        * _sigmoid(tile(mo_ref, bi, hsl(h)))))
        yield
        store(ml_ref, ml, BI)
        yield
        mL = each(heads, lambda bi, h: m_col[bi, h][L - 1:L, :])
        wL_col = each(heads, lambda bi, h: jnp.exp(b_col[bi, h][L - 1:L, :] - b_col[bi, h] + ig_col[bi, h] - mL[bi, h]))
        yield
        wL0 = each(heads, lambda bi, h: jnp.exp(g_col[bi, h][L - 1:L, :] - mL[bi, h]))
        kw = each(heads, lambda bi, h: Kp[bi, h // per] * wL_col[bi, h])
        yield
        c_new = each(heads, lambda bi, h: wL0[bi, h] * C0p[bi, h // per] + _dot_tn(kw[bi, h], vb[bi, h]))
        yield
        n_new = each(heads, lambda bi, h: wL0[bi, h] * n0p[bi, h // per] + jnp.sum(kw[bi, h], axis=0, keepdims=True))
        yield
        for bi, p in pairs:
            cs[bi, psl(p), :] = _select_chain(rowp, [c_new[bi, p * per + u] for u in range(per)], dk)
            ns[bi, :, psl(p)] = _select_chain(lane1, [n_new[bi, p * per + u] for u in range(per)], dk)
        yield
        for bi in BI:
            m_new = m_prev[bi]
            for h in range(H):
                m_new = jnp.where(laneh == h, mL[bi, h], m_new)
            ms[bi] = m_new
        yield

    def gla(BI):
        pairs = [(bi, p) for bi in BI for p in range(H // per)]
        heads = [(bi, h) for bi in BI for h in range(H)]
        seqs = lambda f: {bi: f(bi) for bi in BI}
        gqk = seqs(lambda bi: tile(gqk_ref, bi, allc))
        yield
        Q2 = seqs(lambda bi: gqk[bi][:, :HK] * scale)
        K2 = seqs(lambda bi: gqk[bi][:, HK:])
        if valid < L:
            K2 = seqs(lambda bi: jnp.where(t_col < valid, K2[bi], 0.0))
        yield
        la = seqs(lambda bi: _log_sigmoid(_dot(tile(sm_ref, bi, allc), wa2_ref[...]) + ba_ref[...])
                  * (1.0 / GLA_TAU))
        if valid < L:
            la = seqs(lambda bi: jnp.where(t_col < valid, la[bi], 0.0))
        yield
        TL = seqs(lambda bi: _dot3(tmat_ref[L:(2 + nlev) * L, :], la[bi]))
        yield
        Q2b = seqs(lambda bi: Q2[bi].astype(BF16))
        K2b = seqs(lambda bi: K2[bi].astype(BF16))
        yield
        scores = each(heads, lambda bi, h: jnp.where(
            eye, _dot_nt(jnp.where(inhead[h % per], Q2b[bi][:, psl(h // per)], 0), K2b[bi][:, psl(h // per)]), 0.0))
        yield
        for i in range(nlev):
            n = L >> (i + 1)
            second = (t_col & n) != 0
            En = seqs(lambda bi: jnp.exp(TL[bi][(1 + i) * L:(2 + i) * L]))
            yield
            X = seqs(lambda bi: (jnp.where(second, Q2[bi], K2[bi]) * En[bi]).astype(BF16))
            yield
            Xk = seqs(lambda bi: jnp.where(second, 0, X[bi]))
            yield
            sn = each(heads, lambda bi, h: _dot_nt(
                jnp.where(second & inhead[h % per], X[bi][:, psl(h // per)], 0), Xk[bi][:, psl(h // per)]))
            yield
            if i > 0:
                sh = int(round(math.log2(2 * n)))
                same = (tt >> sh) == (sc >> sh)
                sn = each(heads, lambda bi, h: jnp.where(same, sn[bi, h], 0.0))
            scores = each(heads, lambda bi, h: scores[bi, h] + sn[bi, h])
            yield
        A = seqs(lambda bi: TL[bi][0:L])
        AL = seqs(lambda bi: A[bi][L - 1:L, :])
        QA = seqs(lambda bi: (Q2[bi] * jnp.exp(A[bi])).astype(BF16))
        yield
        kd = seqs(lambda bi: (K2[bi] * jnp.exp(AL[bi] - A[bi])).astype(BF16))
        eAL = seqs(lambda bi: jnp.exp(AL[bi]))
        yield
        S0p = each(pairs, lambda bi, p: ss[bi, psl(p), :])
        dec_col = each(pairs, lambda bi, p: jnp.sum(jnp.where(ek, eAL[bi][:, psl(p)], 0.0), axis=1, keepdims=True))
        yield
        v2b = each(heads, lambda bi, h: tile(gv_ref, bi, hsl(h)).astype(BF16))
        qs = each(heads, lambda bi, h: _dot(jnp.where(inhead[h % per], QA[bi][:, psl(h // per)], 0), S0p[bi, h // per]))
        yield
        o = each(heads, lambda bi, h: _dot(scores[bi, h], v2b[bi, h]) + qs[bi, h])
        yield
        gla_out = each(heads, lambda bi, h: (_headnorm(o[bi, h], gln_ref[:, hsl(h)])
                                             * _silu(tile(gg_ref, bi, hsl(h)))))
        yield
        store(gla_ref, gla_out, BI)
        yield
        upd = each(heads, lambda bi, h: _dot_tn(kd[bi][:, psl(h // per)], v2b[bi, h]))
        yield
        for bi, p in pairs:
            ss[bi, psl(p), :] = (dec_col[bi, p] * S0p[bi, p]
                                 + _select_chain(rowp, [upd[bi, p * per + u] for u in range(per)], dk))
        yield

    per_tile = L // valid
    if valid == L and bt >= 4:
        gens = []
        for k in range(bt):
            gk = range(k, k + 1)
            gens.append(itertools.chain(mlstm(gk), gla(gk)) if k % 2 == 0 else itertools.chain(gla(gk), mlstm(gk)))
        _interleave(*gens)
    elif bt % 2 == 0 and (bt // 2) % per_tile == 0:
        ga, gb = range(bt // 2), range(bt // 2, bt)
        _interleave(itertools.chain(mlstm(ga), gla(ga)), itertools.chain(gla(gb), mlstm(gb)))
    else:
        _interleave(mlstm(range(bt)), gla(range(bt)))

    @pl.when(c == nc - 1)
    def _():
        c_ref[0] = cs[...].reshape(bt, H, dk, dv)
        n_ref[0] = ns[...]
        m_ref[0] = ms[...]
        s_ref[0] = ss[...].reshape(bt, H, dk, dv)


def _sel_rows(r):
    return jnp.asarray(np.eye(r, LANES, dtype=np.float32), dtype=BF16)


def _mix_dtype(L, valid):
    return BF16 if valid == L else F32


def _row_blocks(B, nc, L, valid, bt):
    if valid == L:
        return (B, nc * L), lambda w, k: pl.BlockSpec((bt, L, w), lambda b, c: (b, c, k))
    assert nc == 1 and L % valid == 0 and (bt * valid) % L == 0
    return (B // bt, bt * valid), lambda w, k: pl.BlockSpec((1, bt * valid, w), lambda b, c: (b, 0, k))


def _ab_scan(P, B, nc, L, valid, bt, j, states, prev, prm):
    C0, n0, m0, S0 = states
    nst, _, H, dk, dv = C0.shape
    HK = H * dk
    lead, blk = _row_blocks(B, nc, L, valid, bt)
    assert P.shape[:2] == lead and dv == LANES and LANES % dk == 0 and B % bt == 0
    tmat, nlev = _ab_mats(L)
    wv = H * dv
    assert 2 * HK == wv
    nsm = 6 * wv // LANES
    sel = _sel_rows(2 * H)
    sec = lambda k: blk(wv, k)
    st_c = pl.BlockSpec((1, bt, H, dk, dv), lambda b, c: (j, b, 0, 0, 0))
    st_n = pl.BlockSpec((1, bt, 1, HK), lambda b, c: (j, b, 0, 0))
    st_m = pl.BlockSpec((1, bt, 1, H), lambda b, c: (j, b, 0, 0))
    full = lambda a: pl.BlockSpec(a.shape, lambda b, c: (0,) * a.ndim)
    n_alias = 0 if prev is None else 4
    consts = [prm["gate_bias"], prm["ml_norm"], prm["wa2"], prm["ba"], prm["gla_norm"], tmat, sel]
    in_specs = ([sec(k) for k in range(6)]
                + [blk(LANES, nsm), st_c, st_n, st_m, st_c]
                + [full(a) for a in consts]
                + [pl.BlockSpec(memory_space=pl.ANY)] * n_alias)
    out_specs = [sec(0), sec(0), st_c, st_n, st_m, st_c]
    out_shape = [
        jax.ShapeDtypeStruct(lead + (wv,), _mix_dtype(L, valid)),
        jax.ShapeDtypeStruct(lead + (wv,), _mix_dtype(L, valid)),
        jax.ShapeDtypeStruct(C0.shape, F32), jax.ShapeDtypeStruct(n0.shape, F32),
        jax.ShapeDtypeStruct(m0.shape, F32), jax.ShapeDtypeStruct(S0.shape, F32),
    ]
    n_in = len(in_specs) - n_alias
    aliases = {n_in + k: 2 + k for k in range(n_alias)}
    args = [P] * 7 + [C0, n0, m0, S0] + consts + (list(prev) if prev is not None else [])
    return pl.pallas_call(
        functools.partial(_ab_kernel, L, valid, nlev, nc, bt, H, dk, dv, n_alias),
        grid=(B // bt, nc),
        in_specs=in_specs,
        out_specs=out_specs,
        out_shape=out_shape,
        input_output_aliases=aliases,
        scratch_shapes=[pltpu.VMEM((bt, HK, dv), F32), pltpu.VMEM((bt, 1, HK), F32),
                        pltpu.VMEM((bt, 1, H), F32), pltpu.VMEM((bt, HK, dv), F32)],
        compiler_params=pltpu.CompilerParams(
            dimension_semantics=("parallel", "arbitrary"), vmem_limit_bytes=VMEM_LIMIT),
        name="ab_scan",
    )(*args)


def _ssd_kernel(L, valid, nc, bt, G, hpg, hd, wave, preact, n_alias,
                p_ref, cv_ref, h0_ref,
                cw_ref, cb_ref, dtbr_ref, dtbc_ref, alr_ref, alc_ref, dsk_ref, nrm_ref,
                tmat_ref, exp_ref, sel_ref, *rest):
    y_ref, h_ref, catx, catb, catc, hs = rest[n_alias:]
    cidx = pl.program_id(1)
    gw = hpg * hd
    nh = G * hpg
    inner = nh * hd
    N = h0_ref.shape[4]
    W = cw_ref.shape[0]
    P0 = SUBLANES - (W - 1)
    per = LANES // hd
    cats = ((catx, 0, inner), (catb, inner, G * N), (catc, inner + G * N, G * N))

    lay = _SsdLayout(inner, G * N)

    @pl.when(cidx == 0)
    def _():
        if not preact:
            for cat, off, wid in cats:
                cat[:, P0:SUBLANES, :] = cv_ref[0, :, :, off:off + wid]
        hs[...] = h0_ref[0].reshape(bt, inner, N)

    t_col = lax.broadcasted_iota(jnp.int32, (L, 1), 0)
    s_row = lax.broadcasted_iota(jnp.int32, (1, L), 1)
    tt = lax.broadcasted_iota(jnp.int32, (L, L), 0)
    sc = lax.broadcasted_iota(jnp.int32, (L, L), 1)
    causal = sc <= tt
    lane = lax.broadcasted_iota(jnp.int32, (L, LANES), 1)
    upper = tmat_ref[0:L, :]
    lower = tmat_ref[L:2 * L, :]

    def conv_silu(cat, bi, off, lo, wid):
        xall = cat[bi, :, lo:lo + wid]
        cols = slice(off + lo, off + lo + wid)
        acc = cb_ref[:, cols] + xall[SUBLANES:SUBLANES + L] * cw_ref[W - 1:W, cols]
        for w in range(W - 1):
            tap = pltpu.roll(xall, W - 1 - w, axis=0)[SUBLANES:SUBLANES + L]
            acc = acc + tap * cw_ref[w:w + 1, cols]
        return _silu(acc)

    prow = lambda bi, off, wid: _seq_tile(p_ref, bi, slice(off, off + wid), L, valid)
    if not preact:
        assert nc == 1
        for bi in range(bt):
            for g in range(G):
                catx[bi, SUBLANES:SUBLANES + L, g * gw:(g + 1) * gw] = prow(bi, lay.xbc(g * gw), gw)
            catb[bi, SUBLANES:SUBLANES + L, :] = prow(bi, lay.xbc(inner), G * N)
            catc[bi, SUBLANES:SUBLANES + L, :] = prow(bi, lay.xbc(inner + G * N), G * N)

    BI = range(bt)
    each = lambda keys, f: {k: f(*k) for k in keys}
    gsl = lambda g: slice(g * gw, (g + 1) * gw)
    dt_raw = [prow(bi, lay.dt, LANES) for bi in BI]
    dtc = [_softplus(x + dtbr_ref[...]) for x in dt_raw]
    dtr = [_softplus(_rows_of_transpose(sel_ref[...], x) + dtbc_ref[...]) for x in dt_raw]
    if valid < L:
        dtc = [jnp.where(t_col < valid, x, 0.0) for x in dtc]
        dtr = [jnp.where(s_row < valid, x, 0.0) for x in dtr]
    cs_col = [_dot3(lower, x * (-jnp.exp(alr_ref[...]))) for x in dtc]
    cs_row = [_dot3_right(x * (-jnp.exp(alc_ref[...])), upper) for x in dtr]
    cd3 = [_split3(jnp.concatenate([cs_col[bi], dtc[bi]], axis=0)) for bi in BI]

    problems = [(bi, g) for bi in BI for g in range(G)]
    for w0 in range(0, len(problems), wave):
        keys = problems[w0:w0 + wave]
        hkeys = [(bi, g, jl) for bi, g in keys for jl in range(hpg)]
        if preact:
            xa = each(keys, lambda bi, g: prow(bi, lay.xbc(g * gw), gw))
            Bm = each(keys, lambda bi, g: prow(bi, lay.xbc(inner + g * N), N))
            Cm = each(keys, lambda bi, g: prow(bi, lay.xbc(inner + G * N + g * N), N))
            zg = each(keys, lambda bi, g: prow(bi, lay.z(g * gw), gw))
        else:
            xa = each(keys, lambda bi, g: conv_silu(catx, bi, 0, g * gw, gw))
            Bm = each(keys, lambda bi, g: conv_silu(catb, bi, inner, g * N, N))
            Cm = each(keys, lambda bi, g: conv_silu(catc, bi, inner + G * N, g * N, N))
            zg = each(keys, lambda bi, g: _silu(prow(bi, lay.z(g * gw), gw)))
        ce = each(keys, lambda bi, g: (
            jnp.dot(cd3[bi][0], exp_ref[:, gsl(g)], preferred_element_type=F32)
            + jnp.dot(cd3[bi][1], exp_ref[:, gsl(g)], preferred_element_type=F32)
            + jnp.dot(cd3[bi][2], exp_ref[:, gsl(g)], preferred_element_type=F32)))
        cs_exp = each(keys, lambda bi, g: ce[bi, g][0:L])
        dt_exp = each(keys, lambda bi, g: ce[bi, g][L:2 * L])
        Bmb = each(keys, lambda bi, g: Bm[bi, g].astype(BF16))
        Cmb = each(keys, lambda bi, g: Cm[bi, g].astype(BF16))
        CB = each(keys, lambda bi, g: _dot_nt(Cmb[bi, g], Bmb[bi, g]))
        hs0 = each(keys, lambda bi, g: hs[bi, gsl(g), :])
        yc = each(keys, lambda bi, g: _dot_nt(Cmb[bi, g], hs0[bi, g]) * jnp.exp(cs_exp[bi, g]))
        xab = each(keys, lambda bi, g: xa[bi, g].astype(BF16))
        seg = each(hkeys, lambda bi, g, jl: jnp.where(
            causal, cs_col[bi][:, g * hpg + jl:g * hpg + jl + 1] - cs_row[bi][g * hpg + jl:g * hpg + jl + 1, :],
            -jnp.inf))
        mj = each(hkeys, lambda bi, g, jl: CB[bi, g] * (
            jnp.exp(seg[bi, g, jl]) * dtr[bi][g * hpg + jl:g * hpg + jl + 1, :]))
        yh = each(hkeys, lambda bi, g, jl: _dot(
            mj[bi, g, jl], xab[bi, g][:, (jl // per) * LANES:(jl // per + 1) * LANES]))
        yi = each(keys, lambda bi, g: jnp.concatenate(
            [_select_chain(lane, [yh[bi, g, p * per + u] for u in range(per)], hd)
             for p in range(gw // LANES)], axis=1))
        y = each(keys, lambda bi, g: yc[bi, g] + yi[bi, g] + dsk_ref[:, gsl(g)] * xa[bi, g])
        y = each(keys, lambda bi, g: y[bi, g] * zg[bi, g])
        y = each(keys, lambda bi, g: y[bi, g] * lax.rsqrt(
            jnp.mean(y[bi, g] * y[bi, g], axis=1, keepdims=True) + LN_EPS) * nrm_ref[:, gsl(g)])
        if valid == L:
            for bi, g in keys:
                y_ref[bi, :, gsl(g)] = y[bi, g].astype(y_ref.dtype)
        else:
            assert len(keys) == bt * G
            for g in range(G):
                _store_seqs(y_ref, gsl(g), [y[bi, g] for bi in BI], L, valid)

        xw = each(keys, lambda bi, g: xa[bi, g] * (
            jnp.exp(cs_exp[bi, g][L - 1:L, :] - cs_exp[bi, g]) * dt_exp[bi, g]))
        upd = each(keys, lambda bi, g: _dot_tn(xw[bi, g], Bmb[bi, g]))
        dec = each(hkeys, lambda bi, g, jl: jnp.exp(cs_row[bi][g * hpg + jl:g * hpg + jl + 1, L - 1:L]))
        for bi, g, jl in hkeys:
            r0 = g * gw + jl * hd
            hs[bi, r0:r0 + hd, :] = (dec[bi, g, jl] * hs0[bi, g][jl * hd:(jl + 1) * hd, :]
                                     + upd[bi, g][jl * hd:(jl + 1) * hd, :])

    @pl.when(cidx == nc - 1)
    def _():
        h_ref[0] = hs[...].reshape(h_ref.shape[1:])


def _ssd_scan(P, B, nc, L, valid, bt, j, h0, conv0, prev, prm):
    nst, _, nh, hd, N = h0.shape
    G = prm["groups"]
    hpg = nh // G
    gw = hpg * hd
    inner = nh * hd
    GN = G * N
    W1, cd = conv0.shape[2], conv0.shape[3]
    lead, blk = _row_blocks(B, nc, L, valid, bt)
    assert P.shape[:2] == lead and N == LANES and gw % LANES == 0 and B % bt == 0
    assert nh <= LANES and inner % GN == 0 and cd == inner + 2 * GN
    tmat = _ssd_mats(L)
    sel = _sel_rows(nh)
    wave = bt * G if L <= 2 * SUBLANES else 2
    lay = _SsdLayout(inner, GN)
    assert P.shape[2] == 2 * lay.tile and lay.half % gw == 0 and lay.zhalf % gw == 0
    preact = valid == L
    crows = SUBLANES if preact else SUBLANES + L
    full = lambda a: pl.BlockSpec(a.shape, lambda b, c: (0,) * a.ndim)
    st_h = pl.BlockSpec((1, bt, nh, hd, N), lambda b, c: (j, b, 0, 0, 0))
    n_alias = 0 if prev is None else 1
    consts = [prm["conv_w"], prm["conv_b"], prm["dtb_row"], prm["dtb_col"], prm["alog_row"],
              prm["alog_col"], prm["dskip"], prm["norm"], tmat, prm["expand"], sel]
    in_specs = [
        blk(2 * lay.tile, 0),
        pl.BlockSpec((1, bt, W1, cd), lambda b, c: (j, b, 0, 0)),
        st_h,
    ] + [full(a) for a in consts] + [pl.BlockSpec(memory_space=pl.ANY)] * n_alias
    out_specs = [blk(inner, 0), st_h]
    out_shape = [jax.ShapeDtypeStruct(lead + (inner,), _mix_dtype(L, valid)),
                 jax.ShapeDtypeStruct(h0.shape, F32)]
    n_in = len(in_specs) - n_alias
    args = [P, conv0, h0] + consts + ([prev] if prev is not None else [])
    return pl.pallas_call(
        functools.partial(_ssd_kernel, L, valid, nc, bt, G, hpg, hd, wave, preact, n_alias),
        grid=(B // bt, nc),
        in_specs=in_specs,
        out_specs=out_specs,
        out_shape=out_shape,
        input_output_aliases={n_in: 1} if n_alias else {},
        scratch_shapes=[pltpu.VMEM((bt, crows, inner), F32), pltpu.VMEM((bt, crows, GN), F32),
                        pltpu.VMEM((bt, crows, GN), F32), pltpu.VMEM((bt, inner, N), F32)],
        compiler_params=pltpu.CompilerParams(
            dimension_semantics=("parallel", "arbitrary"), vmem_limit_bytes=VMEM_LIMIT),
        name="ssd_scan",
    )(*args)


def _prep_ab(j, H, dk, dv, w_in, ig_bias, fg_bias, ml_norm, wa2, ba, gla_norm, w_out):
    rank = wa2.shape[1]
    qk, vv = H * dk, H * dv
    o = np.cumsum([0, qk, qk, vv, vv, H, H, qk, qk, vv, vv, rank])
    assert int(o[-1]) == w_in.shape[2]
    wj = w_in[j]
    small = jnp.concatenate([wj[:, o[4]:o[6]], wj[:, o[10]:o[11]]], axis=1)
    small = jnp.pad(small, ((0, 0), (0, LANES - small.shape[1])))
    w = jnp.concatenate([wj[:, :o[4]], wj[:, o[6]:o[10]], small], axis=1).astype(BF16)
    wa2p = jnp.zeros((LANES, qk), F32).at[2 * H:2 * H + rank, :].set(wa2[j]).astype(BF16)
    return {
        "w_in": w,
        "gate_bias": jnp.concatenate([ig_bias[j], fg_bias[j]]).astype(F32).reshape(2 * H, 1),
        "ml_norm": ml_norm[j].reshape(1, vv),
        "gla_norm": gla_norm[j].reshape(1, vv),
        "wa2": wa2p,
        "ba": ba[j].reshape(1, qk),
        "w_out_ml": w_out[j][:vv].astype(BF16),
        "w_out_gla": w_out[j][vv:].astype(BF16),
    }


def _prep_ssd(j, G, nh, hd, N, w_in, conv_w, conv_b, dt_bias, a_log, d_skip, norm_g, w_out):
    inner = nh * hd
    hpg = nh // G
    cd = inner + 2 * G * N
    wz, wxbc, wdt = w_in[j][:, :inner], w_in[j][:, inner:inner + cd], w_in[j][:, inner + cd:]
    lay = _SsdLayout(inner, G * N)
    wdt = jnp.pad(wdt, ((0, 0), (0, LANES - nh)))
    w = jnp.concatenate([wxbc[:, :lay.half], wz[:, :lay.zhalf], wdt,
                         wxbc[:, lay.half:], wz[:, lay.zhalf:], jnp.zeros_like(wdt)], axis=1).astype(BF16)
    lane_form = lambda v: jnp.pad(v, (0, LANES - nh)).reshape(1, LANES)
    e = np.zeros((LANES, inner), np.float32)
    for r in range(nh):
        e[r, r * hd:(r + 1) * hd] = 1.0
    return {
        "groups": G,
        "w_in": w,
        "conv_w": conv_w[j], "conv_b": conv_b[j].reshape(1, cd),
        "dtb_row": lane_form(dt_bias[j]), "dtb_col": dt_bias[j].reshape(nh, 1),
        "alog_row": lane_form(a_log[j]), "alog_col": a_log[j].reshape(nh, 1),
        "dskip": jnp.repeat(d_skip[j], hd).reshape(1, inner),
        "norm": norm_g[j].reshape(1, inner),
        "expand": jnp.asarray(e, dtype=BF16),
        "w_out": w_out[j].astype(BF16),
    }


def _largest_divisor(n, cap, step=1):
    return max(d for d in range(step, cap + 1, step) if n % d == 0)


def _plan(B, T, chunk):
    if T % chunk == 0:
        return chunk, T // chunk, _largest_divisor(B, 8), _largest_divisor(B, 4)
    L = max(SUBLANES, 1 << int(math.ceil(math.log2(T))))
    per = L // T
    assert L % T == 0 and B % per == 0
    return L, 1, _largest_divisor(B, 8, per), _largest_divisor(B, 4, per)


def _trunk(x, states, ab_prm, ssd_prm, mlp, lns, alpha, chunk):
    B, T, D = x.shape
    mC, mn, mm, gS, sh, sconv = states
    n_ab, _, H, dk, dv = mC.shape
    L, nc, bt_ab, bt_ssd = _plan(B, T, chunk)
    valid = T if nc == 1 else L
    M = B * T
    tm = 512 if M % 512 == 0 else M
    tmm = 1024 if M % 1024 == 0 else tm
    X = x.reshape(M, D)
    ab_states = (mC, mn.reshape(n_ab, B, 1, H * dk), mm.reshape(n_ab, B, 1, H), gS)
    ab_out, h_out, ncv = None, None, []
    depth = mlp[0].shape[0]
    for l in range(depth):
        j = l // 2
        if l % 2 == 0:
            p = ab_prm[j]
            P = _proj(X, p["w_in"], tm, p["w_in"].shape[1])
            lead = _row_blocks(B, nc, L, valid, bt_ab)[0]
            res = _ab_scan(P.reshape(lead + (-1,)), B, nc, L, valid, bt_ab, j, ab_states, ab_out, p)
            ab_out = res[2:]
            ys = [res[0].reshape(M, -1), res[1].reshape(M, -1)]
            ws = [p["w_out_ml"], p["w_out_gla"]]
        else:
            p = ssd_prm[j]
            inner = p["norm"].shape[1]
            cd = p["conv_b"].shape[1]
            lay = _SsdLayout(inner, (cd - inner) // 2)
            W1 = sconv.shape[2]
            assert T >= W1
            if valid == L:
                tmc = max(t for t in (128, 256, 512) if T % t == 0)
                P, tail = _proj_conv(X, p["w_in"], p["conv_w"], p["conv_b"], sconv, j, B, tmc, lay)
                ncv.append(tail[:, SUBLANES - W1:])
            else:
                P = _proj(X, p["w_in"], tm, lay.tile)
                last = P.reshape(B, T, -1)[:, T - W1:]
                ncv.append(jnp.concatenate([last[:, :, :lay.half], last[:, :, lay.tile:lay.tile + lay.half]], axis=2))
            lead = _row_blocks(B, nc, L, valid, bt_ssd)[0]
            y, h_out = _ssd_scan(P.reshape(lead + (-1,)), B, nc, L, valid, bt_ssd, j, sh, sconv, h_out, p)
            ys = [y.reshape(M, -1)]
            ws = [p["w_out"]]
        X = _outproj_ln(ys, ws, X, lns[0], lns[1], l, alpha, tmm)
        X = _mlp_ln(X, mlp[0], mlp[1], lns[2], lns[3], l, alpha, tmm, 1024)
    nC, nn_, nm, nS = ab_out
    return (X.reshape(B, T, D), nC, nn_.reshape(mn.shape), nm.reshape(mm.shape), nS, h_out, jnp.stack(ncv))


def kernel(x_prompt, x_sample, state_mlstm_C, state_mlstm_n, state_mlstm_m, state_gla_S, state_ssd_h,
           state_ssd_conv, ab_w_in, ab_ig_bias, ab_fg_bias, ab_ml_norm, ab_gla_wa2, ab_gla_ba, ab_gla_norm,
           ab_w_out, ssd_w_in, ssd_conv_w, ssd_conv_b, ssd_dt_bias, ssd_a_log, ssd_d, ssd_norm, ssd_w_out,
           mlp_w1, mlp_w2, ln_mix_g, ln_mix_b, ln_mlp_g, ln_mlp_b):
    depth = mlp_w1.shape[0]
    D = x_prompt.shape[2]
    alpha = (2 * depth) ** 0.25
    n_ab, _, H, dk, dv = state_mlstm_C.shape
    n_ssd, _, nh, hd, N = state_ssd_h.shape
    cd = state_ssd_conv.shape[3]
    G = (cd - nh * hd) // (2 * N)
    ab_prm = [_prep_ab(j, H, dk, dv, ab_w_in, ab_ig_bias, ab_fg_bias, ab_ml_norm, ab_gla_wa2,
                       ab_gla_ba, ab_gla_norm, ab_w_out) for j in range(n_ab)]
    ssd_prm = [_prep_ssd(j, G, nh, hd, N, ssd_w_in, ssd_conv_w, ssd_conv_b, ssd_dt_bias, ssd_a_log,
                         ssd_d, ssd_norm, ssd_w_out) for j in range(n_ssd)]
    mlp = (mlp_w1, mlp_w2)
    lns = tuple(a.reshape(depth, 1, D) for a in (ln_mix_g, ln_mix_b, ln_mlp_g, ln_mlp_b))

    Bp = x_prompt.shape[0]
    zeros = (jnp.zeros((n_ab, Bp, H, dk, dv), F32), jnp.zeros((n_ab, Bp, H, dk), F32),
             jnp.zeros((n_ab, Bp, H), F32), jnp.zeros((n_ab, Bp, H, dk, dv), F32),
             jnp.zeros((n_ssd, Bp, nh, hd, N), F32), jnp.zeros((n_ssd, Bp) + state_ssd_conv.shape[2:], F32))
    carried = (state_mlstm_C, state_mlstm_n, state_mlstm_m, state_gla_S, state_ssd_h, state_ssd_conv)
    chunk = 128
    yp = _trunk(x_prompt, zeros, ab_prm, ssd_prm, mlp, lns, alpha, chunk)
    ys = _trunk(x_sample, carried, ab_prm, ssd_prm, mlp, lns, alpha, chunk)
    return (yp[0], ys[0]) + yp[1:] + ys[1:]
```
